```python
import jax, jax.numpy as jnp
from jax import lax
import numpy as np

D_MODEL = 1024
BATCH = 4
SEQ = 8192
DEPTH = 2

GRID_W = 64
CTX_LEN = 256

GROUP_W = D_MODEL // 4
MIX_W = 4 * GROUP_W

MLA_HEADS = 4
MLA_NOPE = 64
MLA_ROPE = 32
MLA_V = GROUP_W // MLA_HEADS
MLA_Q_LORA = 256
MLA_KV_LORA = 128
MLA_SCALE = (MLA_NOPE + MLA_ROPE) ** -0.5
ROPE_BASE = 10000.0

FOURIER_GROUPS = 4

POOL_WINDOWS = (2, 4, 8, 16)

NA_HEADS = 4
NA_HEAD_DIM = GROUP_W // NA_HEADS
NA_SCALE = NA_HEAD_DIM ** -0.5
NA_WIN_ROWS = 8
NA_WIN_COLS = 16
NA_QCOLS = 16
NA_KCOLS = NA_QCOLS + NA_WIN_COLS

Q_BLOCK = 128
NEG_INF = -1e30

IN_W = MLA_Q_LORA + MLA_KV_LORA + MLA_ROPE + GROUP_W + GROUP_W + 3 * GROUP_W
IN_SPLITS = (MLA_Q_LORA,
             MLA_Q_LORA + MLA_KV_LORA,
             MLA_Q_LORA + MLA_KV_LORA + MLA_ROPE,
             MLA_Q_LORA + MLA_KV_LORA + MLA_ROPE + GROUP_W,
             MLA_Q_LORA + MLA_KV_LORA + MLA_ROPE + 2 * GROUP_W,
             MLA_Q_LORA + MLA_KV_LORA + MLA_ROPE + 3 * GROUP_W,
             MLA_Q_LORA + MLA_KV_LORA + MLA_ROPE + 4 * GROUP_W)

D_FF = 2816
N_EXPERTS = 8
TOP_K = 2
D_FF_EXPERT = 3584
N_DENSE = (DEPTH + 1) // 2
N_MOE = DEPTH // 2

kernel_name = 'hybrid_mla_fourier_pool_natten_moe_diffusion_block'


def _rmsnorm(x, g, eps=1e-6):
    xf = x.astype(jnp.float32)
    y = xf * lax.rsqrt(jnp.mean(xf * xf, axis=-1, keepdims=True) + eps)
    return (y * g.astype(jnp.float32)).astype(x.dtype)


def _modulate(h, shift, scale):
    return h * (1 + scale) + shift


def _axial_rope_tables(n_tokens):
    t = jnp.arange(n_tokens)
    row = (t // GRID_W).astype(jnp.float32)
    col = (t % GRID_W).astype(jnp.float32)
    half = MLA_ROPE // 2
    inv = 1.0 / (ROPE_BASE ** (jnp.arange(0, half, 2, dtype=jnp.float32) / half))
    ang_r = row[:, None] * inv
    ang_c = col[:, None] * inv
    return (jnp.cos(ang_r), jnp.sin(ang_r), jnp.cos(ang_c), jnp.sin(ang_c))


def _rotate(x, cos, sin):
    x1, x2 = jnp.split(x, 2, axis=-1)
    return jnp.concatenate([x1 * cos - x2 * sin, x1 * sin + x2 * cos], axis=-1)


def _apply_axial_rope(x, tabs):
    cr, sr, cc, sc = tabs
    extra = x.ndim - 3
    shp = lambda a: a.reshape(a.shape[0], *([1] * extra), a.shape[1]).astype(x.dtype)
    half = MLA_ROPE // 2
    return jnp.concatenate([_rotate(x[..., :half], shp(cr), shp(sr)),
                            _rotate(x[..., half:], shp(cc), shp(sc))], axis=-1)


def _mla_queries(u_cq, g_cq, w_uq, g_qn, g_qr):
    cq = _rmsnorm(u_cq, g_cq)
    q = (cq @ w_uq).reshape(*u_cq.shape[:-1], MLA_HEADS, MLA_NOPE + MLA_ROPE)
    return _rmsnorm(q[..., :MLA_NOPE], g_qn), _rmsnorm(q[..., MLA_NOPE:], g_qr)


def _mla_keys_values(u_ckv, u_kr, g_ckv, w_ukv, g_kn, g_kr):
    ckv = _rmsnorm(u_ckv, g_ckv)
    kv = (ckv @ w_ukv).reshape(*u_ckv.shape[:-1], MLA_HEADS, MLA_NOPE + MLA_V)
    return _rmsnorm(kv[..., :MLA_NOPE], g_kn), _rmsnorm(u_kr, g_kr), kv[..., MLA_NOPE:]


def _mla_scores(qn, qr, kn, kr):
    s = jnp.einsum('bqhd,bkhd->bhqk', qn, kn) + jnp.einsum('bqhr,bkr->bhqk', qr, kr)
    return s.astype(jnp.float32) * MLA_SCALE


def _mla_latent(qn, qr, kn, kr, v, kn_c, kr_c, v_c):
    B, L = qn.shape[:2]
    n_lat = kn.shape[1]

    def block(i):
        s0 = i * Q_BLOCK
        qn_b = lax.dynamic_slice_in_dim(qn, s0, Q_BLOCK, axis=1)
        qr_b = lax.dynamic_slice_in_dim(qr, s0, Q_BLOCK, axis=1)
        scores = jnp.concatenate([_mla_scores(qn_b, qr_b, kn, kr),
                                  _mla_scores(qn_b, qr_b, kn_c, kr_c)], axis=-1)
        p = jax.nn.softmax(scores, axis=-1).astype(v.dtype)
        return (jnp.einsum('bhqk,bkhd->bqhd', p[..., :n_lat], v)
                + jnp.einsum('bhqk,bkhd->bqhd', p[..., n_lat:], v_c))

    out = lax.map(block, jnp.arange(L // Q_BLOCK))
    return out.transpose(1, 0, 2, 3, 4).reshape(B, L, MLA_HEADS * MLA_V)


def _mla_context(qn_c, qr_c, kn_c, kr_c, v_c):
    B, L = qn_c.shape[:2]
    p = jax.nn.softmax(_mla_scores(qn_c, qr_c, kn_c, kr_c), axis=-1).astype(v_c.dtype)
    return jnp.einsum('bhqk,bkhd->bqhd', p, v_c).reshape(B, L, MLA_HEADS * MLA_V)


def _fourier_mix(u, w_f):
    B, L, C = u.shape
    ug = u.astype(jnp.float32).reshape(B, L, FOURIER_GROUPS, C // FOURIER_GROUPS)
    f = jnp.fft.fftn(ug, axes=(1, 3), norm='ortho').real.reshape(B, L, C)
    return f.astype(u.dtype) @ w_f


def _multiscale_pool(u, w_pool, p_scale):
    B, L, C = u.shape
    G = len(POOL_WINDOWS)
    cg = C // G
    ug = u.reshape(B, L, G, cg)
    csum = jnp.concatenate([jnp.zeros((B, 1, G, cg), jnp.float32),
                            jnp.cumsum(ug.astype(jnp.float32), axis=1)], axis=1)
    w = jnp.array(POOL_WINDOWS, dtype=jnp.int32)
    t = jnp.arange(L, dtype=jnp.int32)[:, None]
    lo = jnp.clip(t - w // 2, 0, L)
    hi = jnp.clip(t + w - w // 2, 0, L)
    gidx = jnp.arange(G)[None, :]
    win_sum = csum[:, hi, gidx] - csum[:, lo, gidx]
    count = (hi - lo).astype(jnp.float32)[None, :, :, None]
    pooled = (win_sum / count).astype(u.dtype) - ug
    y = jnp.einsum('blgc,gcd->blgd', pooled, w_pool).reshape(B, L, C)
    return y * p_scale


def _heads(u):
    return u.reshape(*u.shape[:-1], NA_HEADS, NA_HEAD_DIM)


def _na_latent(q, k, v, k_c, v_c, rpb):
    B, L = q.shape[:2]
    rows = L // GRID_W
    wr = min(NA_WIN_ROWS, rows)
    ncb = GRID_W // NA_QCOLS
    qg = q.reshape(B, rows, GRID_W, NA_HEADS, NA_HEAD_DIM)
    kg = k.reshape(B, rows, GRID_W, NA_HEADS, NA_HEAD_DIM)
    vg = v.reshape(B, rows, GRID_W, NA_HEADS, NA_HEAD_DIM)
    qcol = jnp.arange(GRID_W).reshape(ncb, NA_QCOLS)
    win_c0 = jnp.clip(qcol - NA_WIN_COLS // 2, 0, GRID_W - NA_WIN_COLS)
    band_c0 = jnp.clip(qcol[:, 0] - NA_WIN_COLS // 2, 0, GRID_W - NA_KCOLS)
    kcol = band_c0[:, None] + jnp.arange(NA_KCOLS)
    col_ok = ((kcol[:, None, :] >= win_c0[..., None])
              & (kcol[:, None, :] < win_c0[..., None] + NA_WIN_COLS))
    col_off = jnp.clip(kcol[:, None, :] - qcol[..., None] + NA_WIN_COLS - 1,
                       0, 2 * NA_WIN_COLS - 2)
    n_loc = wr * NA_KCOLS

    def row_step(r):
        r0 = jnp.clip(r - wr // 2, 0, rows - wr)
        q_blk = lax.dynamic_index_in_dim(qg, r, axis=1, keepdims=False)
        q_blk = q_blk.reshape(B, ncb, NA_QCOLS, NA_HEADS, NA_HEAD_DIM)
        k_blk = lax.dynamic_slice_in_dim(kg, r0, wr, axis=1)[:, :, kcol]
        v_blk = lax.dynamic_slice_in_dim(vg, r0, wr, axis=1)[:, :, kcol]
        row_off = r0 + jnp.arange(wr) - r + NA_WIN_ROWS - 1
        bias = rpb[:, row_off[None, None, :, None], col_off[:, :, None, :]]
        s_loc = (jnp.einsum('bnqhd,brnkhd->bhnqrk', q_blk, k_blk).astype(jnp.float32) * NA_SCALE
                 + bias.astype(jnp.float32))
        s_loc = jnp.where(col_ok[:, :, None, :], s_loc, NEG_INF)
        s_loc = s_loc.reshape(B, NA_HEADS, ncb, NA_QCOLS, n_loc)
        s_ctx = jnp.einsum('bnqhd,bchd->bhnqc', q_blk, k_c).astype(jnp.float32) * NA_SCALE
        p = jax.nn.softmax(jnp.concatenate([s_loc, s_ctx], axis=-1), axis=-1).astype(v.dtype)
        p_loc = p[..., :n_loc].reshape(B, NA_HEADS, ncb, NA_QCOLS, wr, NA_KCOLS)
        o = (jnp.einsum('bhnqrk,brnkhd->bnqhd', p_loc, v_blk)
             + jnp.einsum('bhnqc,bchd->bnqhd', p[..., n_loc:], v_c))
        return o.reshape(B, GRID_W, NA_HEADS * NA_HEAD_DIM)

    out = lax.map(row_step, jnp.arange(rows))
    return out.transpose(1, 0, 2, 3).reshape(B, L, NA_HEADS * NA_HEAD_DIM)


def _dense_attn(q, k, v):
    B, L = q.shape[:2]
    s = jnp.einsum('bqhd,bkhd->bhqk', q, k).astype(jnp.float32) * NA_SCALE
    p = jax.nn.softmax(s, axis=-1).astype(v.dtype)
    return jnp.einsum('bhqk,bkhd->bqhd', p, v).reshape(B, L, NA_HEADS * NA_HEAD_DIM)


def _swiglu(h, w1, w3, w2):
    return (jax.nn.silu(h @ w1) * (h @ w3)) @ w2


def _moe(h, w_router, w1, w3, w2):
    logits = (h @ w_router).astype(jnp.float32)
    top_v, top_i = lax.top_k(logits, TOP_K)
    top_g = jax.nn.softmax(top_v, axis=-1)
    gate = jnp.sum(top_g[..., None] * jax.nn.one_hot(top_i, N_EXPERTS, dtype=jnp.float32),
                   axis=-2).astype(h.dtype)
    y = jnp.zeros_like(h)
    for e in range(N_EXPERTS):
        y = y + gate[..., e:e + 1] * _swiglu(h, w1[e], w3[e], w2[e])
    return y


def setup_inputs(seed: int = 0) -> dict:
    key = jax.random.key(seed)
    ks = iter(jax.random.split(key, 48))
    f32 = jnp.float32

    def nrm(shape, scale):
        return jax.random.normal(next(ks), shape, f32) * scale

    def gain(shape):
        return 1.0 + 0.1 * jax.random.normal(next(ks), shape, f32)

    D = D_MODEL
    cg = GROUP_W // len(POOL_WINDOWS)
    return {
        'x': nrm((BATCH, SEQ, D), 1.0),
        'c': nrm((BATCH, D), 1.0),
        'ctx': nrm((BATCH, CTX_LEN, D), 1.0),
        'c_ctx': nrm((D,), 1.0),
        'w_ada': nrm((DEPTH, D, 6 * D), D ** -0.5),
        'b_ada': nrm((DEPTH, 6 * D), 0.02),
        'g_mix': gain((DEPTH, D)),
        'g_ffn': gain((DEPTH, D)),
        'w_in': nrm((DEPTH, D, IN_W), D ** -0.5),
        'w_out': nrm((DEPTH, MIX_W, D), MIX_W ** -0.5),
        'g_cq': gain((DEPTH, MLA_Q_LORA)),
        'g_ckv': gain((DEPTH, MLA_KV_LORA)),
        'w_uq': nrm((DEPTH, MLA_Q_LORA, MLA_HEADS * (MLA_NOPE + MLA_ROPE)), MLA_Q_LORA ** -0.5),
        'w_ukv': nrm((DEPTH, MLA_KV_LORA, MLA_HEADS * (MLA_NOPE + MLA_V)), MLA_KV_LORA ** -0.5),
        'g_mla_qn': gain((DEPTH, MLA_NOPE)),
        'g_mla_qr': gain((DEPTH, MLA_ROPE)),
        'g_mla_kn': gain((DEPTH, MLA_NOPE)),
        'g_mla_kr': gain((DEPTH, MLA_ROPE)),
        'w_fourier': nrm((DEPTH, GROUP_W, GROUP_W), GROUP_W ** -0.5),
        'w_pool': nrm((DEPTH, len(POOL_WINDOWS), cg, cg), cg ** -0.5),
        'pool_scale': gain((DEPTH, GROUP_W)),
        'g_na_q': gain((DEPTH, NA_HEAD_DIM)),
        'g_na_k': gain((DEPTH, NA_HEAD_DIM)),
        'na_rpb': nrm((DEPTH, NA_HEADS, 2 * NA_WIN_ROWS - 1, 2 * NA_WIN_COLS - 1), 0.1),
        'w1_dense': nrm((N_DENSE, D, D_FF), D ** -0.5),
        'w3_dense': nrm((N_DENSE, D, D_FF), D ** -0.5),
        'w2_dense': nrm((N_DENSE, D_FF, D), D_FF ** -0.5),
        'w_router': nrm((N_MOE, D, N_EXPERTS), D ** -0.5),
        'w1_moe': nrm((N_MOE, N_EXPERTS, D, D_FF_EXPERT), D ** -0.5),
        'w3_moe': nrm((N_MOE, N_EXPERTS, D, D_FF_EXPERT), D ** -0.5),
        'w2_moe': nrm((N_MOE, N_EXPERTS, D_FF_EXPERT, D), D_FF_EXPERT ** -0.5),
    }


def reference(x, c, ctx, c_ctx, w_ada, b_ada, g_mix, g_ffn, w_in, w_out,
              g_cq, g_ckv, w_uq, w_ukv, g_mla_qn, g_mla_qr, g_mla_kn, g_mla_kr,
              w_fourier, w_pool, pool_scale, g_na_q, g_na_k, na_rpb,
              w1_dense, w3_dense, w2_dense, w_router, w1_moe, w3_moe, w2_moe):
    L = x.shape[1]
    tabs = _axial_rope_tables(L)
    silu_c = jax.nn.silu(c)
    silu_cc = jax.nn.silu(c_ctx)
    for l in range(DEPTH):
        last = l == DEPTH - 1
        mx = (silu_c @ w_ada[l] + b_ada[l])[:, None, :]
        mc = silu_cc @ w_ada[l] + b_ada[l]
        sh1, sc1, ga1, sh2, sc2, ga2 = jnp.split(mx, 6, axis=-1)
        csh1, csc1, cga1, csh2, csc2, cga2 = jnp.split(mc, 6, axis=-1)

        ux = _modulate(_rmsnorm(x, g_mix[l]), sh1, sc1) @ w_in[l]
        uc = _modulate(_rmsnorm(ctx, g_mix[l]), csh1, csc1) @ w_in[l]
        x_cq, x_ckv, x_kr, x_f, x_p, x_nq, x_nk, x_nv = jnp.split(ux, IN_SPLITS, axis=-1)
        c_cq, c_ckv, c_kr, c_f, c_p, c_nq, c_nk, c_nv = jnp.split(uc, IN_SPLITS, axis=-1)

        kn_c, kr_c, v_c = _mla_keys_values(c_ckv, c_kr, g_ckv[l], w_ukv[l], g_mla_kn[l], g_mla_kr[l])
        nk_c = _rmsnorm(_heads(c_nk), g_na_k[l])
        nv_c = _heads(c_nv)

        qn, qr = _mla_queries(x_cq, g_cq[l], w_uq[l], g_mla_qn[l], g_mla_qr[l])
        qr = _apply_axial_rope(qr, tabs)
        kn, kr, v = _mla_keys_values(x_ckv, x_kr, g_ckv[l], w_ukv[l], g_mla_kn[l], g_mla_kr[l])
        kr = _apply_axial_rope(kr, tabs)
        nq = _rmsnorm(_heads(x_nq), g_na_q[l])
        nk = _rmsnorm(_heads(x_nk), g_na_k[l])
        o_x = jnp.concatenate([
            _mla_latent(qn, qr, kn, kr, v, kn_c, kr_c, v_c),
            _fourier_mix(x_f, w_fourier[l]),
            _multiscale_pool(x_p, w_pool[l], pool_scale[l]),
            _na_latent(nq, nk, _heads(x_nv), nk_c, nv_c, na_rpb[l]),
        ], axis=-1)
        x = x + ga1 * (o_x @ w_out[l])

        if not last:
            qn_c, qr_c = _mla_queries(c_cq, g_cq[l], w_uq[l], g_mla_qn[l], g_mla_qr[l])
            o_c = jnp.concatenate([
                _mla_context(qn_c, qr_c, kn_c, kr_c, v_c),
                _fourier_mix(c_f, w_fourier[l]),
                _multiscale_pool(c_p, w_pool[l], pool_scale[l]),
                _dense_attn(_rmsnorm(_heads(c_nq), g_na_q[l]), nk_c, nv_c),
            ], axis=-1)
            ctx = ctx + cga1 * (o_c @ w_out[l])

        i = l // 2
        if l % 2 == 0:
            ffn = lambda h, i=i: _swiglu(h, w1_dense[i], w3_dense[i], w2_dense[i])
        else:
            ffn = lambda h, i=i: _moe(h, w_router[i], w1_moe[i], w3_moe[i], w2_moe[i])
        x = x + ga2 * ffn(_modulate(_rmsnorm(x, g_ffn[l]), sh2, sc2))
        if not last:
            ctx = ctx + cga2 * ffn(_modulate(_rmsnorm(ctx, g_ffn[l]), csh2, csc2))
    return x
```

```python
import functools
import math

import numpy as np
import jax
import jax.numpy as jnp
from jax import lax
from jax.experimental import pallas as pl
from jax.experimental.pallas import tpu as pltpu

F32 = jnp.float32
BF16 = jnp.bfloat16

GRID_W = 64
LANES = 128
EPS = 1e-6
NEG_INF = -1e30

MLA_HEADS = 4
MLA_NOPE = 64
MLA_ROPE = 32
MLA_V = 64
MLA_SCALE = (MLA_NOPE + MLA_ROPE) ** -0.5
ROPE_BASE = 10000.0

FOURIER_GROUPS = 4
POOL_WINDOWS = (2, 4, 8, 16)
POOL_HALO = 8

NA_HEADS = 4
NA_HEAD_DIM = 64
NA_SCALE = NA_HEAD_DIM ** -0.5
NA_WIN_ROWS = 8
NA_WIN_COLS = 16
NA_QROWS = 8
NA_KROWS = 16

TOP_K = 2
VMEM_LIMIT = 56 * 1024 * 1024


def _cparams(sem, vmem=None):
    return pltpu.CompilerParams(dimension_semantics=sem, vmem_limit_bytes=vmem)


def _silu(a):
    return a / (1.0 + jnp.exp(-a))


def _rms(v, n):
    return lax.rsqrt(jnp.sum(v * v, axis=-1, keepdims=True) * (1.0 / n) + EPS)


def _adaln_kernel(c_ref, w_ref, b_ref, o_ref):
    o_ref[0] = jnp.dot(_silu(c_ref[...]), w_ref[0], preferred_element_type=F32) + b_ref[0]


def _adaln(cc, w_ada, b_ada):
    depth, d, d6 = w_ada.shape
    tn = 512
    return pl.pallas_call(
        _adaln_kernel,
        grid=(depth, d6 // tn),
        in_specs=[pl.BlockSpec((8, d), lambda l, j: (0, 0)),
                  pl.BlockSpec((1, d, tn), lambda l, j: (l, 0, j)),
                  pl.BlockSpec((1, 1, tn), lambda l, j: (l, 0, j))],
        out_specs=pl.BlockSpec((1, 8, tn), lambda l, j: (l, 0, j)),
        out_shape=jax.ShapeDtypeStruct((depth, 8, d6), F32),
        compiler_params=_cparams(("arbitrary", "arbitrary")),
        name="adaln",
    )(cc, w_ada, b_ada.reshape(depth, 1, d6))


IN_COLS = 1920


def _rope_swap_perm():
    j = np.arange(MLA_ROPE)
    return np.where((j % 16) < 8, j + 8, j - 8)


def _prep_in_weights(w_in_l):
    d = w_in_l.shape[0]
    s = [0, 256, 384, 416, 672, 928, 1184, 1440, 1696]
    cq, ckv, kr, f, p, nq, nk, nv = [w_in_l[:, s[i]:s[i + 1]] for i in range(8)]
    z = lambda n: jnp.zeros((d, n), w_in_l.dtype)
    krsw = kr[:, _rope_swap_perm()]
    return jnp.concatenate([cq, ckv, f, p, nq, nk, nv, z(64), kr, z(32), z(64), krsw, z(32)],
                           axis=1).astype(BF16)


def _prep_uq(w_uq_l):
    r = w_uq_l.shape[0]
    w = w_uq_l.reshape(r, MLA_HEADS, MLA_NOPE + MLA_ROPE)
    z = lambda n: jnp.zeros((r, MLA_HEADS, n), w.dtype)
    main = jnp.concatenate([w, z(32)], axis=-1)
    sw = jnp.concatenate([z(64), w[..., MLA_NOPE:][..., _rope_swap_perm()], z(32)], axis=-1)
    return jnp.concatenate([main.reshape(r, -1), sw.reshape(r, -1)], axis=1).astype(BF16)


def _prep_ukv(w_ukv_l):
    r = w_ukv_l.shape[0]
    w = w_ukv_l.reshape(r, MLA_HEADS, MLA_NOPE + MLA_V)
    z = jnp.zeros((r, 64), w.dtype)
    tiles = [jnp.concatenate([w[:, h, :MLA_NOPE], z], axis=1) for h in range(MLA_HEADS)]
    for h in range(MLA_HEADS):
        v = w[:, h, MLA_NOPE:]
        tiles.append(jnp.concatenate([v, z] if h % 2 == 0 else [z, v], axis=1))
    return jnp.concatenate(tiles, axis=1).astype(BF16)


def _lane_vec(parts):
    cols = [jnp.zeros((n,), F32) if a is None else a.astype(F32) for a, n in parts]
    return jnp.concatenate(cols).reshape(1, -1)


def _rope_tables(n_tokens):
    t = np.arange(n_tokens)
    pos = np.stack([t // GRID_W, t % GRID_W], axis=1).astype(np.float64)
    half = MLA_ROPE // 2
    inv = 1.0 / (ROPE_BASE ** (np.arange(0, half, 2, dtype=np.float64) / half))
    j = np.arange(MLA_ROPE)
    ang = pos[:, j // 16] * inv[j % 8]
    sign = np.where((j % 16) < 8, -1.0, 1.0)
    cos = np.ones((n_tokens, LANES), np.float32)
    sin = np.zeros((n_tokens, LANES), np.float32)
    cos[:, 64:96] = np.cos(ang)
    sin[:, 64:96] = np.sin(ang) * sign
    return jnp.asarray(cos), jnp.asarray(sin)


def _inproj_kernel(x_ref, sh_ref, sc_ref, g_ref, cos_ref, sin_ref, win_ref,
                   gcq_ref, wuq_ref, gq_ref, gqsw_ref,
                   gckv_ref, wukv_ref, gkn_ref, gkr_ref, gkrsw_ref, gnq_ref, gnk_ref,
                   q_ref, k_ref, v_ref, f_ref, p_ref, nq_ref, nk_ref, nv_ref):
    x = x_ref[0]
    d = x.shape[-1]
    h = (x * _rms(x, d)) * g_ref[...]
    h = h * (1.0 + sc_ref[0]) + sh_ref[0]
    u = jnp.dot(h.astype(BF16), win_ref[...], preferred_element_type=F32)
    cq, ckv = u[:, 0:256], u[:, 256:384]
    f_ref[0] = u[:, 384:640].astype(BF16)
    p_ref[0] = u[:, 640:896]
    nq, nk, nv = u[:, 896:1152], u[:, 1152:1408], u[:, 1408:1664]
    krb, krs = u[:, 1664:1792], u[:, 1792:1920]
    cos, sin = cos_ref[...], sin_ref[...]
    lane = lax.broadcasted_iota(jnp.int32, (1, LANES), 1)
    m_nope = lane < MLA_NOPE
    m_rope = jnp.logical_and(lane >= MLA_NOPE, lane < MLA_NOPE + MLA_ROPE)

    cqn = (cq * _rms(cq, 256) * gcq_ref[...]).astype(BF16)
    qall = jnp.dot(cqn, wuq_ref[...], preferred_element_type=F32)
    for hd in range(MLA_HEADS):
        blk = qall[:, hd * LANES:(hd + 1) * LANES]
        sw = qall[:, (MLA_HEADS + hd) * LANES:(MLA_HEADS + hd + 1) * LANES]
        sq = blk * blk
        rn = lax.rsqrt(jnp.sum(jnp.where(m_nope, sq, 0.0), -1, keepdims=True) * (1.0 / MLA_NOPE) + EPS)
        rr = lax.rsqrt(jnp.sum(jnp.where(m_rope, sq, 0.0), -1, keepdims=True) * (1.0 / MLA_ROPE) + EPS)
        qh = blk * jnp.where(m_nope, rn, rr) * gq_ref[...]
        qs = sw * rr * gqsw_ref[...]
        q_ref[0, hd] = ((qh * cos + qs * sin) * MLA_SCALE).astype(BF16)

    ckvn = (ckv * _rms(ckv, 128) * gckv_ref[...]).astype(BF16)
    kvall = jnp.dot(ckvn, wukv_ref[...], preferred_element_type=F32)
    rkr = _rms(krb, MLA_ROPE)
    krot = (krb * rkr * gkr_ref[...]) * cos + (krs * rkr * gkrsw_ref[...]) * sin
    for hd in range(MLA_HEADS):
        blk = kvall[:, hd * LANES:(hd + 1) * LANES]
        k_ref[0, hd] = (blk * _rms(blk, MLA_NOPE) * gkn_ref[...] + krot).astype(BF16)
        vb = kvall[:, (MLA_HEADS + hd) * LANES:(MLA_HEADS + hd + 1) * LANES]
        one_lane = 64 if hd % 2 == 0 else 0
        v_ref[0, hd] = jnp.where(lane == one_lane, 1.0, vb).astype(BF16)

    lane2 = lax.broadcasted_iota(jnp.int32, (1, 2 * LANES), 1)

    def seg_rms(t):
        sq = t * t
        r = jnp.zeros_like(t)
        for s in range(NA_HEADS):
            m = (lane2 // NA_HEAD_DIM) == s
            ss = jnp.sum(jnp.where(m, sq, 0.0), -1, keepdims=True)
            r = jnp.where(m, lax.rsqrt(ss * (1.0 / NA_HEAD_DIM) + EPS), r)
        return r

    nqn = nq * seg_rms(nq) * gnq_ref[...] * NA_SCALE
    nkn = nk * seg_rms(nk) * gnk_ref[...]
    for hd in range(NA_HEADS):
        t0 = (hd // 2) * LANES
        own = (lane < 64) if hd % 2 == 0 else (lane >= 64)
        one_lane = 64 if hd % 2 == 0 else 0
        nq_ref[0, hd] = jnp.where(own, nqn[:, t0:t0 + LANES], 0.0).astype(BF16)
        nk_ref[0, hd] = jnp.where(own, nkn[:, t0:t0 + LANES], 0.0).astype(BF16)
        nv_ref[0, hd] = jnp.where(own, nv[:, t0:t0 + LANES],
                                  jnp.where(lane == one_lane, 1.0, 0.0)).astype(BF16)


def _inproj(x, sh, sc, g, cos, sin, wts, tm):
    b, n, d = x.shape
    heads = MLA_HEADS
    row = lambda a: pl.BlockSpec(a.shape, lambda i, j: (0,) * a.ndim)
    per_b = pl.BlockSpec((1, 1, d), lambda i, j: (i, 0, 0))
    tok = lambda w: pl.BlockSpec((1, tm, w), lambda i, j: (i, j, 0))
    hd_spec = pl.BlockSpec((1, heads, tm, LANES), lambda i, j: (i, 0, j, 0))
    tab = pl.BlockSpec((tm, LANES), lambda i, j: (j, 0))
    hshape = jax.ShapeDtypeStruct((b, heads, n, LANES), BF16)
    return pl.pallas_call(
        _inproj_kernel,
        grid=(b, n // tm),
        in_specs=[tok(d), per_b, per_b, row(g), tab, tab] + [row(w) for w in wts],
        out_specs=[hd_spec, hd_spec, hd_spec, tok(256), tok(256), hd_spec, hd_spec, hd_spec],
        out_shape=[hshape, hshape, hshape,
                   jax.ShapeDtypeStruct((b, n, 256), BF16), jax.ShapeDtypeStruct((b, n, 256), F32),
                   hshape, hshape, hshape],
        compiler_params=_cparams(("parallel", "parallel"), VMEM_LIMIT),
        name="inproj",
    )(x, sh, sc, g, cos, sin, *wts)


def _flash_pair_kernel(q_ref, k_ref, v_ref, o_ref, *, tk, nk):
    lane = lax.broadcasted_iota(jnp.int32, (1, LANES), 1)
    tq = q_ref.shape[2]
    out = None
    for hh in range(2):
        q = q_ref[0, hh]

        def body(j, carry, hh=hh, q=q):
            m, acc = carry
            start = pl.multiple_of(j * tk, tk)
            ks = k_ref[0, hh, pl.ds(start, tk), :]
            vs = v_ref[0, hh, pl.ds(start, tk), :]
            s = lax.dot_general(q, ks, (((1,), (1,)), ((), ())), preferred_element_type=F32)
            m_new = jnp.maximum(m, jnp.max(s, axis=-1, keepdims=True))
            p = jnp.exp(s - m_new)
            acc = acc * jnp.exp(m - m_new) + jnp.dot(p.astype(BF16), vs, preferred_element_type=F32)
            return m_new, acc

        m0 = jnp.full((tq, 1), NEG_INF, F32)
        _, acc = lax.fori_loop(0, nk, body, (m0, jnp.zeros((tq, LANES), F32)))
        one_lane = 64 if hh == 0 else 0
        own = (lane < 64) if hh == 0 else (lane >= 64)
        o = jnp.where(own, acc / acc[:, one_lane:one_lane + 1], 0.0)
        out = o if out is None else out + o
    o_ref[0] = out


def _flash_pairs(q, k, v, tq):
    b, heads, lq, _ = q.shape
    lk = k.shape[2]
    tk = 768 if lk % 768 == 0 else 256
    kv_spec = pl.BlockSpec((1, 2, lk, LANES), lambda i, p, j: (i, p, 0, 0))
    return pl.pallas_call(
        functools.partial(_flash_pair_kernel, tk=tk, nk=lk // tk),
        grid=(b, heads // 2, lq // tq),
        in_specs=[pl.BlockSpec((1, 2, tq, LANES), lambda i, p, j: (i, p, j, 0)), kv_spec, kv_spec],
        out_specs=pl.BlockSpec((1, tq, LANES), lambda i, p, j: (i, j, p)),
        out_shape=jax.ShapeDtypeStruct((b, lq, heads * 64), F32),
        compiler_params=_cparams(("parallel", "parallel", "parallel"), VMEM_LIMIT),
        name="flash_pairs",
    )(q, k, v)


def _table(a):
    return jnp.asarray(a, F32).astype(BF16)


def _dft_consts(n_tokens):
    rows = n_tokens // GRID_W
    r = np.arange(rows)
    ang_r = 2 * np.pi * np.outer(r, r) / rows
    a_mat = np.concatenate([np.cos(ang_r), -np.sin(ang_r)], axis=0)
    c = np.arange(GRID_W)
    ang_t = 2 * np.pi * np.outer(r, c) / n_tokens
    ang_c = 2 * np.pi * np.outer(c, c) / GRID_W
    cc, sc = np.cos(ang_c), np.sin(ang_c)
    m_b = np.block([[cc, sc], [-sc, cc]])
    return a_mat, np.cos(ang_t), np.sin(ang_t), m_b


def _channel_dft(n_tokens, width):
    cg = width // FOURIER_GROUPS
    j = np.arange(cg)
    ang = 2 * np.pi * np.outer(j, j) / cg
    eye = np.eye(FOURIER_GROUPS)
    scale = 1.0 / math.sqrt(n_tokens * cg)
    return np.kron(eye, np.cos(ang)) * scale, np.kron(eye, np.sin(ang)) * scale


def _fourier_rows_kernel(u_ref, a_ref, tc_ref, ts_ref, o_ref):
    z = jnp.dot(a_ref[...], u_ref[0], preferred_element_type=F32)
    r = z.shape[0] // 2
    zr, zi = z[:r], z[r:]
    c, s = tc_ref[...], ts_ref[...]
    o_ref[0, :r] = (zr * c + zi * s).astype(BF16)
    o_ref[0, r:] = (zi * c - zr * s).astype(BF16)


def _fourier_cols_kernel(z_ref, mb_ref, cs_ref, wf_ref, o_ref):
    kb = z_ref.shape[1]
    w = GRID_W
    cw = wf_ref.shape[0]
    for t in range(kb):
        y = jnp.dot(mb_ref[...], z_ref[0, t], preferred_element_type=F32)
        f = (jnp.dot(y[:w].astype(BF16), cs_ref[:cw], preferred_element_type=F32)
             + jnp.dot(y[w:].astype(BF16), cs_ref[cw:], preferred_element_type=F32))
        o_ref[0, t] = jnp.dot(f.astype(BF16), wf_ref[...], preferred_element_type=F32)


def _fourier_latent(uf, wf):
    b, n, cw = uf.shape
    rows = n // GRID_W
    a_mat, tw_c, tw_s, m_b = _dft_consts(n)
    cc, sc = _channel_dft(n, cw)
    wide = GRID_W * cw
    tn = 2048
    expand = lambda t: jnp.broadcast_to(jnp.asarray(t, F32)[:, :, None], (rows, GRID_W, cw)).reshape(rows, wide)
    z = pl.pallas_call(
        _fourier_rows_kernel,
        grid=(wide // tn, b),
        in_specs=[pl.BlockSpec((1, rows, tn), lambda j, i: (i, 0, j)),
                  pl.BlockSpec((2 * rows, rows), lambda j, i: (0, 0)),
                  pl.BlockSpec((rows, tn), lambda j, i: (0, j)),
                  pl.BlockSpec((rows, tn), lambda j, i: (0, j))],
        out_specs=pl.BlockSpec((1, 2 * rows, tn), lambda j, i: (i, 0, j)),
        out_shape=jax.ShapeDtypeStruct((b, 2 * rows, wide), BF16),
        compiler_params=_cparams(("parallel", "parallel")),
        name="fourier_rows",
    )(uf.reshape(b, rows, wide), _table(a_mat), expand(tw_c), expand(tw_s))
    z = z.reshape(b, 2, rows, GRID_W, cw).transpose(0, 2, 1, 3, 4).reshape(b, rows, 2 * GRID_W, cw)
    kb = 8
    y = pl.pallas_call(
        _fourier_cols_kernel,
        grid=(b, rows // kb),
        in_specs=[pl.BlockSpec((1, kb, 2 * GRID_W, cw), lambda i, j: (i, j, 0, 0)),
                  pl.BlockSpec((2 * GRID_W, 2 * GRID_W), lambda i, j: (0, 0)),
                  pl.BlockSpec((2 * cw, cw), lambda i, j: (0, 0)),
                  pl.BlockSpec((cw, cw), lambda i, j: (0, 0))],
        out_specs=pl.BlockSpec((1, kb, GRID_W, cw), lambda i, j: (i, j, 0, 0)),
        out_shape=jax.ShapeDtypeStruct((b, rows, GRID_W, cw), F32),
        compiler_params=_cparams(("parallel", "parallel")),
        name="fourier_cols",
    )(z, _table(m_b), _table(np.concatenate([cc, sc], axis=0)), wf)
    return y.transpose(0, 2, 1, 3).reshape(b, n, cw)


def _fourier_dense_kernel(u_ref, cl_ref, sl_ref, cs_ref, wf_ref, o_ref):
    u = u_ref[0]
    cw = wf_ref.shape[0]
    a = jnp.dot(u, cs_ref[:cw], preferred_element_type=F32).astype(BF16)
    bb = jnp.dot(u, cs_ref[cw:], preferred_element_type=F32).astype(BF16)
    f = (jnp.dot(cl_ref[...], a, preferred_element_type=F32)
         - jnp.dot(sl_ref[...], bb, preferred_element_type=F32))
    o_ref[0] = jnp.dot(f.astype(BF16), wf_ref[...], preferred_element_type=F32)


def _fourier_dense(uf, wf):
    b, n, cw = uf.shape
    t = np.arange(n)
    ang = 2 * np.pi * np.outer(t, t) / n
    cc, sc = _channel_dft(n, cw)
    full = lambda shape: pl.BlockSpec(shape, lambda i: (0,) * len(shape))
    return pl.pallas_call(
        _fourier_dense_kernel,
        grid=(b,),
        in_specs=[pl.BlockSpec((1, n, cw), lambda i: (i, 0, 0)), full((n, n)), full((n, n)),
                  full((2 * cw, cw)), full((cw, cw))],
        out_specs=pl.BlockSpec((1, n, cw), lambda i: (i, 0, 0)),
        out_shape=jax.ShapeDtypeStruct((b, n, cw), F32),
        compiler_params=_cparams(("parallel",)),
        name="fourier_dense",
    )(uf, _table(np.cos(ang)), _table(np.sin(ang)), _table(np.concatenate([cc, sc], axis=0)), wf)


def _pool_kernel(x_ref, prev_ref, next_ref, wp_ref, ps_ref, o_ref, pad_ref, *, n_tokens):
    i = pl.program_id(1)
    tm = x_ref.shape[1]
    h = POOL_HALO
    pad_ref[0:h] = jnp.where(i > 0, prev_ref[0], 0.0)
    pad_ref[h:h + tm] = x_ref[0]
    pad_ref[h + tm:2 * h + tm] = jnp.where(i < pl.num_programs(1) - 1, next_ref[0], 0.0)
    ld = lambda off: pad_ref[h + off:h + off + tm, :]
    x0 = ld(0)
    sums = []
    acc = x0
    lo, hi = 0, 0
    for w in POOL_WINDOWS:
        for off in list(range(-(w // 2), lo)) + list(range(hi + 1, w // 2)):
            acc = acc + ld(off)
        lo, hi = -(w // 2), w // 2 - 1
        sums.append(acc)
    t = i * tm + lax.broadcasted_iota(jnp.int32, (tm, 1), 0)
    grp = lax.broadcasted_iota(jnp.int32, (1, x0.shape[1]), 1) // (x0.shape[1] // len(POOL_WINDOWS))
    pooled = jnp.zeros_like(x0)
    for gi, w in enumerate(POOL_WINDOWS):
        cnt = (jnp.minimum(t + w // 2, n_tokens) - jnp.maximum(t - w // 2, 0)).astype(F32)
        pooled = jnp.where(grp == gi, sums[gi] / cnt, pooled)
    pooled = pooled - x0
    o_ref[0] = jnp.dot(pooled.astype(BF16), wp_ref[...], preferred_element_type=F32) * ps_ref[...]


def _pool(up, wp_bd, ps, tm):
    b, n, cw = up.shape
    h = POOL_HALO
    nblk = n // h
    return pl.pallas_call(
        functools.partial(_pool_kernel, n_tokens=n),
        grid=(b, n // tm),
        in_specs=[pl.BlockSpec((1, tm, cw), lambda i, j: (i, j, 0)),
                  pl.BlockSpec((1, h, cw), lambda i, j: (i, jnp.maximum(j * (tm // h) - 1, 0), 0)),
                  pl.BlockSpec((1, h, cw), lambda i, j: (i, jnp.minimum((j + 1) * (tm // h), nblk - 1), 0)),
                  pl.BlockSpec((cw, cw), lambda i, j: (0, 0)),
                  pl.BlockSpec((1, cw), lambda i, j: (0, 0))],
        out_specs=pl.BlockSpec((1, tm, cw), lambda i, j: (i, j, 0)),
        out_shape=jax.ShapeDtypeStruct((b, n, cw), F32),
        scratch_shapes=[pltpu.VMEM((tm + 2 * h, cw), F32)],
        compiler_params=_cparams(("parallel", "parallel")),
        name="pool",
    )(up, up, up, wp_bd, ps)


def _na_bias_tables(rpb):
    qr = np.arange(NA_QROWS)[:, None, None, None]
    qc = np.arange(GRID_W)[None, :, None, None]
    kr = np.arange(NA_KROWS)[None, None, :, None]
    kc = np.arange(GRID_W)[None, None, None, :]
    c0 = np.clip(qc - NA_WIN_COLS // 2, 0, GRID_W - NA_WIN_COLS)
    col_ok = (kc >= c0) & (kc < c0 + NA_WIN_COLS)
    co = np.clip(kc - qc + NA_WIN_COLS - 1, 0, 2 * NA_WIN_COLS - 2)
    tabs = []
    for case in range(3):
        krel = kr - 4 * case
        r0 = qr - NA_WIN_ROWS // 2
        r0 = np.maximum(r0, 0) if case == 0 else (np.minimum(r0, 0) if case == 2 else r0)
        row_ok = (krel >= r0) & (krel < r0 + NA_WIN_ROWS)
        ro = np.clip(krel - qr + NA_WIN_ROWS - 1, 0, 2 * NA_WIN_ROWS - 2)
        ok = np.broadcast_to(row_ok & col_ok, (NA_QROWS, GRID_W, NA_KROWS, GRID_W))
        ro_b = np.broadcast_to(ro, ok.shape).reshape(NA_QROWS * GRID_W, NA_KROWS * GRID_W)
        co_b = np.broadcast_to(co, ok.shape).reshape(NA_QROWS * GRID_W, NA_KROWS * GRID_W)
        okf = ok.reshape(NA_QROWS * GRID_W, NA_KROWS * GRID_W)
        tabs.append(jnp.where(okf[None], rpb[:, ro_b, co_b].astype(F32), NEG_INF))
    return jnp.stack(tabs)


def _na_kernel(q_ref, k_ref, v_ref, kc_ref, vc_ref, bias_ref, o_ref, *, rows):
    rb = pl.program_id(2)
    kr0 = jnp.clip(rb * NA_QROWS - NA_WIN_ROWS // 2, 0, rows - NA_KROWS)
    start = pl.multiple_of(kr0 * GRID_W, GRID_W)
    nkeys = NA_KROWS * GRID_W
    lane = lax.broadcasted_iota(jnp.int32, (1, LANES), 1)
    dn = (((1,), (1,)), ((), ()))
    out = None
    for hh in range(2):
        q = q_ref[0, hh]
        kb = k_ref[0, hh, pl.ds(start, nkeys), :]
        vb = v_ref[0, hh, pl.ds(start, nkeys), :]
        s = lax.dot_general(q, kb, dn, preferred_element_type=F32) + bias_ref[0, hh]
        sc = lax.dot_general(q, kc_ref[0, hh], dn, preferred_element_type=F32)
        m = jnp.maximum(jnp.max(s, -1, keepdims=True), jnp.max(sc, -1, keepdims=True))
        p = jnp.exp(s - m).astype(BF16)
        pc = jnp.exp(sc - m).astype(BF16)
        acc = (jnp.dot(p, vb, preferred_element_type=F32)
               + jnp.dot(pc, vc_ref[0, hh], preferred_element_type=F32))
        one_lane = 64 if hh == 0 else 0
        own = (lane < 64) if hh == 0 else (lane >= 64)
        o = jnp.where(own, acc / acc[:, one_lane:one_lane + 1], 0.0)
        out = o if out is None else out + o
    o_ref[0] = out


def _na_latent(nq, nk, nv, nkc, nvc, bias):
    b, heads, n, _ = nq.shape
    rows = n // GRID_W
    assert rows >= NA_KROWS and rows % NA_QROWS == 0
    tq = NA_QROWS * GRID_W
    nctx = nkc.shape[2]

    def bias_idx(i, p, j):
        kr0 = jnp.clip(j * NA_QROWS - NA_WIN_ROWS // 2, 0, rows - NA_KROWS)
        return ((j * NA_QROWS - kr0) // 4, p, 0, 0)

    res = pl.BlockSpec((1, 2, n, LANES), lambda i, p, j: (i, p, 0, 0))
    ctx = pl.BlockSpec((1, 2, nctx, LANES), lambda i, p, j: (i, p, 0, 0))
    return pl.pallas_call(
        functools.partial(_na_kernel, rows=rows),
        grid=(b, heads // 2, rows // NA_QROWS),
        in_specs=[pl.BlockSpec((1, 2, tq, LANES), lambda i, p, j: (i, p, j, 0)), res, res, ctx, ctx,
                  pl.BlockSpec((1, 2, tq, NA_KROWS * GRID_W), bias_idx)],
        out_specs=pl.BlockSpec((1, tq, LANES), lambda i, p, j: (i, j, p)),
        out_shape=jax.ShapeDtypeStruct((b, n, heads * 64), F32),
        compiler_params=_cparams(("parallel", "parallel", "parallel"), VMEM_LIMIT),
        name="na_latent",
    )(nq, nk, nv, nkc, nvc, bias)


def _outproj_kernel(x_ref, oa_ref, of_ref, op_ref, on_ref, w_ref, ga_ref, g_ref, sh_ref, sc_ref, *rest,
                    route):
    if route:
        wr_ref, x1_ref, h2_ref, rt_ref = rest
    else:
        x1_ref, h2_ref = rest
    gw = oa_ref.shape[-1]
    o = None
    for gi, r in enumerate((oa_ref, of_ref, op_ref, on_ref)):
        t = jnp.dot(r[0].astype(BF16), w_ref[gi * gw:(gi + 1) * gw, :], preferred_element_type=F32)
        o = t if o is None else o + t
    x1 = x_ref[0] + ga_ref[0] * o
    x1_ref[0] = x1
    h = (x1 * _rms(x1, x1.shape[-1])) * g_ref[...]
    h = h * (1.0 + sc_ref[0]) + sh_ref[0]
    h2_ref[0] = h.astype(h2_ref.dtype)
    if route:
        n_exp = 8
        lane = lax.broadcasted_iota(jnp.int32, (1, LANES), 1)
        lg = jnp.dot(h, wr_ref[...], preferred_element_type=F32, precision=lax.Precision.HIGHEST)
        lg = jnp.where(lane < n_exp, lg, NEG_INF)
        v1 = jnp.max(lg, -1, keepdims=True)
        i1 = jnp.min(jnp.where(lg == v1, lane, LANES), -1, keepdims=True)
        lg2 = jnp.where(lane == i1, NEG_INF, lg)
        v2 = jnp.max(lg2, -1, keepdims=True)
        i2 = jnp.min(jnp.where(lg2 == v2, lane, LANES), -1, keepdims=True)
        e2 = jnp.exp(v2 - v1)
        g1 = 1.0 / (1.0 + e2)
        g2 = e2 / (1.0 + e2)
        rt_ref[0] = jnp.where(lane == 0, i1.astype(F32),
                              jnp.where(lane == 1, i2.astype(F32),
                                        jnp.where(lane == 2, g1, jnp.where(lane == 3, g2, 0.0))))


def _outproj(x, oa, of, op, on, w_out, ga, g, sh, sc, tm, w_router=None):
    b, n, d = x.shape
    route = w_router is not None
    gw = oa.shape[-1]
    tok = lambda w: pl.BlockSpec((1, tm, w), lambda i, j: (i, j, 0))
    per_b = pl.BlockSpec((1, 1, d), lambda i, j: (i, 0, 0))
    full = lambda a: pl.BlockSpec(a.shape, lambda i, j: (0,) * a.ndim)
    ins = [x, oa, of, op, on, w_out, ga, g, sh, sc]
    in_specs = [tok(d), tok(gw), tok(gw), tok(gw), tok(gw), full(w_out), per_b, full(g), per_b, per_b]
    out_specs = [tok(d), tok(d)]
    out_shape = [jax.ShapeDtypeStruct((b, n, d), F32),
                 jax.ShapeDtypeStruct((b, n, d), F32 if route else BF16)]
    if route:
        ins.append(w_router)
        in_specs.append(full(w_router))
        out_specs.append(tok(LANES))
        out_shape.append(jax.ShapeDtypeStruct((b, n, LANES), F32))
    return pl.pallas_call(
        functools.partial(_outproj_kernel, route=route),
        grid=(b, n // tm),
        in_specs=in_specs, out_specs=out_specs, out_shape=out_shape,
        compiler_params=_cparams(("parallel", "parallel"), VMEM_LIMIT),
        name="outproj",
    )(*ins)


def _ffn_kernel(h_ref, x_ref, ga_ref, w1_ref, w3_ref, w2_ref, o_ref, acc_ref, *, fc):
    h = h_ref[0]
    ff = w1_ref.shape[1]
    for j in range(ff // fc):
        a = jnp.dot(h, w1_ref[:, j * fc:(j + 1) * fc], preferred_element_type=F32)
        bb = jnp.dot(h, w3_ref[:, j * fc:(j + 1) * fc], preferred_element_type=F32)
        t = jnp.dot((_silu(a) * bb).astype(BF16), w2_ref[j * fc:(j + 1) * fc, :], preferred_element_type=F32)
        if j == 0:
            acc_ref[...] = t
        else:
            acc_ref[...] += t
    o_ref[0] = x_ref[0] + ga_ref[0] * acc_ref[...]


def _ffn_dense(h2, x1, ga, w1, w3, w2, tm):
    b, n, d = x1.shape
    ff = w1.shape[1]
    fc = 256 if ff % 256 == 0 else ff
    tok = pl.BlockSpec((1, tm, d), lambda i, j: (i, j, 0))
    per_b = pl.BlockSpec((1, 1, d), lambda i, j: (i, 0, 0))
    full = lambda a: pl.BlockSpec(a.shape, lambda i, j: (0, 0), pipeline_mode=pl.Buffered(1))
    return pl.pallas_call(
        functools.partial(_ffn_kernel, fc=fc),
        grid=(b, n // tm),
        in_specs=[tok, tok, per_b, full(w1), full(w3), full(w2)],
        out_specs=tok,
        out_shape=jax.ShapeDtypeStruct((b, n, d), F32),
        scratch_shapes=[pltpu.VMEM((tm, d), F32)],
        compiler_params=_cparams(("parallel", "parallel"), VMEM_LIMIT),
        name="ffn_dense",
    )(h2, x1, ga, w1, w3, w2)


def _expert_kernel(te_ref, tv_ref, src_ref, h_hbm, gs_ref, w1_ref, w3_ref, w2_ref, y_ref,
                   xs_ref, xb_ref, acc_ref, sem):
    i = pl.program_id(0)
    j = pl.program_id(1)
    tme = xs_ref.shape[0]
    valid = tv_ref[i] > 0

    @pl.when(jnp.logical_and(valid, j == 0))
    def _():
        base = i * tme

        def issue(r, c):
            pltpu.make_async_copy(h_hbm.at[src_ref[base + r]], xs_ref.at[r], sem).start()
            return c

        lax.fori_loop(0, tme, issue, 0)
        pltpu.make_async_copy(h_hbm.at[pl.ds(0, tme)], xs_ref, sem).wait()
        xb_ref[...] = xs_ref[...].astype(BF16)

    @pl.when(valid)
    def _():
        xb = xb_ref[...]
        a = jnp.dot(xb, w1_ref[0], preferred_element_type=F32)
        bb = jnp.dot(xb, w3_ref[0], preferred_element_type=F32)
        t = jnp.dot((_silu(a) * bb).astype(BF16), w2_ref[0], preferred_element_type=F32)

        @pl.when(j == 0)
        def _():
            acc_ref[...] = t

        @pl.when(j > 0)
        def _():
            acc_ref[...] += t

    @pl.when(j == pl.num_programs(1) - 1)
    def _():
        y_ref[...] = jnp.where(valid, acc_ref[...] * gs_ref[...], 0.0)


def _combine_kernel(d0_ref, d1_ref, ys_hbm, x_ref, ga_ref, o_ref, buf_ref, sem):
    i = pl.program_id(0)
    tc = x_ref.shape[0]
    base = i * tc

    def issue(r, c):
        pltpu.make_async_copy(ys_hbm.at[d0_ref[base + r]], buf_ref.at[0, r], sem).start()
        pltpu.make_async_copy(ys_hbm.at[d1_ref[base + r]], buf_ref.at[1, r], sem).start()
        return c

    lax.fori_loop(0, tc, issue, 0)
    pltpu.make_async_copy(ys_hbm.at[pl.ds(0, tc)], buf_ref.at[0], sem).wait()
    pltpu.make_async_copy(ys_hbm.at[pl.ds(0, tc)], buf_ref.at[1], sem).wait()
    o_ref[...] = x_ref[...] + ga_ref[...] * (buf_ref[0] + buf_ref[1])


def _moe(h2, x1, ga_rows, rt, w1, w3, w2, tme, tc):
    n, d = h2.shape
    n_exp, _, ff = w1.shape
    fc = 512 if ff % 512 == 0 else ff
    e_idx = rt[:, 0:TOP_K].astype(jnp.int32).reshape(-1)
    gate = rt[:, TOP_K:2 * TOP_K].reshape(-1)
    onehot = (e_idx[:, None] == jnp.arange(n_exp)[None, :]).astype(jnp.int32)
    pos = jnp.take_along_axis(jnp.cumsum(onehot, axis=0) - onehot, e_idx[:, None], axis=1)[:, 0]
    counts = jnp.sum(onehot, axis=0)
    padded = ((counts + tme - 1) // tme) * tme
    ends = jnp.cumsum(padded)
    offs = ends - padded
    dest = offs[e_idx] + pos
    n_rows = n * TOP_K + n_exp * tme
    n_tiles = n_rows // tme
    src = jnp.zeros((n_rows,), jnp.int32).at[dest].set(jnp.arange(n * TOP_K, dtype=jnp.int32) // TOP_K)
    gsort = jnp.zeros((n_rows, 1), F32).at[dest, 0].set(gate)
    tile_start = jnp.arange(n_tiles, dtype=jnp.int32) * tme
    tvalid = (tile_start < ends[-1]).astype(jnp.int32)
    texp = jnp.minimum(jnp.sum((tile_start[:, None] >= ends[None, :]).astype(jnp.int32), axis=1), n_exp - 1)
    texp = jnp.where(tvalid > 0, texp, texp[jnp.maximum(ends[-1] // tme - 1, 0)])
    nf = ff // fc

    def wcol(i, j, te, tv, s):
        return (te[i], 0, jnp.where(tv[i] > 0, j, nf - 1))

    def wrow(i, j, te, tv, s):
        return (te[i], jnp.where(tv[i] > 0, j, nf - 1), 0)

    ys = pl.pallas_call(
        _expert_kernel,
        grid_spec=pltpu.PrefetchScalarGridSpec(
            num_scalar_prefetch=3,
            grid=(n_tiles, nf),
            in_specs=[pl.BlockSpec(memory_space=pl.ANY),
                      pl.BlockSpec((tme, 1), lambda i, j, te, tv, s: (i, 0)),
                      pl.BlockSpec((1, d, fc), wcol),
                      pl.BlockSpec((1, d, fc), wcol),
                      pl.BlockSpec((1, fc, d), wrow)],
            out_specs=pl.BlockSpec((tme, d), lambda i, j, te, tv, s: (i, 0)),
            scratch_shapes=[pltpu.VMEM((tme, d), F32), pltpu.VMEM((tme, d), BF16),
                            pltpu.VMEM((tme, d), F32), pltpu.SemaphoreType.DMA(())]),
        out_shape=jax.ShapeDtypeStruct((n_rows, d), F32),
        compiler_params=_cparams(("arbitrary", "arbitrary"), VMEM_LIMIT),
        name="experts",
    )(texp, tvalid, src, h2, gsort, w1, w3, w2)

    d2 = dest.reshape(n, TOP_K)
    return pl.pallas_call(
        _combine_kernel,
        grid_spec=pltpu.PrefetchScalarGridSpec(
            num_scalar_prefetch=2,
            grid=(n // tc,),
            in_specs=[pl.BlockSpec(memory_space=pl.ANY),
                      pl.BlockSpec((tc, d), lambda i, a, b_: (i, 0)),
                      pl.BlockSpec((tc, d), lambda i, a, b_: (i, 0))],
            out_specs=pl.BlockSpec((tc, d), lambda i, a, b_: (i, 0)),
            scratch_shapes=[pltpu.VMEM((2, tc, d), F32), pltpu.SemaphoreType.DMA(())]),
        out_shape=jax.ShapeDtypeStruct((n, d), F32),
        compiler_params=_cparams(("arbitrary",), VMEM_LIMIT),
        name="moe_combine",
    )(d2[:, 0], d2[:, 1], ys, x1, ga_rows)


def _pick_tile(n, pref):
    t = pref
    while n % t:
        t //= 2
    return t


def kernel(x, c, ctx, c_ctx, w_ada, b_ada, g_mix, g_ffn, w_in, w_out, g_cq, g_ckv, w_uq, w_ukv,
           g_mla_qn, g_mla_qr, g_mla_kn, g_mla_kr, w_fourier, w_pool, pool_scale, g_na_q, g_na_k,
           na_rpb, w1_dense, w3_dense, w2_dense, w_router, w1_moe, w3_moe, w2_moe):
    b, n, d = x.shape
    nctx = ctx.shape[1]
    depth = w_ada.shape[0]
    assert b <= 7 and n % GRID_W == 0

    cc = jnp.zeros((8, d), F32).at[:b].set(c).at[b].set(c_ctx)
    mods = _adaln(cc, w_ada, b_ada)
    cos_x, sin_x = _rope_tables(n)
    cos_c = jnp.ones((nctx, LANES), F32)
    sin_c = jnp.zeros((nctx, LANES), F32)
    sw = _rope_swap_perm()
    tm_x = _pick_tile(n, 256)
    tm_c = _pick_tile(nctx, 256)

    for l in range(depth):
        last = l == depth - 1
        mx = mods[l, :b].reshape(b, 1, 6 * d)
        mc = jnp.broadcast_to(mods[l, b].reshape(1, 1, 6 * d), (b, 1, 6 * d))
        part = lambda m, k: m[:, :, k * d:(k + 1) * d]

        wts = [
            _prep_in_weights(w_in[l]),
            g_cq[l].reshape(1, -1), _prep_uq(w_uq[l]),
            _lane_vec([(g_mla_qn[l], 64), (g_mla_qr[l], 32), (None, 32)]),
            _lane_vec([(None, 64), (g_mla_qr[l][sw], 32), (None, 32)]),
            g_ckv[l].reshape(1, -1), _prep_ukv(w_ukv[l]),
            _lane_vec([(g_mla_kn[l], 64), (None, 64)]),
            _lane_vec([(None, 64), (g_mla_kr[l], 32), (None, 32)]),
            _lane_vec([(None, 64), (g_mla_kr[l][sw], 32), (None, 32)]),
            jnp.tile(g_na_q[l], NA_HEADS).reshape(1, -1), jnp.tile(g_na_k[l], NA_HEADS).reshape(1, -1),
        ]
        gmix = g_mix[l].reshape(1, d)
        qx, kx, vx, fx, px, nqx, nkx, nvx = _inproj(x, part(mx, 0), part(mx, 1), gmix, cos_x, sin_x, wts, tm_x)
        qc, kc, vc, fc_, pc, nqc, nkc, nvc = _inproj(ctx, part(mc, 0), part(mc, 1), gmix, cos_c, sin_c, wts, tm_c)

        wf = w_fourier[l].astype(BF16)
        cg = w_pool.shape[-1]
        wp_bd = jnp.zeros((len(POOL_WINDOWS) * cg,) * 2, F32)
        for gi in range(len(POOL_WINDOWS)):
            wp_bd = wp_bd.at[gi * cg:(gi + 1) * cg, gi * cg:(gi + 1) * cg].set(w_pool[l, gi])
        wp_bd = wp_bd.astype(BF16)
        ps = pool_scale[l].reshape(1, -1)
        w_out_l = w_out[l].astype(BF16)
        gffn = g_ffn[l].reshape(1, d)

        o_mla = _flash_pairs(qx, jnp.concatenate([kx, kc], axis=2), jnp.concatenate([vx, vc], axis=2),
                             _pick_tile(n, 256))
        o_f = _fourier_latent(fx, wf)
        o_p = _pool(px, wp_bd, ps, _pick_tile(n, 512))
        o_na = _na_latent(nqx, nkx, nvx, nkc, nvc, _na_bias_tables(na_rpb[l]))

        moe_layer = l % 2 == 1
        i = l // 2
        res = _outproj(x, o_mla, o_f, o_p, o_na, w_out_l, part(mx, 2), gffn, part(mx, 3), part(mx, 4), tm_x,
                       w_router=jnp.pad(w_router[i], ((0, 0), (0, LANES - w_router.shape[-1]))) if moe_layer else None)
        if moe_layer:
            x1, h2, rt = res
            ga_rows = jnp.broadcast_to(part(mx, 5), (b, n, d)).reshape(b * n, d)
            x = _moe(h2.reshape(b * n, d), x1.reshape(b * n, d), ga_rows, rt.reshape(b * n, LANES),
                     w1_moe[i].astype(BF16), w3_moe[i].astype(BF16), w2_moe[i].astype(BF16),
                     tme=1024, tc=256).reshape(b, n, d)
        else:
            x1, h2 = res
            w1, w3, w2 = w1_dense[i].astype(BF16), w3_dense[i].astype(BF16), w2_dense[i].astype(BF16)
            x = _ffn_dense(h2, x1, part(mx, 5), w1, w3, w2, _pick_tile(n, 512))

        if not last:
            oc_mla = _flash_pairs(qc, kc, vc, tm_c)
            oc_f = _fourier_dense(fc_, wf)
            oc_p = _pool(pc, wp_bd, ps, tm_c)
            oc_na = _flash_pairs(nqc, nkc, nvc, tm_c)
            resc = _outproj(ctx, oc_mla, oc_f, oc_p, oc_na, w_out_l, part(mc, 2), gffn, part(mc, 3),
                            part(mc, 4), tm_c,
                            w_router=jnp.pad(w_router[i], ((0, 0), (0, LANES - w_router.shape[-1]))) if moe_layer else None)
            if moe_layer:
                c1, hc2, rtc = resc
                gac = jnp.broadcast_to(part(mc, 5), (b, nctx, d)).reshape(b * nctx, d)
                ctx = _moe(hc2.reshape(b * nctx, d), c1.reshape(b * nctx, d), gac, rtc.reshape(b * nctx, LANES),
                           w1_moe[i].astype(BF16), w3_moe[i].astype(BF16), w2_moe[i].astype(BF16),
                           tme=256, tc=256).reshape(b, nctx, d)
            else:
                c1, hc2 = resc
                ctx = _ffn_dense(hc2, c1, part(mc, 5), w1, w3, w2, tm_c)
    return x
```

```python
import functools
import math

import numpy as np
import jax
import jax.numpy as jnp
from jax import lax
from jax.experimental import pallas as pl
from jax.experimental.pallas import tpu as pltpu

F32 = jnp.float32
BF16 = jnp.bfloat16

GRID_W = 64
LANES = 128
EPS = 1e-6
NEG_INF = -1e30

MLA_HEADS = 4
MLA_NOPE = 64
MLA_ROPE = 32
MLA_V = 64
MLA_SCALE = (MLA_NOPE + MLA_ROPE) ** -0.5
MLA_SPARE_LANE = MLA_NOPE + MLA_ROPE
ROPE_BASE = 10000.0
LOG2E = 1.0 / math.log(2.0)
MAX_SOFTMAX_GAP_LOG2 = 120.0

FOURIER_GROUPS = 4
POOL_WINDOWS = (2, 4, 8, 16)
POOL_HALO = 8

NA_HEADS = 4
NA_HEAD_DIM = 64
NA_SCALE = NA_HEAD_DIM ** -0.5
NA_WIN_ROWS = 8
NA_WIN_COLS = 16
NA_QROWS = 8
NA_KROWS = 16

TOP_K = 2
VMEM_LIMIT = 56 * 1024 * 1024


def _cparams(sem, vmem=None):
    return pltpu.CompilerParams(dimension_semantics=sem, vmem_limit_bytes=vmem)


def _silu(a):
    return a / (1.0 + jnp.exp(-a))


def _rms(v, n):
    return lax.rsqrt(jnp.sum(v * v, axis=-1, keepdims=True) * (1.0 / n) + EPS)


def _adaln_kernel(c_ref, w_ref, b_ref, o_ref):
    o_ref[0] = jnp.dot(_silu(c_ref[...]), w_ref[0], preferred_element_type=F32) + b_ref[0]


def _adaln(cc, w_ada, b_ada):
    depth, d, d6 = w_ada.shape
    tn = 512
    return pl.pallas_call(
        _adaln_kernel,
        grid=(depth, d6 // tn),
        in_specs=[pl.BlockSpec((8, d), lambda l, j: (0, 0)),
                  pl.BlockSpec((1, d, tn), lambda l, j: (l, 0, j)),
                  pl.BlockSpec((1, 1, tn), lambda l, j: (l, 0, j))],
        out_specs=pl.BlockSpec((1, 8, tn), lambda l, j: (l, 0, j)),
        out_shape=jax.ShapeDtypeStruct((depth, 8, d6), F32),
        compiler_params=_cparams(("arbitrary", "arbitrary")),
        name="adaln",
    )(cc, w_ada, b_ada.reshape(depth, 1, d6))


IN_COLS = 1920


def _rope_swap_perm():
    j = np.arange(MLA_ROPE)
    return np.where((j % 16) < 8, j + 8, j - 8)


def _prep_in_weights(w_in_l):
    d = w_in_l.shape[0]
    s = [0, 256, 384, 416, 672, 928, 1184, 1440, 1696]
    cq, ckv, kr, f, p, nq, nk, nv = [w_in_l[:, s[i]:s[i + 1]] for i in range(8)]
    z = lambda n: jnp.zeros((d, n), w_in_l.dtype)
    krsw = kr[:, _rope_swap_perm()]
    return jnp.concatenate([cq, ckv, f, p, nq, nk, nv, z(64), kr, z(32), z(64), krsw, z(32)],
                           axis=1).astype(BF16)


def _prep_uq(w_uq_l):
    r = w_uq_l.shape[0]
    w = w_uq_l.reshape(r, MLA_HEADS, MLA_NOPE + MLA_ROPE)
    z = lambda n: jnp.zeros((r, MLA_HEADS, n), w.dtype)
    main = jnp.concatenate([w, z(32)], axis=-1)
    sw = jnp.concatenate([z(64), w[..., MLA_NOPE:][..., _rope_swap_perm()], z(32)], axis=-1)
    return jnp.concatenate([main.reshape(r, -1), sw.reshape(r, -1)], axis=1).astype(BF16)


def _prep_ukv(w_ukv_l):
    r = w_ukv_l.shape[0]
    w = w_ukv_l.reshape(r, MLA_HEADS, MLA_NOPE + MLA_V)
    z = jnp.zeros((r, 64), w.dtype)
    tiles = [jnp.concatenate([w[:, h, :MLA_NOPE], z], axis=1) for h in range(MLA_HEADS)]
    for h in range(MLA_HEADS):
        v = w[:, h, MLA_NOPE:]
        tiles.append(jnp.concatenate([v, z] if h % 2 == 0 else [z, v], axis=1))
    return jnp.concatenate(tiles, axis=1).astype(BF16)


def _lane_vec(parts):
    cols = [jnp.zeros((n,), F32) if a is None else a.astype(F32) for a, n in parts]
    return jnp.concatenate(cols).reshape(1, -1)


def _rope_tables(n_tokens):
    t = np.arange(n_tokens)
    pos = np.stack([t // GRID_W, t % GRID_W], axis=1).astype(np.float64)
    half = MLA_ROPE // 2
    inv = 1.0 / (ROPE_BASE ** (np.arange(0, half, 2, dtype=np.float64) / half))
    j = np.arange(MLA_ROPE)
    ang = pos[:, j // 16] * inv[j % 8]
    sign = np.where((j % 16) < 8, -1.0, 1.0)
    cos = np.ones((n_tokens, LANES), np.float32)
    sin = np.zeros((n_tokens, LANES), np.float32)
    cos[:, 64:96] = np.cos(ang)
    sin[:, 64:96] = np.sin(ang) * sign
    return jnp.asarray(cos), jnp.asarray(sin)


def _inproj_kernel(x_ref, sh_ref, sc_ref, g_ref, cos_ref, sin_ref, bound_ref, win_ref,
                   gcq_ref, wuq_ref, gq_ref, gqsw_ref,
                   gckv_ref, wukv_ref, gkn_ref, gkr_ref, gkrsw_ref, gnq_ref, gnk_ref,
                   q_ref, k_ref, v_ref, f_ref, p_ref, nq_ref, nk_ref, nv_ref):
    x = x_ref[0]
    d = x.shape[-1]
    h = (x * _rms(x, d)) * g_ref[...]
    h = h * (1.0 + sc_ref[0]) + sh_ref[0]
    u = jnp.dot(h.astype(BF16), win_ref[...], preferred_element_type=F32)
    cq, ckv = u[:, 0:256], u[:, 256:384]
    f_ref[0] = u[:, 384:640].astype(BF16)
    p_ref[0] = u[:, 640:896]
    nq, nk, nv = u[:, 896:1152], u[:, 1152:1408], u[:, 1408:1664]
    krb, krs = u[:, 1664:1792], u[:, 1792:1920]
    cos, sin = cos_ref[...], sin_ref[...]
    lane = lax.broadcasted_iota(jnp.int32, (1, LANES), 1)
    m_nope = lane < MLA_NOPE
    m_rope = jnp.logical_and(lane >= MLA_NOPE, lane < MLA_NOPE + MLA_ROPE)
    bound_mla, bound_na = bound_ref[:, 0:1], bound_ref[:, 1:2]

    cqn = (cq * _rms(cq, 256) * gcq_ref[...]).astype(BF16)
    qall = jnp.dot(cqn, wuq_ref[...], preferred_element_type=F32)
    for hd in range(MLA_HEADS):
        blk = qall[:, hd * LANES:(hd + 1) * LANES]
        sw = qall[:, (MLA_HEADS + hd) * LANES:(MLA_HEADS + hd + 1) * LANES]
        sq = blk * blk
        rn = lax.rsqrt(jnp.sum(jnp.where(m_nope, sq, 0.0), -1, keepdims=True) * (1.0 / MLA_NOPE) + EPS)
        rr = lax.rsqrt(jnp.sum(jnp.where(m_rope, sq, 0.0), -1, keepdims=True) * (1.0 / MLA_ROPE) + EPS)
        qh = blk * jnp.where(m_nope, rn, rr) * gq_ref[...]
        qs = sw * rr * gqsw_ref[...]
        q_ref[0, hd] = jnp.where(lane == MLA_SPARE_LANE, -bound_mla,
                                 (qh * cos + qs * sin) * (MLA_SCALE * LOG2E)).astype(BF16)

    ckvn = (ckv * _rms(ckv, 128) * gckv_ref[...]).astype(BF16)
    kvall = jnp.dot(ckvn, wukv_ref[...], preferred_element_type=F32)
    rkr = _rms(krb, MLA_ROPE)
    krot = (krb * rkr * gkr_ref[...]) * cos + (krs * rkr * gkrsw_ref[...]) * sin
    for hd in range(MLA_HEADS):
        blk = kvall[:, hd * LANES:(hd + 1) * LANES]
        k_ref[0, hd] = jnp.where(lane == MLA_SPARE_LANE, 1.0,
                                 blk * _rms(blk, MLA_NOPE) * gkn_ref[...] + krot).astype(BF16)
        vb = kvall[:, (MLA_HEADS + hd) * LANES:(MLA_HEADS + hd + 1) * LANES]
        one_lane = 64 if hd % 2 == 0 else 0
        v_ref[0, hd] = jnp.where(lane == one_lane, 1.0, vb).astype(BF16)

    lane2 = lax.broadcasted_iota(jnp.int32, (1, 2 * LANES), 1)

    def seg_rms(t):
        sq = t * t
        r = jnp.zeros_like(t)
        for s in range(NA_HEADS):
            m = (lane2 // NA_HEAD_DIM) == s
            ss = jnp.sum(jnp.where(m, sq, 0.0), -1, keepdims=True)
            r = jnp.where(m, lax.rsqrt(ss * (1.0 / NA_HEAD_DIM) + EPS), r)
        return r

    nqn = nq * seg_rms(nq) * gnq_ref[...] * (NA_SCALE * LOG2E)
    nkn = nk * seg_rms(nk) * gnk_ref[...]
    for hd in range(NA_HEADS):
        t0 = (hd // 2) * LANES
        own = (lane < 64) if hd % 2 == 0 else (lane >= 64)
        one_lane = 64 if hd % 2 == 0 else 0
        spare = lane == (LANES - 1 if hd % 2 == 0 else 0)
        nq_ref[0, hd] = jnp.where(own, nqn[:, t0:t0 + LANES], jnp.where(spare, -bound_na, 0.0)).astype(BF16)
        nk_ref[0, hd] = jnp.where(own, nkn[:, t0:t0 + LANES], jnp.where(spare, 1.0, 0.0)).astype(BF16)
        nv_ref[0, hd] = jnp.where(own, nv[:, t0:t0 + LANES],
                                  jnp.where(lane == one_lane, 1.0, 0.0)).astype(BF16)


def _inproj(x, sh, sc, g, cos, sin, bounds, wts, tm):
    b, n, d = x.shape
    heads = MLA_HEADS
    row = lambda a: pl.BlockSpec(a.shape, lambda i, j: (0,) * a.ndim)
    per_b = pl.BlockSpec((1, 1, d), lambda i, j: (i, 0, 0))
    tok = lambda w: pl.BlockSpec((1, tm, w), lambda i, j: (i, j, 0))
    hd_spec = pl.BlockSpec((1, heads, tm, LANES), lambda i, j: (i, 0, j, 0))
    tab = pl.BlockSpec((tm, LANES), lambda i, j: (j, 0))
    hshape = jax.ShapeDtypeStruct((b, heads, n, LANES), BF16)
    return pl.pallas_call(
        _inproj_kernel,
        grid=(b, n // tm),
        in_specs=[tok(d), per_b, per_b, row(g), tab, tab, row(bounds)] + [row(w) for w in wts],
        out_specs=[hd_spec, hd_spec, hd_spec, tok(256), tok(256), hd_spec, hd_spec, hd_spec],
        out_shape=[hshape, hshape, hshape,
                   jax.ShapeDtypeStruct((b, n, 256), BF16), jax.ShapeDtypeStruct((b, n, 256), F32),
                   hshape, hshape, hshape],
        compiler_params=_cparams(("parallel", "parallel"), VMEM_LIMIT),
        name="inproj",
    )(x, sh, sc, g, cos, sin, bounds, *wts)


_DN_T = (((1,), (1,)), ((), ()))


def _pair_output(accs):
    lane = lax.broadcasted_iota(jnp.int32, (1, LANES), 1)
    out = None
    for hh, acc in enumerate(accs):
        one_lane = 64 if hh == 0 else 0
        own = (lane < 64) if hh == 0 else (lane >= 64)
        o = jnp.where(own, acc / acc[:, one_lane:one_lane + 1], 0.0)
        out = o if out is None else out + o
    return out


def _flash_pair_kernel(q_ref, k_ref, v_ref, o_ref, *, tk, nk, bounded):
    tq = q_ref.shape[2]
    qs = [q_ref[0, hh] for hh in range(2)]

    def chunk(j, hh):
        start = pl.multiple_of(j * tk, tk)
        ks = k_ref[0, hh, pl.ds(start, tk), :]
        vs = v_ref[0, hh, pl.ds(start, tk), :]
        return lax.dot_general(qs[hh], ks, _DN_T, preferred_element_type=F32), vs

    zero = jnp.zeros((tq, LANES), F32)
    if bounded:
        def body(j, accs):
            new = []
            for hh in range(2):
                s, vs = chunk(j, hh)
                new.append(accs[hh] + jnp.dot(jnp.exp2(s).astype(BF16), vs, preferred_element_type=F32))
            return tuple(new)

        accs = lax.fori_loop(0, nk, body, (zero, zero))
    else:
        def body(j, carry):
            new = []
            for hh in range(2):
                m, acc = carry[hh]
                s, vs = chunk(j, hh)
                m_new = jnp.maximum(m, jnp.max(s, axis=-1, keepdims=True))
                p = jnp.exp2(s - m_new).astype(BF16)
                new.append((m_new, acc * jnp.exp2(m - m_new) + jnp.dot(p, vs, preferred_element_type=F32)))
            return tuple(new)

        m0 = jnp.full((tq, 1), NEG_INF, F32)
        carry = lax.fori_loop(0, nk, body, ((m0, zero), (m0, zero)))
        accs = (carry[0][1], carry[1][1])
    o_ref[0] = _pair_output(accs)


def _flash_pairs(q, k, v, tq, bounded):
    b, heads, lq, _ = q.shape
    lk = k.shape[2]
    tk = 768 if lk % 768 == 0 else 256
    kv_spec = pl.BlockSpec((1, 2, lk, LANES), lambda i, p, j: (i, p, 0, 0))

    def call(flag):
        return pl.pallas_call(
            functools.partial(_flash_pair_kernel, tk=tk, nk=lk // tk, bounded=flag),
            grid=(b, heads // 2, lq // tq),
            in_specs=[pl.BlockSpec((1, 2, tq, LANES), lambda i, p, j: (i, p, j, 0)), kv_spec, kv_spec],
            out_specs=pl.BlockSpec((1, tq, LANES), lambda i, p, j: (i, j, p)),
            out_shape=jax.ShapeDtypeStruct((b, lq, heads * 64), F32),
            compiler_params=_cparams(("parallel", "parallel", "parallel"), VMEM_LIMIT),
            name="flash_pairs" if flag else "flash_pairs_online",
        )

    return lax.cond(bounded, call(True), call(False), q, k, v)


def _table(a):
    return jnp.asarray(a, F32).astype(BF16)


def _dft_consts(n_tokens):
    rows = n_tokens // GRID_W
    r = np.arange(rows)
    ang_r = 2 * np.pi * np.outer(r, r) / rows
    a_mat = np.concatenate([np.cos(ang_r), -np.sin(ang_r)], axis=0)
    c = np.arange(GRID_W)
    ang_t = 2 * np.pi * np.outer(r, c) / n_tokens
    ang_c = 2 * np.pi * np.outer(c, c) / GRID_W
    cc, sc = np.cos(ang_c), np.sin(ang_c)
    m_b = np.block([[cc, sc], [-sc, cc]])
    return a_mat, np.cos(ang_t), np.sin(ang_t), m_b


def _channel_dft(n_tokens, width):
    cg = width // FOURIER_GROUPS
    j = np.arange(cg)
    ang = 2 * np.pi * np.outer(j, j) / cg
    eye = np.eye(FOURIER_GROUPS)
    scale = 1.0 / math.sqrt(n_tokens * cg)
    return np.kron(eye, np.cos(ang)) * scale, np.kron(eye, np.sin(ang)) * scale


def _fourier_rows_kernel(u_ref, a_ref, tc_ref, ts_ref, o_ref):
    z = jnp.dot(a_ref[...], u_ref[0], preferred_element_type=F32)
    r = z.shape[0] // 2
    zr, zi = z[:r], z[r:]
    c, s = tc_ref[...], ts_ref[...]
    o_ref[0, :r] = (zr * c + zi * s).astype(BF16)
    o_ref[0, r:] = (zi * c - zr * s).astype(BF16)


def _fourier_cols_kernel(z_ref, mb_ref, cs_ref, wf_ref, o_ref):
    kb = z_ref.shape[1]
    w = GRID_W
    cw = wf_ref.shape[0]
    for t in range(kb):
        y = jnp.dot(mb_ref[...], z_ref[0, t], preferred_element_type=F32)
        f = (jnp.dot(y[:w].astype(BF16), cs_ref[:cw], preferred_element_type=F32)
             + jnp.dot(y[w:].astype(BF16), cs_ref[cw:], preferred_element_type=F32))
        o_ref[0, t] = jnp.dot(f.astype(BF16), wf_ref[...], preferred_element_type=F32)


def _fourier_latent(uf, wf):
    b, n, cw = uf.shape
    rows = n // GRID_W
    a_mat, tw_c, tw_s, m_b = _dft_consts(n)
    cc, sc = _channel_dft(n, cw)
    wide = GRID_W * cw
    tn = 2048
    expand = lambda t: jnp.broadcast_to(jnp.asarray(t, F32)[:, :, None], (rows, GRID_W, cw)).reshape(rows, wide)
    z = pl.pallas_call(
        _fourier_rows_kernel,
        grid=(wide // tn, b),
        in_specs=[pl.BlockSpec((1, rows, tn), lambda j, i: (i, 0, j)),
                  pl.BlockSpec((2 * rows, rows), lambda j, i: (0, 0)),
                  pl.BlockSpec((rows, tn), lambda j, i: (0, j)),
                  pl.BlockSpec((rows, tn), lambda j, i: (0, j))],
        out_specs=pl.BlockSpec((1, 2 * rows, tn), lambda j, i: (i, 0, j)),
        out_shape=jax.ShapeDtypeStruct((b, 2 * rows, wide), BF16),
        compiler_params=_cparams(("parallel", "parallel")),
        name="fourier_rows",
    )(uf.reshape(b, rows, wide), _table(a_mat), expand(tw_c), expand(tw_s))
    z = z.reshape(b, 2, rows, GRID_W, cw).transpose(0, 2, 1, 3, 4).reshape(b, rows, 2 * GRID_W, cw)
    kb = 8
    y = pl.pallas_call(
        _fourier_cols_kernel,
        grid=(b, rows // kb),
        in_specs=[pl.BlockSpec((1, kb, 2 * GRID_W, cw), lambda i, j: (i, j, 0, 0)),
                  pl.BlockSpec((2 * GRID_W, 2 * GRID_W), lambda i, j: (0, 0)),
                  pl.BlockSpec((2 * cw, cw), lambda i, j: (0, 0)),
                  pl.BlockSpec((cw, cw), lambda i, j: (0, 0))],
        out_specs=pl.BlockSpec((1, kb, GRID_W, cw), lambda i, j: (i, j, 0, 0)),
        out_shape=jax.ShapeDtypeStruct((b, rows, GRID_W, cw), F32),
        compiler_params=_cparams(("parallel", "parallel")),
        name="fourier_cols",
    )(z, _table(m_b), _table(np.concatenate([cc, sc], axis=0)), wf)
    return y.transpose(0, 2, 1, 3).reshape(b, n, cw)


def _fourier_dense_kernel(u_ref, cl_ref, sl_ref, cs_ref, wf_ref, o_ref):
    u = u_ref[0]
    cw = wf_ref.shape[0]
    a = jnp.dot(u, cs_ref[:cw], preferred_element_type=F32).astype(BF16)
    bb = jnp.dot(u, cs_ref[cw:], preferred_element_type=F32).astype(BF16)
    f = (jnp.dot(cl_ref[...], a, preferred_element_type=F32)
         - jnp.dot(sl_ref[...], bb, preferred_element_type=F32))
    o_ref[0] = jnp.dot(f.astype(BF16), wf_ref[...], preferred_element_type=F32)


def _fourier_dense(uf, wf):
    b, n, cw = uf.shape
    t = np.arange(n)
    ang = 2 * np.pi * np.outer(t, t) / n
    cc, sc = _channel_dft(n, cw)
    full = lambda shape: pl.BlockSpec(shape, lambda i: (0,) * len(shape))
    return pl.pallas_call(
        _fourier_dense_kernel,
        grid=(b,),
        in_specs=[pl.BlockSpec((1, n, cw), lambda i: (i, 0, 0)), full((n, n)), full((n, n)),
                  full((2 * cw, cw)), full((cw, cw))],
        out_specs=pl.BlockSpec((1, n, cw), lambda i: (i, 0, 0)),
        out_shape=jax.ShapeDtypeStruct((b, n, cw), F32),
        compiler_params=_cparams(("parallel",)),
        name="fourier_dense",
    )(uf, _table(np.cos(ang)), _table(np.sin(ang)), _table(np.concatenate([cc, sc], axis=0)), wf)


def _pool_kernel(x_ref, prev_ref, next_ref, wp_ref, ps_ref, o_ref, pad_ref, *, n_tokens):
    i = pl.program_id(1)
    tm = x_ref.shape[1]
    h = POOL_HALO
    pad_ref[0:h] = jnp.where(i > 0, prev_ref[0], 0.0)
    pad_ref[h:h + tm] = x_ref[0]
    pad_ref[h + tm:2 * h + tm] = jnp.where(i < pl.num_programs(1) - 1, next_ref[0], 0.0)
    ld = lambda off: pad_ref[h + off:h + off + tm, :]
    x0 = ld(0)
    sums = []
    acc = x0
    lo, hi = 0, 0
    for w in POOL_WINDOWS:
        for off in list(range(-(w // 2), lo)) + list(range(hi + 1, w // 2)):
            acc = acc + ld(off)
        lo, hi = -(w // 2), w // 2 - 1
        sums.append(acc)
    t = i * tm + lax.broadcasted_iota(jnp.int32, (tm, 1), 0)
    grp = lax.broadcasted_iota(jnp.int32, (1, x0.shape[1]), 1) // (x0.shape[1] // len(POOL_WINDOWS))
    pooled = jnp.zeros_like(x0)
    for gi, w in enumerate(POOL_WINDOWS):
        cnt = (jnp.minimum(t + w // 2, n_tokens) - jnp.maximum(t - w // 2, 0)).astype(F32)
        pooled = jnp.where(grp == gi, sums[gi] / cnt, pooled)
    pooled = pooled - x0
    o_ref[0] = jnp.dot(pooled.astype(BF16), wp_ref[...], preferred_element_type=F32) * ps_ref[...]


def _pool(up, wp_bd, ps, tm):
    b, n, cw = up.shape
    h = POOL_HALO
    nblk = n // h
    return pl.pallas_call(
        functools.partial(_pool_kernel, n_tokens=n),
        grid=(b, n // tm),
        in_specs=[pl.BlockSpec((1, tm, cw), lambda i, j: (i, j, 0)),
                  pl.BlockSpec((1, h, cw), lambda i, j: (i, jnp.maximum(j * (tm // h) - 1, 0), 0)),
                  pl.BlockSpec((1, h, cw), lambda i, j: (i, jnp.minimum((j + 1) * (tm // h), nblk - 1), 0)),
                  pl.BlockSpec((cw, cw), lambda i, j: (0, 0)),
                  pl.BlockSpec((1, cw), lambda i, j: (0, 0))],
        out_specs=pl.BlockSpec((1, tm, cw), lambda i, j: (i, j, 0)),
        out_shape=jax.ShapeDtypeStruct((b, n, cw), F32),
        scratch_shapes=[pltpu.VMEM((tm + 2 * h, cw), F32)],
        compiler_params=_cparams(("parallel", "parallel")),
        name="pool",
    )(up, up, up, wp_bd, ps)


def _na_bias_tables(rpb):
    heads = rpb.shape[0]
    qr = np.arange(NA_QROWS)[:, None]
    kr = np.arange(NA_KROWS)[None, :]
    qc = np.arange(GRID_W)[:, None]
    kc = np.arange(GRID_W)[None, :]
    c0 = np.clip(qc - NA_WIN_COLS // 2, 0, GRID_W - NA_WIN_COLS)
    col_ok = (kc >= c0) & (kc < c0 + NA_WIN_COLS)
    co = np.clip(kc - qc + NA_WIN_COLS - 1, 0, 2 * NA_WIN_COLS - 2)
    pick_c = (co.reshape(1, -1) == np.arange(2 * NA_WIN_COLS - 1)[:, None]).astype(np.float32)
    by_col = jnp.einsum("hrc,cx->hrx", rpb.astype(F32) * LOG2E, jnp.asarray(pick_c),
                        precision=lax.Precision.HIGHEST)
    tabs = []
    for case in range(3):
        krel = kr - 4 * case
        r0 = qr - NA_WIN_ROWS // 2
        r0 = np.maximum(r0, 0) if case == 0 else (np.minimum(r0, 0) if case == 2 else r0)
        row_ok = (krel >= r0) & (krel < r0 + NA_WIN_ROWS)
        ro = np.clip(krel - qr + NA_WIN_ROWS - 1, 0, 2 * NA_WIN_ROWS - 2)
        pick_r = (ro.reshape(-1, 1) == np.arange(2 * NA_WIN_ROWS - 1)[None, :]).astype(np.float32)
        t = jnp.einsum("pr,hrx->hpx", jnp.asarray(pick_r), by_col, precision=lax.Precision.HIGHEST)
        t = t.reshape(heads, NA_QROWS, NA_KROWS, GRID_W, GRID_W).transpose(0, 1, 3, 2, 4)
        ok = jnp.logical_and(jnp.asarray(row_ok)[None, :, None, :, None],
                             jnp.asarray(col_ok)[None, None, :, None, :])
        tabs.append(jnp.where(ok, t, NEG_INF).reshape(heads, NA_QROWS * GRID_W, NA_KROWS * GRID_W))
    return jnp.stack(tabs)


def _na_kernel(q_ref, k_ref, v_ref, kc_ref, vc_ref, bias_ref, o_ref, *, rows, bounded):
    rb = pl.program_id(2)
    kr0 = jnp.clip(rb * NA_QROWS - NA_WIN_ROWS // 2, 0, rows - NA_KROWS)
    start = pl.multiple_of(kr0 * GRID_W, GRID_W)
    nkeys = NA_KROWS * GRID_W
    accs = []
    for hh in range(2):
        q = q_ref[0, hh]
        kb = k_ref[0, hh, pl.ds(start, nkeys), :]
        vb = v_ref[0, hh, pl.ds(start, nkeys), :]
        s = lax.dot_general(q, kb, _DN_T, preferred_element_type=F32) + bias_ref[0, hh]
        sc = lax.dot_general(q, kc_ref[0, hh], _DN_T, preferred_element_type=F32)
        if not bounded:
            m = jnp.maximum(jnp.max(s, -1, keepdims=True), jnp.max(sc, -1, keepdims=True))
            s, sc = s - m, sc - m
        accs.append(jnp.dot(jnp.exp2(s).astype(BF16), vb, preferred_element_type=F32)
                    + jnp.dot(jnp.exp2(sc).astype(BF16), vc_ref[0, hh], preferred_element_type=F32))
    o_ref[0] = _pair_output(accs)


def _na_latent(nq, nk, nv, nkc, nvc, bias, bounded):
    b, heads, n, _ = nq.shape
    rows = n // GRID_W
    assert rows >= NA_KROWS and rows % NA_QROWS == 0
    tq = NA_QROWS * GRID_W
    nctx = nkc.shape[2]

    def bias_idx(i, p, j):
        kr0 = jnp.clip(j * NA_QROWS - NA_WIN_ROWS // 2, 0, rows - NA_KROWS)
        return ((j * NA_QROWS - kr0) // 4, p, 0, 0)

    res = pl.BlockSpec((1, 2, n, LANES), lambda i, p, j: (i, p, 0, 0))
    ctx = pl.BlockSpec((1, 2, nctx, LANES), lambda i, p, j: (i, p, 0, 0))

    def call(flag):
        return pl.pallas_call(
            functools.partial(_na_kernel, rows=rows, bounded=flag),
            grid=(b, heads // 2, rows // NA_QROWS),
            in_specs=[pl.BlockSpec((1, 2, tq, LANES), lambda i, p, j: (i, p, j, 0)), res, res, ctx, ctx,
                      pl.BlockSpec((1, 2, tq, NA_KROWS * GRID_W), bias_idx)],
            out_specs=pl.BlockSpec((1, tq, LANES), lambda i, p, j: (i, j, p)),
            out_shape=jax.ShapeDtypeStruct((b, n, heads * 64), F32),
            compiler_params=_cparams(("parallel", "parallel", "parallel"), VMEM_LIMIT),
            name="na_latent" if flag else "na_latent_rowmax",
        )

    return lax.cond(bounded, call(True), call(False), nq, nk, nv, nkc, nvc, bias)


def _outproj_kernel(x_ref, oa_ref, of_ref, op_ref, on_ref, w_ref, ga_ref, g_ref, sh_ref, sc_ref, *rest,
                    route):
    if route:
        wrh_ref, wrl_ref, x1_ref, h2_ref, rt_ref = rest
    else:
        x1_ref, h2_ref = rest
    gw = oa_ref.shape[-1]
    o = None
    for gi, r in enumerate((oa_ref, of_ref, op_ref, on_ref)):
        t = jnp.dot(r[0].astype(BF16), w_ref[gi * gw:(gi + 1) * gw, :], preferred_element_type=F32)
        o = t if o is None else o + t
    x1 = x_ref[0] + ga_ref[0] * o
    x1_ref[0] = x1
    h = (x1 * _rms(x1, x1.shape[-1])) * g_ref[...]
    h = h * (1.0 + sc_ref[0]) + sh_ref[0]
    h2_ref[0] = h.astype(h2_ref.dtype)
    if route:
        n_exp = 8
        lane = lax.broadcasted_iota(jnp.int32, (1, LANES), 1)
        hh = h.astype(BF16)
        hl = (h - hh.astype(F32)).astype(BF16)
        lg = (jnp.dot(hh, wrh_ref[...], preferred_element_type=F32)
              + (jnp.dot(hl, wrh_ref[...], preferred_element_type=F32)
                 + jnp.dot(hh, wrl_ref[...], preferred_element_type=F32)))
        lg = jnp.where(lane < n_exp, lg, NEG_INF)
        v1 = jnp.max(lg, -1, keepdims=True)
        i1 = jnp.min(jnp.where(lg == v1, lane, LANES), -1, keepdims=True)
        lg2 = jnp.where(lane == i1, NEG_INF, lg)
        v2 = jnp.max(lg2, -1, keepdims=True)
        i2 = jnp.min(jnp.where(lg2 == v2, lane, LANES), -1, keepdims=True)
        e2 = jnp.exp(v2 - v1)
        g1 = 1.0 / (1.0 + e2)
        g2 = e2 / (1.0 + e2)
        rt_ref[0] = jnp.where(lane == 0, i1.astype(F32),
                              jnp.where(lane == 1, i2.astype(F32),
                                        jnp.where(lane == 2, g1, jnp.where(lane == 3, g2, 0.0))))


def _outproj(x, oa, of, op, on, w_out, ga, g, sh, sc, tm, w_router=None):
    b, n, d = x.shape
    route = w_router is not None
    gw = oa.shape[-1]
    tok = lambda w: pl.BlockSpec((1, tm, w), lambda i, j: (i, j, 0))
    per_b = pl.BlockSpec((1, 1, d), lambda i, j: (i, 0, 0))
    full = lambda a: pl.BlockSpec(a.shape, lambda i, j: (0,) * a.ndim)
    ins = [x, oa, of, op, on, w_out, ga, g, sh, sc]
    in_specs = [tok(d), tok(gw), tok(gw), tok(gw), tok(gw), full(w_out), per_b, full(g), per_b, per_b]
    out_specs = [tok(d), tok(d)]
    out_shape = [jax.ShapeDtypeStruct((b, n, d), F32),
                 jax.ShapeDtypeStruct((b, n, d), F32 if route else BF16)]
    if route:
        wr = jnp.pad(w_router.astype(F32), ((0, 0), (0, LANES - w_router.shape[-1])))
        wr_hi = wr.astype(BF16)
        wr_lo = (wr - wr_hi.astype(F32)).astype(BF16)
        ins += [wr_hi, wr_lo]
        in_specs += [full(wr_hi), full(wr_lo)]
        out_specs.append(tok(LANES))
        out_shape.append(jax.ShapeDtypeStruct((b, n, LANES), F32))
    return pl.pallas_call(
        functools.partial(_outproj_kernel, route=route),
        grid=(b, n // tm),
        in_specs=in_specs, out_specs=out_specs, out_shape=out_shape,
        compiler_params=_cparams(("parallel", "parallel"), VMEM_LIMIT),
        name="outproj",
    )(*ins)


def _ffn_kernel(h_ref, x_ref, ga_ref, w1_ref, w3_ref, w2_ref, o_ref, acc_ref, *, fc):
    h = h_ref[0]
    ff = w1_ref.shape[1]
    for j in range(ff // fc):
        a = jnp.dot(h, w1_ref[:, j * fc:(j + 1) * fc], preferred_element_type=F32)
        bb = jnp.dot(h, w3_ref[:, j * fc:(j + 1) * fc], preferred_element_type=F32)
        t = jnp.dot((_silu(a) * bb).astype(BF16), w2_ref[j * fc:(j + 1) * fc, :], preferred_element_type=F32)
        if j == 0:
            acc_ref[...] = t
        else:
            acc_ref[...] += t
    o_ref[0] = x_ref[0] + ga_ref[0] * acc_ref[...]


def _ffn_dense(h2, x1, ga, w1, w3, w2, tm):
    b, n, d = x1.shape
    ff = w1.shape[1]
    fc = 256 if ff % 256 == 0 else ff
    tok = pl.BlockSpec((1, tm, d), lambda i, j: (i, j, 0))
    per_b = pl.BlockSpec((1, 1, d), lambda i, j: (i, 0, 0))
    full = lambda a: pl.BlockSpec(a.shape, lambda i, j: (0, 0), pipeline_mode=pl.Buffered(1))
    return pl.pallas_call(
        functools.partial(_ffn_kernel, fc=fc),
        grid=(b, n // tm),
        in_specs=[tok, tok, per_b, full(w1), full(w3), full(w2)],
        out_specs=tok,
        out_shape=jax.ShapeDtypeStruct((b, n, d), F32),
        scratch_shapes=[pltpu.VMEM((tm, d), F32)],
        compiler_params=_cparams(("parallel", "parallel"), VMEM_LIMIT),
        name="ffn_dense",
    )(h2, x1, ga, w1, w3, w2)


def _expert_kernel(te_ref, tv_ref, src_ref, h_hbm, w1_ref, w3_ref, w2_ref, y_ref, xs_ref, sems,
                   *, fc, n_tiles):
    i = pl.program_id(0)
    tme = y_ref.shape[0]
    nf = w1_ref.shape[2] // fc
    valid = tv_ref[i] > 0
    slot = i % 2

    def start_row(tile, r, sl):
        pltpu.make_async_copy(h_hbm.at[src_ref[tile * tme + r]], xs_ref.at[sl, r], sems.at[sl]).start()

    def wait_slot(sl):
        pltpu.make_async_copy(h_hbm.at[pl.ds(0, tme)], xs_ref.at[sl], sems.at[sl]).wait()

    @pl.when(i == 0)
    def _():
        def issue(r, c):
            start_row(0, r, 0)
            return c

        lax.fori_loop(0, tme, issue, 0, unroll=8)

    @pl.when(jnp.logical_or(i == 0, tv_ref[jnp.maximum(i - 1, 0)] > 0))
    def _():
        wait_slot(slot)

    @pl.when(valid)
    def _():
        nxt = jnp.minimum(i + 1, n_tiles - 1)
        xb = xs_ref[slot].astype(BF16)
        cuts = [(k * tme) // nf for k in range(nf + 1)]
        for j in range(nf):
            for r in range(cuts[j], cuts[j + 1]):
                start_row(nxt, r, 1 - slot)
            a = jnp.dot(xb, w1_ref[0, :, j * fc:(j + 1) * fc], preferred_element_type=F32)
            bb = jnp.dot(xb, w3_ref[0, :, j * fc:(j + 1) * fc], preferred_element_type=F32)
            t = jnp.dot((_silu(a) * bb).astype(BF16), w2_ref[0, j * fc:(j + 1) * fc, :],
                        preferred_element_type=F32)
            if j == 0:
                y_ref[...] = t
            else:
                y_ref[...] += t

    @pl.when(jnp.logical_not(valid))
    def _():
        y_ref[...] = jnp.zeros_like(y_ref)

    @pl.when(jnp.logical_and(i == n_tiles - 1, valid))
    def _():
        wait_slot(1 - slot)


def _combine_kernel(d0_ref, d1_ref, ys_hbm, x_ref, ga_ref, rt_ref, o_ref, buf_ref, sem):
    i = pl.program_id(0)
    j = pl.program_id(1)
    tc = x_ref.shape[1]
    base = (i * pl.num_programs(1) + j) * tc

    def issue(r, c):
        pltpu.make_async_copy(ys_hbm.at[d0_ref[base + r]], buf_ref.at[0, r], sem).start()
        pltpu.make_async_copy(ys_hbm.at[d1_ref[base + r]], buf_ref.at[1, r], sem).start()
        return c

    lax.fori_loop(0, tc, issue, 0, unroll=8)
    pltpu.make_async_copy(ys_hbm.at[pl.ds(0, tc)], buf_ref.at[0], sem).wait()
    pltpu.make_async_copy(ys_hbm.at[pl.ds(0, tc)], buf_ref.at[1], sem).wait()
    rt = rt_ref[0]
    y = rt[:, TOP_K:TOP_K + 1] * buf_ref[0] + rt[:, TOP_K + 1:TOP_K + 2] * buf_ref[1]
    o_ref[0] = x_ref[0] + ga_ref[0] * y


def _moe(h2, x1, ga, rt, w1, w3, w2, tme, tc):
    b, n_b, d = h2.shape
    n = b * n_b
    n_exp, _, ff = w1.shape
    fc = 512 if ff % 512 == 0 else ff
    e_idx = rt[..., 0:TOP_K].astype(jnp.int32).reshape(-1)
    onehot = (e_idx[:, None] == jnp.arange(n_exp)[None, :]).astype(jnp.int32)
    pos = jnp.take_along_axis(jnp.cumsum(onehot, axis=0) - onehot, e_idx[:, None], axis=1)[:, 0]
    counts = jnp.sum(onehot, axis=0)
    padded = ((counts + tme - 1) // tme) * tme
    ends = jnp.cumsum(padded)
    offs = ends - padded
    dest = offs[e_idx] + pos
    n_rows = n * TOP_K + n_exp * tme
    n_tiles = n_rows // tme
    src = jnp.zeros((n_rows,), jnp.int32).at[dest].set(jnp.arange(n * TOP_K, dtype=jnp.int32) // TOP_K)
    tile_start = jnp.arange(n_tiles, dtype=jnp.int32) * tme
    tvalid = (tile_start < ends[-1]).astype(jnp.int32)
    texp = jnp.minimum(jnp.sum((tile_start[:, None] >= ends[None, :]).astype(jnp.int32), axis=1), n_exp - 1)
    texp = jnp.where(tvalid > 0, texp, texp[jnp.maximum(ends[-1] // tme - 1, 0)])
    wspec = lambda shape: pl.BlockSpec((1,) + shape, lambda i, te, tv, s: (te[i], 0, 0),
                                       pipeline_mode=pl.Buffered(1))
    ys = pl.pallas_call(
        functools.partial(_expert_kernel, fc=fc, n_tiles=n_tiles),
        grid_spec=pltpu.PrefetchScalarGridSpec(
            num_scalar_prefetch=3,
            grid=(n_tiles,),
            in_specs=[pl.BlockSpec(memory_space=pl.ANY), wspec((d, ff)), wspec((d, ff)), wspec((ff, d))],
            out_specs=pl.BlockSpec((tme, d), lambda i, te, tv, s: (i, 0)),
            scratch_shapes=[pltpu.VMEM((2, tme, d), F32), pltpu.SemaphoreType.DMA((2,))]),
        out_shape=jax.ShapeDtypeStruct((n_rows, d), F32),
        compiler_params=_cparams(("arbitrary",), VMEM_LIMIT),
        name="experts",
    )(texp, tvalid, src, h2.reshape(n, d), w1, w3, w2)

    d2 = dest.reshape(n, TOP_K)
    tok = lambda w: pl.BlockSpec((1, tc, w), lambda i, j, a, b_: (i, j, 0))
    return pl.pallas_call(
        _combine_kernel,
        grid_spec=pltpu.PrefetchScalarGridSpec(
            num_scalar_prefetch=2,
            grid=(b, n_b // tc),
            in_specs=[pl.BlockSpec(memory_space=pl.ANY), tok(d),
                      pl.BlockSpec((1, 1, d), lambda i, j, a, b_: (i, 0, 0)), tok(LANES)],
            out_specs=tok(d),
            scratch_shapes=[pltpu.VMEM((2, tc, d), F32), pltpu.SemaphoreType.DMA(())]),
        out_shape=jax.ShapeDtypeStruct((b, n_b, d), F32),
        compiler_params=_cparams(("arbitrary", "arbitrary"), VMEM_LIMIT),
        name="moe_combine",
    )(d2[:, 0], d2[:, 1], ys, x1, ga, rt)


def _pick_tile(n, pref):
    t = pref
    while n % t:
        t //= 2
    return t


def kernel(x, c, ctx, c_ctx, w_ada, b_ada, g_mix, g_ffn, w_in, w_out, g_cq, g_ckv, w_uq, w_ukv,
           g_mla_qn, g_mla_qr, g_mla_kn, g_mla_kr, w_fourier, w_pool, pool_scale, g_na_q, g_na_k,
           na_rpb, w1_dense, w3_dense, w2_dense, w_router, w1_moe, w3_moe, w2_moe):
    b, n, d = x.shape
    nctx = ctx.shape[1]
    depth = w_ada.shape[0]
    assert b <= 7 and n % GRID_W == 0

    cc = jnp.zeros((8, d), F32).at[:b].set(c).at[b].set(c_ctx)
    mods = _adaln(cc, w_ada, b_ada)
    cos_x, sin_x = _rope_tables(n)
    cos_c = jnp.ones((nctx, LANES), F32)
    sin_c = jnp.zeros((nctx, LANES), F32)
    sw = _rope_swap_perm()
    tm_x = _pick_tile(n, 256)
    tm_c = _pick_tile(nctx, 256)
    amax = lambda g: jnp.max(jnp.abs(g.astype(F32)))

    for l in range(depth):
        last = l == depth - 1
        mx = mods[l, :b].reshape(b, 1, 6 * d)
        mc = jnp.broadcast_to(mods[l, b].reshape(1, 1, 6 * d), (b, 1, 6 * d))
        part = lambda m, k: m[:, :, k * d:(k + 1) * d]

        bound_mla = (MLA_SCALE * LOG2E) * (MLA_NOPE * amax(g_mla_qn[l]) * amax(g_mla_kn[l])
                                           + MLA_ROPE * amax(g_mla_qr[l]) * amax(g_mla_kr[l]))
        qk_na = (NA_SCALE * LOG2E) * NA_HEAD_DIM * amax(g_na_q[l]) * amax(g_na_k[l])
        rpb_hi = jnp.maximum(jnp.max(na_rpb[l]).astype(F32), 0.0) * LOG2E
        rpb_lo = jnp.minimum(jnp.min(na_rpb[l]).astype(F32), 0.0) * LOG2E
        bound_na = qk_na + rpb_hi
        mla_bounded = 2.0 * bound_mla <= MAX_SOFTMAX_GAP_LOG2
        na_bounded = 2.0 * qk_na + rpb_hi - rpb_lo <= MAX_SOFTMAX_GAP_LOG2
        bounds = jnp.zeros((1, LANES), F32).at[0, 0].set(bound_mla).at[0, 1].set(bound_na)

        wts = [
            _prep_in_weights(w_in[l]),
            g_cq[l].reshape(1, -1), _prep_uq(w_uq[l]),
            _lane_vec([(g_mla_qn[l], 64), (g_mla_qr[l], 32), (None, 32)]),
            _lane_vec([(None, 64), (g_mla_qr[l][sw], 32), (None, 32)]),
            g_ckv[l].reshape(1, -1), _prep_ukv(w_ukv[l]),
            _lane_vec([(g_mla_kn[l], 64), (None, 64)]),
            _lane_vec([(None, 64), (g_mla_kr[l], 32), (None, 32)]),
            _lane_vec([(None, 64), (g_mla_kr[l][sw], 32), (None, 32)]),
            jnp.tile(g_na_q[l], NA_HEADS).reshape(1, -1), jnp.tile(g_na_k[l], NA_HEADS).reshape(1, -1),
        ]
        gmix = g_mix[l].reshape(1, d)
        qx, kx, vx, fx, px, nqx, nkx, nvx = _inproj(x, part(mx, 0), part(mx, 1), gmix, cos_x, sin_x,
                                                    bounds, wts, tm_x)
        qc, kc, vc, fc_, pc, nqc, nkc, nvc = _inproj(ctx, part(mc, 0), part(mc, 1), gmix, cos_c, sin_c,
                                                     bounds, wts, tm_c)

        wf = w_fourier[l].astype(BF16)
        cg = w_pool.shape[-1]
        wp_bd = jnp.zeros((len(POOL_WINDOWS) * cg,) * 2, F32)
        for gi in range(len(POOL_WINDOWS)):
            wp_bd = wp_bd.at[gi * cg:(gi + 1) * cg, gi * cg:(gi + 1) * cg].set(w_pool[l, gi])
        wp_bd = wp_bd.astype(BF16)
        ps = pool_scale[l].reshape(1, -1)
        w_out_l = w_out[l].astype(BF16)
        gffn = g_ffn[l].reshape(1, d)

        o_mla = _flash_pairs(qx, jnp.concatenate([kx, kc], axis=2), jnp.concatenate([vx, vc], axis=2),
                             _pick_tile(n, 256), mla_bounded)
        o_f = _fourier_latent(fx, wf)
        o_p = _pool(px, wp_bd, ps, _pick_tile(n, 512))
        o_na = _na_latent(nqx, nkx, nvx, nkc, nvc, _na_bias_tables(na_rpb[l]), na_bounded)

        moe_layer = l % 2 == 1
        i = l // 2
        res = _outproj(x, o_mla, o_f, o_p, o_na, w_out_l, part(mx, 2), gffn, part(mx, 3), part(mx, 4), tm_x,
                       w_router=w_router[i] if moe_layer else None)
        if moe_layer:
            x1, h2, rt = res
            w1, w3, w2 = w1_moe[i].astype(BF16), w3_moe[i].astype(BF16), w2_moe[i].astype(BF16)
            x = _moe(h2, x1, part(mx, 5), rt, w1, w3, w2, tme=512, tc=256)
        else:
            x1, h2 = res
            w1, w3, w2 = w1_dense[i].astype(BF16), w3_dense[i].astype(BF16), w2_dense[i].astype(BF16)
            x = _ffn_dense(h2, x1, part(mx, 5), w1, w3, w2, _pick_tile(n, 512))

        if not last:
            oc_mla = _flash_pairs(qc, kc, vc, tm_c, mla_bounded)
            oc_f = _fourier_dense(fc_, wf)
            oc_p = _pool(pc, wp_bd, ps, tm_c)
            oc_na = _flash_pairs(nqc, nkc, nvc, tm_c, na_bounded)
            resc = _outproj(ctx, oc_mla, oc_f, oc_p, oc_na, w_out_l, part(mc, 2), gffn, part(mc, 3),
                            part(mc, 4), tm_c, w_router=w_router[i] if moe_layer else None)
            if moe_layer:
                c1, hc2, rtc = resc
                ctx = _moe(hc2, c1, part(mc, 5), rtc, w1, w3, w2, tme=256, tc=256)
            else:
                c1, hc2 = resc
                ctx = _ffn_dense(hc2, c1, part(mc, 5), w1, w3, w2, tm_c)
    return x
```

```python
import functools
import math

import numpy as np
import jax
import jax.numpy as jnp
from jax import lax
from jax.experimental import pallas as pl
from jax.experimental.pallas import tpu as pltpu

F32 = jnp.float32
BF16 = jnp.bfloat16

GRID_W = 64
LANES = 128
EPS = 1e-6
NEG_INF = -1e30

MLA_HEADS = 4
MLA_NOPE = 64
MLA_ROPE = 32
MLA_V = 64
MLA_SCALE = (MLA_NOPE + MLA_ROPE) ** -0.5
MLA_SPARE_LANE = MLA_NOPE + MLA_ROPE
ROPE_BASE = 10000.0
LOG2E = 1.0 / math.log(2.0)
MAX_SOFTMAX_GAP_LOG2 = 120.0

FOURIER_GROUPS = 4
POOL_WINDOWS = (2, 4, 8, 16)
POOL_HALO = 8

NA_HEADS = 4
NA_HEAD_DIM = 64
NA_SCALE = NA_HEAD_DIM ** -0.5
NA_WIN_ROWS = 8
NA_WIN_COLS = 16
NA_QROWS = 8
NA_KROWS = 16

TOP_K = 2
VMEM_LIMIT = 56 * 1024 * 1024


def _cparams(sem, vmem=None):
    return pltpu.CompilerParams(dimension_semantics=sem, vmem_limit_bytes=vmem)


def _silu(a):
    return a / (1.0 + jnp.exp(-a))


def _rms(v, n):
    return lax.rsqrt(jnp.sum(v * v, axis=-1, keepdims=True) * (1.0 / n) + EPS)


def _adaln_kernel(c_ref, w_ref, b_ref, o_ref):
    o_ref[0] = jnp.dot(_silu(c_ref[...]), w_ref[0], preferred_element_type=F32) + b_ref[0]


def _adaln(cc, w_ada, b_ada):
    depth, d, d6 = w_ada.shape
    tn = 512
    return pl.pallas_call(
        _adaln_kernel,
        grid=(depth, d6 // tn),
        in_specs=[pl.BlockSpec((8, d), lambda l, j: (0, 0)),
                  pl.BlockSpec((1, d, tn), lambda l, j: (l, 0, j)),
                  pl.BlockSpec((1, 1, tn), lambda l, j: (l, 0, j))],
        out_specs=pl.BlockSpec((1, 8, tn), lambda l, j: (l, 0, j)),
        out_shape=jax.ShapeDtypeStruct((depth, 8, d6), F32),
        compiler_params=_cparams(("arbitrary", "arbitrary")),
        name="adaln",
    )(cc, w_ada, b_ada.reshape(depth, 1, d6))


IN_COLS = 1920


def _rope_swap_perm():
    j = np.arange(MLA_ROPE)
    return np.where((j % 16) < 8, j + 8, j - 8)


def _prep_in_weights(w_in_l):
    d = w_in_l.shape[0]
    s = [0, 256, 384, 416, 672, 928, 1184, 1440, 1696]
    cq, ckv, kr, f, p, nq, nk, nv = [w_in_l[:, s[i]:s[i + 1]] for i in range(8)]
    z = lambda n: jnp.zeros((d, n), w_in_l.dtype)
    krsw = kr[:, _rope_swap_perm()]
    return jnp.concatenate([cq, ckv, f, p, nq, nk, nv, z(64), kr, z(32), z(64), krsw, z(32)],
                           axis=1).astype(BF16)


def _prep_uq(w_uq_l):
    r = w_uq_l.shape[0]
    w = w_uq_l.reshape(r, MLA_HEADS, MLA_NOPE + MLA_ROPE)
    z = lambda n: jnp.zeros((r, MLA_HEADS, n), w.dtype)
    main = jnp.concatenate([w, z(32)], axis=-1)
    sw = jnp.concatenate([z(64), w[..., MLA_NOPE:][..., _rope_swap_perm()], z(32)], axis=-1)
    return jnp.concatenate([main.reshape(r, -1), sw.reshape(r, -1)], axis=1).astype(BF16)


def _prep_ukv(w_ukv_l):
    r = w_ukv_l.shape[0]
    w = w_ukv_l.reshape(r, MLA_HEADS, MLA_NOPE + MLA_V)
    z = jnp.zeros((r, 64), w.dtype)
    tiles = [jnp.concatenate([w[:, h, :MLA_NOPE], z], axis=1) for h in range(MLA_HEADS)]
    for h in range(MLA_HEADS):
        v = w[:, h, MLA_NOPE:]
        tiles.append(jnp.concatenate([v, z] if h % 2 == 0 else [z, v], axis=1))
    return jnp.concatenate(tiles, axis=1).astype(BF16)


def _lane_vec(parts):
    cols = [jnp.zeros((n,), F32) if a is None else a.astype(F32) for a, n in parts]
    return jnp.concatenate(cols).reshape(1, -1)


def _rope_tables(n_tokens):
    t = np.arange(n_tokens)
    pos = np.stack([t // GRID_W, t % GRID_W], axis=1).astype(np.float64)
    half = MLA_ROPE // 2
    inv = 1.0 / (ROPE_BASE ** (np.arange(0, half, 2, dtype=np.float64) / half))
    j = np.arange(MLA_ROPE)
    ang = pos[:, j // 16] * inv[j % 8]
    sign = np.where((j % 16) < 8, -1.0, 1.0)
    cos = np.ones((n_tokens, LANES), np.float32)
    sin = np.zeros((n_tokens, LANES), np.float32)
    cos[:, 64:96] = np.cos(ang)
    sin[:, 64:96] = np.sin(ang) * sign
    return jnp.asarray(cos), jnp.asarray(sin)


def _inproj_kernel(x_ref, sh_ref, sc_ref, g_ref, cos_ref, sin_ref, bound_ref, win_ref,
                   gcq_ref, wuq_ref, gq_ref, gqsw_ref,
                   gckv_ref, wukv_ref, gkn_ref, gkr_ref, gkrsw_ref, gnq_ref, gnk_ref,
                   q_ref, k_ref, v_ref, f_ref, p_ref, nq_ref, nk_ref, nv_ref):
    x = x_ref[0]
    d = x.shape[-1]
    h = (x * _rms(x, d)) * g_ref[...]
    h = h * (1.0 + sc_ref[0]) + sh_ref[0]
    u = jnp.dot(h.astype(BF16), win_ref[...], preferred_element_type=F32)
    cq, ckv = u[:, 0:256], u[:, 256:384]
    f_ref[0] = u[:, 384:640].astype(BF16)
    p_ref[0] = u[:, 640:896]
    nq, nk, nv = u[:, 896:1152], u[:, 1152:1408], u[:, 1408:1664]
    krb, krs = u[:, 1664:1792], u[:, 1792:1920]
    cos, sin = cos_ref[...], sin_ref[...]
    lane = lax.broadcasted_iota(jnp.int32, (1, LANES), 1)
    m_nope = lane < MLA_NOPE
    m_rope = jnp.logical_and(lane >= MLA_NOPE, lane < MLA_NOPE + MLA_ROPE)
    bound_mla, bound_na = bound_ref[:, 0:1], bound_ref[:, 1:2]

    cqn = (cq * _rms(cq, 256) * gcq_ref[...]).astype(BF16)
    qall = jnp.dot(cqn, wuq_ref[...], preferred_element_type=F32)
    for hd in range(MLA_HEADS):
        blk = qall[:, hd * LANES:(hd + 1) * LANES]
        sw = qall[:, (MLA_HEADS + hd) * LANES:(MLA_HEADS + hd + 1) * LANES]
        sq = blk * blk
        rn = lax.rsqrt(jnp.sum(jnp.where(m_nope, sq, 0.0), -1, keepdims=True) * (1.0 / MLA_NOPE) + EPS)
        rr = lax.rsqrt(jnp.sum(jnp.where(m_rope, sq, 0.0), -1, keepdims=True) * (1.0 / MLA_ROPE) + EPS)
        qh = blk * jnp.where(m_nope, rn, rr) * gq_ref[...]
        qs = sw * rr * gqsw_ref[...]
        q_ref[0, hd] = jnp.where(lane == MLA_SPARE_LANE, -bound_mla,
                                 (qh * cos + qs * sin) * (MLA_SCALE * LOG2E)).astype(BF16)

    ckvn = (ckv * _rms(ckv, 128) * gckv_ref[...]).astype(BF16)
    kvall = jnp.dot(ckvn, wukv_ref[...], preferred_element_type=F32)
    rkr = _rms(krb, MLA_ROPE)
    krot = (krb * rkr * gkr_ref[...]) * cos + (krs * rkr * gkrsw_ref[...]) * sin
    for hd in range(MLA_HEADS):
        blk = kvall[:, hd * LANES:(hd + 1) * LANES]
        k_ref[0, hd] = jnp.where(lane == MLA_SPARE_LANE, 1.0,
                                 blk * _rms(blk, MLA_NOPE) * gkn_ref[...] + krot).astype(BF16)
        vb = kvall[:, (MLA_HEADS + hd) * LANES:(MLA_HEADS + hd + 1) * LANES]
        one_lane = 64 if hd % 2 == 0 else 0
        v_ref[0, hd] = jnp.where(lane == one_lane, 1.0, vb).astype(BF16)

    lane2 = lax.broadcasted_iota(jnp.int32, (1, 2 * LANES), 1)

    def seg_rms(t):
        sq = t * t
        r = jnp.zeros_like(t)
        for s in range(NA_HEADS):
            m = (lane2 // NA_HEAD_DIM) == s
            ss = jnp.sum(jnp.where(m, sq, 0.0), -1, keepdims=True)
            r = jnp.where(m, lax.rsqrt(ss * (1.0 / NA_HEAD_DIM) + EPS), r)
        return r

    nqn = nq * seg_rms(nq) * gnq_ref[...] * (NA_SCALE * LOG2E)
    nkn = nk * seg_rms(nk) * gnk_ref[...]
    for hd in range(NA_HEADS):
        t0 = (hd // 2) * LANES
        own = (lane < 64) if hd % 2 == 0 else (lane >= 64)
        one_lane = 64 if hd % 2 == 0 else 0
        spare = lane == (LANES - 1 if hd % 2 == 0 else 0)
        nq_ref[0, hd] = jnp.where(own, nqn[:, t0:t0 + LANES], jnp.where(spare, -bound_na, 0.0)).astype(BF16)
        nk_ref[0, hd] = jnp.where(own, nkn[:, t0:t0 + LANES], jnp.where(spare, 1.0, 0.0)).astype(BF16)
        nv_ref[0, hd] = jnp.where(own, nv[:, t0:t0 + LANES],
                                  jnp.where(lane == one_lane, 1.0, 0.0)).astype(BF16)


def _inproj(x, sh, sc, g, cos, sin, bounds, wts, tm):
    b, n, d = x.shape
    heads = MLA_HEADS
    row = lambda a: pl.BlockSpec(a.shape, lambda i, j: (0,) * a.ndim)
    per_b = pl.BlockSpec((1, 1, d), lambda i, j: (i, 0, 0))
    tok = lambda w: pl.BlockSpec((1, tm, w), lambda i, j: (i, j, 0))
    hd_spec = pl.BlockSpec((1, heads, tm, LANES), lambda i, j: (i, 0, j, 0))
    tab = pl.BlockSpec((tm, LANES), lambda i, j: (j, 0))
    hshape = jax.ShapeDtypeStruct((b, heads, n, LANES), BF16)
    return pl.pallas_call(
        _inproj_kernel,
        grid=(b, n // tm),
        in_specs=[tok(d), per_b, per_b, row(g), tab, tab, row(bounds)] + [row(w) for w in wts],
        out_specs=[hd_spec, hd_spec, hd_spec, tok(256), tok(256), hd_spec, hd_spec, hd_spec],
        out_shape=[hshape, hshape, hshape,
                   jax.ShapeDtypeStruct((b, n, 256), BF16), jax.ShapeDtypeStruct((b, n, 256), F32),
                   hshape, hshape, hshape],
        compiler_params=_cparams(("parallel", "parallel"), VMEM_LIMIT),
        name="inproj",
    )(x, sh, sc, g, cos, sin, bounds, *wts)


_DN_T = (((1,), (1,)), ((), ()))


def _pair_output(accs):
    lane = lax.broadcasted_iota(jnp.int32, (1, LANES), 1)
    out = None
    for hh, acc in enumerate(accs):
        one_lane = 64 if hh == 0 else 0
        own = (lane < 64) if hh == 0 else (lane >= 64)
        o = jnp.where(own, acc / acc[:, one_lane:one_lane + 1], 0.0)
        out = o if out is None else out + o
    return out


def _flash_pair_kernel(q_ref, *refs, chunks, bounded):
    o_ref = refs[-1]
    tq = q_ref.shape[2]
    accs = []
    for hh in range(2):
        q = q_ref[0, hh]
        acc = jnp.zeros((tq, LANES), F32)
        m = jnp.full((tq, 1), NEG_INF, F32)
        for kset, start, size in chunks:
            ks = refs[2 * kset][0, hh, start:start + size, :]
            vs = refs[2 * kset + 1][0, hh, start:start + size, :]
            s = lax.dot_general(q, ks, _DN_T, preferred_element_type=F32)
            if bounded:
                acc = acc + jnp.dot(jnp.exp2(s).astype(BF16), vs, preferred_element_type=F32)
            else:
                m_new = jnp.maximum(m, jnp.max(s, axis=-1, keepdims=True))
                p = jnp.exp2(s - m_new).astype(BF16)
                acc = acc * jnp.exp2(m - m_new) + jnp.dot(p, vs, preferred_element_type=F32)
                m = m_new
        accs.append(acc)
    o_ref[0] = _pair_output(accs)


def _flash_pairs(q, kv_sets, tq, bounded, tk=2048):
    b, heads, lq, _ = q.shape
    chunks, ops, kv_specs = [], [], []
    for si, (k, v) in enumerate(kv_sets):
        lk = k.shape[2]
        step = tk if lk % tk == 0 else lk
        chunks += [(si, st, step) for st in range(0, lk, step)]
        ops += [k, v]
        kv_specs += [pl.BlockSpec((1, 2, lk, LANES), lambda i, p, j: (i, p, 0, 0))] * 2

    def call(flag):
        return pl.pallas_call(
            functools.partial(_flash_pair_kernel, chunks=tuple(chunks), bounded=flag),
            grid=(b, heads // 2, lq // tq),
            in_specs=[pl.BlockSpec((1, 2, tq, LANES), lambda i, p, j: (i, p, j, 0))] + kv_specs,
            out_specs=pl.BlockSpec((1, tq, LANES), lambda i, p, j: (i, j, p)),
            out_shape=jax.ShapeDtypeStruct((b, lq, heads * 64), F32),
            compiler_params=_cparams(("parallel", "parallel", "parallel"), VMEM_LIMIT),
            name="flash_pairs" if flag else "flash_pairs_online",
        )

    return lax.cond(bounded, call(True), call(False), q, *ops)


def _table(a):
    return jnp.asarray(a, F32).astype(BF16)


def _dft_consts(n_tokens):
    rows = n_tokens // GRID_W
    r = np.arange(rows)
    ang_r = 2 * np.pi * np.outer(r, r) / rows
    a_mat = np.concatenate([np.cos(ang_r), -np.sin(ang_r)], axis=0)
    c = np.arange(GRID_W)
    ang_t = 2 * np.pi * np.outer(r, c) / n_tokens
    ang_c = 2 * np.pi * np.outer(c, c) / GRID_W
    cc, sc = np.cos(ang_c), np.sin(ang_c)
    m_b = np.block([[cc, sc], [-sc, cc]])
    return a_mat, np.cos(ang_t), np.sin(ang_t), m_b


def _channel_dft(n_tokens, width):
    cg = width // FOURIER_GROUPS
    j = np.arange(cg)
    ang = 2 * np.pi * np.outer(j, j) / cg
    eye = np.eye(FOURIER_GROUPS)
    scale = 1.0 / math.sqrt(n_tokens * cg)
    return np.kron(eye, np.cos(ang)) * scale, np.kron(eye, np.sin(ang)) * scale


def _fourier_rows_kernel(u_ref, a_ref, tc_ref, ts_ref, o_ref):
    z = jnp.dot(a_ref[...], u_ref[0], preferred_element_type=F32)
    r = z.shape[0] // 2
    zr, zi = z[:r], z[r:]
    c, s = tc_ref[...], ts_ref[...]
    o_ref[0, :r] = (zr * c + zi * s).astype(BF16)
    o_ref[0, r:] = (zi * c - zr * s).astype(BF16)


def _fourier_cols_kernel(z_ref, mb_ref, cs_ref, wf_ref, o_ref):
    kb = z_ref.shape[1]
    w = GRID_W
    cw = wf_ref.shape[0]
    ys = [jnp.dot(mb_ref[...], z_ref[0, t], preferred_element_type=F32) for t in range(kb)]
    yr = jnp.concatenate([y[:w] for y in ys], axis=0).astype(BF16)
    yi = jnp.concatenate([y[w:] for y in ys], axis=0).astype(BF16)
    f = (jnp.dot(yr, cs_ref[:cw], preferred_element_type=F32)
         + jnp.dot(yi, cs_ref[cw:], preferred_element_type=F32))
    out = jnp.dot(f.astype(BF16), wf_ref[...], preferred_element_type=F32)
    for t in range(kb):
        o_ref[0, t] = out[t * w:(t + 1) * w]


def _fourier_latent(uf, wf):
    b, n, cw = uf.shape
    rows = n // GRID_W
    a_mat, tw_c, tw_s, m_b = _dft_consts(n)
    cc, sc = _channel_dft(n, cw)
    wide = GRID_W * cw
    tn = 2048
    expand = lambda t: jnp.broadcast_to(jnp.asarray(t, F32)[:, :, None], (rows, GRID_W, cw)).reshape(rows, wide)
    z = pl.pallas_call(
        _fourier_rows_kernel,
        grid=(wide // tn, b),
        in_specs=[pl.BlockSpec((1, rows, tn), lambda j, i: (i, 0, j)),
                  pl.BlockSpec((2 * rows, rows), lambda j, i: (0, 0)),
                  pl.BlockSpec((rows, tn), lambda j, i: (0, j)),
                  pl.BlockSpec((rows, tn), lambda j, i: (0, j))],
        out_specs=pl.BlockSpec((1, 2 * rows, tn), lambda j, i: (i, 0, j)),
        out_shape=jax.ShapeDtypeStruct((b, 2 * rows, wide), BF16),
        compiler_params=_cparams(("parallel", "parallel")),
        name="fourier_rows",
    )(uf.reshape(b, rows, wide), _table(a_mat), expand(tw_c), expand(tw_s))
    z = z.reshape(b, 2, rows, GRID_W, cw).transpose(0, 2, 1, 3, 4).reshape(b, rows, 2 * GRID_W, cw)
    kb = 16
    y = pl.pallas_call(
        _fourier_cols_kernel,
        grid=(b, rows // kb),
        in_specs=[pl.BlockSpec((1, kb, 2 * GRID_W, cw), lambda i, j: (i, j, 0, 0)),
                  pl.BlockSpec((2 * GRID_W, 2 * GRID_W), lambda i, j: (0, 0)),
                  pl.BlockSpec((2 * cw, cw), lambda i, j: (0, 0)),
                  pl.BlockSpec((cw, cw), lambda i, j: (0, 0))],
        out_specs=pl.BlockSpec((1, kb, GRID_W, cw), lambda i, j: (i, j, 0, 0)),
        out_shape=jax.ShapeDtypeStruct((b, rows, GRID_W, cw), F32),
        compiler_params=_cparams(("parallel", "parallel")),
        name="fourier_cols",
    )(z, _table(m_b), _table(np.concatenate([cc, sc], axis=0)), wf)
    return y.transpose(0, 2, 1, 3).reshape(b, n, cw)


def _fourier_dense_kernel(u_ref, cl_ref, sl_ref, cs_ref, wf_ref, o_ref):
    u = u_ref[0]
    cw = wf_ref.shape[0]
    a = jnp.dot(u, cs_ref[:cw], preferred_element_type=F32).astype(BF16)
    bb = jnp.dot(u, cs_ref[cw:], preferred_element_type=F32).astype(BF16)
    f = (jnp.dot(cl_ref[...], a, preferred_element_type=F32)
         - jnp.dot(sl_ref[...], bb, preferred_element_type=F32))
    o_ref[0] = jnp.dot(f.astype(BF16), wf_ref[...], preferred_element_type=F32)


def _fourier_dense(uf, wf):
    b, n, cw = uf.shape
    t = np.arange(n)
    ang = 2 * np.pi * np.outer(t, t) / n
    cc, sc = _channel_dft(n, cw)
    full = lambda shape: pl.BlockSpec(shape, lambda i: (0,) * len(shape))
    return pl.pallas_call(
        _fourier_dense_kernel,
        grid=(b,),
        in_specs=[pl.BlockSpec((1, n, cw), lambda i: (i, 0, 0)), full((n, n)), full((n, n)),
                  full((2 * cw, cw)), full((cw, cw))],
        out_specs=pl.BlockSpec((1, n, cw), lambda i: (i, 0, 0)),
        out_shape=jax.ShapeDtypeStruct((b, n, cw), F32),
        compiler_params=_cparams(("parallel",)),
        name="fourier_dense",
    )(uf, _table(np.cos(ang)), _table(np.sin(ang)), _table(np.concatenate([cc, sc], axis=0)), wf)


def _pool_kernel(x_ref, prev_ref, next_ref, wp_ref, ps_ref, o_ref, pad_ref, *, n_tokens):
    i = pl.program_id(1)
    tm = x_ref.shape[1]
    h = POOL_HALO
    pad_ref[0:h] = jnp.where(i > 0, prev_ref[0], 0.0)
    pad_ref[h:h + tm] = x_ref[0]
    pad_ref[h + tm:2 * h + tm] = jnp.where(i < pl.num_programs(1) - 1, next_ref[0], 0.0)
    ld = lambda off: pad_ref[h + off:h + off + tm, :]
    x0 = ld(0)
    sums = []
    acc = x0
    lo, hi = 0, 0
    for w in POOL_WINDOWS:
        for off in list(range(-(w // 2), lo)) + list(range(hi + 1, w // 2)):
            acc = acc + ld(off)
        lo, hi = -(w // 2), w // 2 - 1
        sums.append(acc)
    t = i * tm + lax.broadcasted_iota(jnp.int32, (tm, 1), 0)
    grp = lax.broadcasted_iota(jnp.int32, (1, x0.shape[1]), 1) // (x0.shape[1] // len(POOL_WINDOWS))
    pooled = jnp.zeros_like(x0)
    for gi, w in enumerate(POOL_WINDOWS):
        cnt = (jnp.minimum(t + w // 2, n_tokens) - jnp.maximum(t - w // 2, 0)).astype(F32)
        pooled = jnp.where(grp == gi, sums[gi] / cnt, pooled)
    pooled = pooled - x0
    o_ref[0] = jnp.dot(pooled.astype(BF16), wp_ref[...], preferred_element_type=F32) * ps_ref[...]


def _pool(up, wp_bd, ps, tm):
    b, n, cw = up.shape
    h = POOL_HALO
    nblk = n // h
    return pl.pallas_call(
        functools.partial(_pool_kernel, n_tokens=n),
        grid=(b, n // tm),
        in_specs=[pl.BlockSpec((1, tm, cw), lambda i, j: (i, j, 0)),
                  pl.BlockSpec((1, h, cw), lambda i, j: (i, jnp.maximum(j * (tm // h) - 1, 0), 0)),
                  pl.BlockSpec((1, h, cw), lambda i, j: (i, jnp.minimum((j + 1) * (tm // h), nblk - 1), 0)),
                  pl.BlockSpec((cw, cw), lambda i, j: (0, 0)),
                  pl.BlockSpec((1, cw), lambda i, j: (0, 0))],
        out_specs=pl.BlockSpec((1, tm, cw), lambda i, j: (i, j, 0)),
        out_shape=jax.ShapeDtypeStruct((b, n, cw), F32),
        scratch_shapes=[pltpu.VMEM((tm + 2 * h, cw), F32)],
        compiler_params=_cparams(("parallel", "parallel")),
        name="pool",
    )(up, up, up, wp_bd, ps)


def _na_bias_tables(rpb):
    heads = rpb.shape[0]
    qr = np.arange(NA_QROWS)[:, None]
    kr = np.arange(NA_KROWS)[None, :]
    qc = np.arange(GRID_W)[:, None]
    kc = np.arange(GRID_W)[None, :]
    c0 = np.clip(qc - NA_WIN_COLS // 2, 0, GRID_W - NA_WIN_COLS)
    col_ok = (kc >= c0) & (kc < c0 + NA_WIN_COLS)
    co = np.clip(kc - qc + NA_WIN_COLS - 1, 0, 2 * NA_WIN_COLS - 2)
    pick_c = (co.reshape(1, -1) == np.arange(2 * NA_WIN_COLS - 1)[:, None]).astype(np.float32)
    by_col = jnp.einsum("hrc,cx->hrx", rpb.astype(F32) * LOG2E, jnp.asarray(pick_c),
                        precision=lax.Precision.HIGHEST)
    tabs = []
    for case in range(3):
        krel = kr - 4 * case
        r0 = qr - NA_WIN_ROWS // 2
        r0 = np.maximum(r0, 0) if case == 0 else (np.minimum(r0, 0) if case == 2 else r0)
        row_ok = (krel >= r0) & (krel < r0 + NA_WIN_ROWS)
        ro = np.clip(krel - qr + NA_WIN_ROWS - 1, 0, 2 * NA_WIN_ROWS - 2)
        pick_r = (ro.reshape(-1, 1) == np.arange(2 * NA_WIN_ROWS - 1)[None, :]).astype(np.float32)
        t = jnp.einsum("pr,hrx->hpx", jnp.asarray(pick_r), by_col, precision=lax.Precision.HIGHEST)
        t = t.reshape(heads, NA_QROWS, NA_KROWS, GRID_W, GRID_W).transpose(0, 1, 3, 2, 4)
        ok = jnp.logical_and(jnp.asarray(row_ok)[None, :, None, :, None],
                             jnp.asarray(col_ok)[None, None, :, None, :])
        tabs.append(jnp.where(ok, t, NEG_INF).reshape(heads, NA_QROWS * GRID_W, NA_KROWS * GRID_W))
    return jnp.stack(tabs)


def _na_kernel(q_ref, k_ref, v_ref, kc_ref, vc_ref, bias_ref, o_ref, *, rows, bounded):
    rb = pl.program_id(2)
    kr0 = jnp.clip(rb * NA_QROWS - NA_WIN_ROWS // 2, 0, rows - NA_KROWS)
    start = pl.multiple_of(kr0 * GRID_W, GRID_W)
    nkeys = NA_KROWS * GRID_W
    accs = []
    for hh in range(2):
        q = q_ref[0, hh]
        kb = k_ref[0, hh, pl.ds(start, nkeys), :]
        vb = v_ref[0, hh, pl.ds(start, nkeys), :]
        s = lax.dot_general(q, kb, _DN_T, preferred_element_type=F32) + bias_ref[0, hh]
        sc = lax.dot_general(q, kc_ref[0, hh], _DN_T, preferred_element_type=F32)
        if not bounded:
            m = jnp.maximum(jnp.max(s, -1, keepdims=True), jnp.max(sc, -1, keepdims=True))
            s, sc = s - m, sc - m
        accs.append(jnp.dot(jnp.exp2(s).astype(BF16), vb, preferred_element_type=F32)
                    + jnp.dot(jnp.exp2(sc).astype(BF16), vc_ref[0, hh], preferred_element_type=F32))
    o_ref[0] = _pair_output(accs)


def _na_latent(nq, nk, nv, nkc, nvc, bias, bounded):
    b, heads, n, _ = nq.shape
    rows = n // GRID_W
    assert rows >= NA_KROWS and rows % NA_QROWS == 0
    tq = NA_QROWS * GRID_W
    nctx = nkc.shape[2]

    def bias_idx(i, p, j):
        kr0 = jnp.clip(j * NA_QROWS - NA_WIN_ROWS // 2, 0, rows - NA_KROWS)
        return ((j * NA_QROWS - kr0) // 4, p, 0, 0)

    res = pl.BlockSpec((1, 2, n, LANES), lambda i, p, j: (i, p, 0, 0))
    ctx = pl.BlockSpec((1, 2, nctx, LANES), lambda i, p, j: (i, p, 0, 0))

    def call(flag):
        return pl.pallas_call(
            functools.partial(_na_kernel, rows=rows, bounded=flag),
            grid=(b, heads // 2, rows // NA_QROWS),
            in_specs=[pl.BlockSpec((1, 2, tq, LANES), lambda i, p, j: (i, p, j, 0)), res, res, ctx, ctx,
                      pl.BlockSpec((1, 2, tq, NA_KROWS * GRID_W), bias_idx)],
            out_specs=pl.BlockSpec((1, tq, LANES), lambda i, p, j: (i, j, p)),
            out_shape=jax.ShapeDtypeStruct((b, n, heads * 64), F32),
            compiler_params=_cparams(("parallel", "parallel", "parallel"), VMEM_LIMIT),
            name="na_latent" if flag else "na_latent_rowmax",
        )

    return lax.cond(bounded, call(True), call(False), nq, nk, nv, nkc, nvc, bias)


def _outproj_kernel(x_ref, oa_ref, of_ref, op_ref, on_ref, w_ref, ga_ref, g_ref, sh_ref, sc_ref, *rest,
                    route):
    if route:
        wrh_ref, wrl_ref, x1_ref, h2_ref, rt_ref = rest
    else:
        x1_ref, h2_ref = rest
    gw = oa_ref.shape[-1]
    o = None
    for gi, r in enumerate((oa_ref, of_ref, op_ref, on_ref)):
        t = jnp.dot(r[0].astype(BF16), w_ref[gi * gw:(gi + 1) * gw, :], preferred_element_type=F32)
        o = t if o is None else o + t
    x1 = x_ref[0] + ga_ref[0] * o
    x1_ref[0] = x1
    h = (x1 * _rms(x1, x1.shape[-1])) * g_ref[...]
    h = h * (1.0 + sc_ref[0]) + sh_ref[0]
    h2_ref[0] = h.astype(h2_ref.dtype)
    if route:
        n_exp = 8
        lane = lax.broadcasted_iota(jnp.int32, (1, LANES), 1)
        hh = h.astype(BF16)
        hl = (h - hh.astype(F32)).astype(BF16)
        lg = (jnp.dot(hh, wrh_ref[...], preferred_element_type=F32)
              + (jnp.dot(hl, wrh_ref[...], preferred_element_type=F32)
                 + jnp.dot(hh, wrl_ref[...], preferred_element_type=F32)))
        lg = jnp.where(lane < n_exp, lg, NEG_INF)
        v1 = jnp.max(lg, -1, keepdims=True)
        i1 = jnp.min(jnp.where(lg == v1, lane, LANES), -1, keepdims=True)
        lg2 = jnp.where(lane == i1, NEG_INF, lg)
        v2 = jnp.max(lg2, -1, keepdims=True)
        i2 = jnp.min(jnp.where(lg2 == v2, lane, LANES), -1, keepdims=True)
        e2 = jnp.exp(v2 - v1)
        g1 = 1.0 / (1.0 + e2)
        g2 = e2 / (1.0 + e2)
        rt_ref[0] = jnp.where(lane == 0, i1.astype(F32),
                              jnp.where(lane == 1, i2.astype(F32),
                                        jnp.where(lane == 2, g1, jnp.where(lane == 3, g2, 0.0))))


def _outproj(x, oa, of, op, on, w_out, ga, g, sh, sc, tm, w_router=None):
    b, n, d = x.shape
    route = w_router is not None
    gw = oa.shape[-1]
    tok = lambda w: pl.BlockSpec((1, tm, w), lambda i, j: (i, j, 0))
    per_b = pl.BlockSpec((1, 1, d), lambda i, j: (i, 0, 0))
    full = lambda a: pl.BlockSpec(a.shape, lambda i, j: (0,) * a.ndim)
    ins = [x, oa, of, op, on, w_out, ga, g, sh, sc]
    in_specs = [tok(d), tok(gw), tok(gw), tok(gw), tok(gw), full(w_out), per_b, full(g), per_b, per_b]
    out_specs = [tok(d), tok(d)]
    out_shape = [jax.ShapeDtypeStruct((b, n, d), F32),
                 jax.ShapeDtypeStruct((b, n, d), F32 if route else BF16)]
    if route:
        wr = jnp.pad(w_router.astype(F32), ((0, 0), (0, LANES - w_router.shape[-1])))
        wr_hi = wr.astype(BF16)
        wr_lo = (wr - wr_hi.astype(F32)).astype(BF16)
        ins += [wr_hi, wr_lo]
        in_specs += [full(wr_hi), full(wr_lo)]
        out_specs.append(tok(LANES))
        out_shape.append(jax.ShapeDtypeStruct((b, n, LANES), F32))
    return pl.pallas_call(
        functools.partial(_outproj_kernel, route=route),
        grid=(b, n // tm),
        in_specs=in_specs, out_specs=out_specs, out_shape=out_shape,
        compiler_params=_cparams(("parallel", "parallel"), VMEM_LIMIT),
        name="outproj",
    )(*ins)


def _ffn_kernel(h_ref, x_ref, ga_ref, w1_ref, w3_ref, w2_ref, o_ref, acc_ref, *, fc):
    h = h_ref[0]
    ff = w1_ref.shape[1]
    for j in range(ff // fc):
        a = jnp.dot(h, w1_ref[:, j * fc:(j + 1) * fc], preferred_element_type=F32)
        bb = jnp.dot(h, w3_ref[:, j * fc:(j + 1) * fc], preferred_element_type=F32)
        t = jnp.dot((_silu(a) * bb).astype(BF16), w2_ref[j * fc:(j + 1) * fc, :], preferred_element_type=F32)
        if j == 0:
            acc_ref[...] = t
        else:
            acc_ref[...] += t
    o_ref[0] = x_ref[0] + ga_ref[0] * acc_ref[...]


def _ffn_dense(h2, x1, ga, w1, w3, w2, tm):
    b, n, d = x1.shape
    ff = w1.shape[1]
    fc = 256 if ff % 256 == 0 else ff
    tok = pl.BlockSpec((1, tm, d), lambda i, j: (i, j, 0))
    per_b = pl.BlockSpec((1, 1, d), lambda i, j: (i, 0, 0))
    full = lambda a: pl.BlockSpec(a.shape, lambda i, j: (0, 0), pipeline_mode=pl.Buffered(1))
    return pl.pallas_call(
        functools.partial(_ffn_kernel, fc=fc),
        grid=(b, n // tm),
        in_specs=[tok, tok, per_b, full(w1), full(w3), full(w2)],
        out_specs=tok,
        out_shape=jax.ShapeDtypeStruct((b, n, d), F32),
        scratch_shapes=[pltpu.VMEM((tm, d), F32)],
        compiler_params=_cparams(("parallel", "parallel"), VMEM_LIMIT),
        name="ffn_dense",
    )(h2, x1, ga, w1, w3, w2)


def _expert_kernel(te_ref, tv_ref, src_ref, h_hbm, w1_ref, w3_ref, w2_ref, y_ref, xs_ref, sems,
                   *, fc, n_tiles):
    i = pl.program_id(0)
    tme = y_ref.shape[0]
    nf = w1_ref.shape[2] // fc
    valid = tv_ref[i] > 0
    slot = i % 2

    def start_row(tile, r, sl):
        pltpu.make_async_copy(h_hbm.at[src_ref[tile * tme + r]], xs_ref.at[sl, r], sems.at[sl]).start()

    def wait_slot(sl):
        pltpu.make_async_copy(h_hbm.at[pl.ds(0, tme)], xs_ref.at[sl], sems.at[sl]).wait()

    @pl.when(i == 0)
    def _():
        def issue(r, c):
            start_row(0, r, 0)
            return c

        lax.fori_loop(0, tme, issue, 0, unroll=8)

    @pl.when(jnp.logical_or(i == 0, tv_ref[jnp.maximum(i - 1, 0)] > 0))
    def _():
        wait_slot(slot)

    @pl.when(valid)
    def _():
        nxt = jnp.minimum(i + 1, n_tiles - 1)
        xb = xs_ref[slot].astype(BF16)
        cuts = [(k * tme) // nf for k in range(nf + 1)]
        for j in range(nf):
            for r in range(cuts[j], cuts[j + 1]):
                start_row(nxt, r, 1 - slot)
            a = jnp.dot(xb, w1_ref[0, :, j * fc:(j + 1) * fc], preferred_element_type=F32)
            bb = jnp.dot(xb, w3_ref[0, :, j * fc:(j + 1) * fc], preferred_element_type=F32)
            t = jnp.dot((_silu(a) * bb).astype(BF16), w2_ref[0, j * fc:(j + 1) * fc, :],
                        preferred_element_type=F32)
            if j == 0:
                y_ref[...] = t
            else:
                y_ref[...] += t

    @pl.when(jnp.logical_not(valid))
    def _():
        y_ref[...] = jnp.zeros_like(y_ref)

    @pl.when(jnp.logical_and(i == n_tiles - 1, valid))
    def _():
        wait_slot(1 - slot)


def _combine_kernel(d0_ref, d1_ref, ys_hbm, x_ref, ga_ref, rt_ref, o_ref, buf_ref, sem):
    i = pl.program_id(0)
    j = pl.program_id(1)
    tc = x_ref.shape[1]
    base = (i * pl.num_programs(1) + j) * tc

    def issue(r, c):
        pltpu.make_async_copy(ys_hbm.at[d0_ref[base + r]], buf_ref.at[0, r], sem).start()
        pltpu.make_async_copy(ys_hbm.at[d1_ref[base + r]], buf_ref.at[1, r], sem).start()
        return c

    lax.fori_loop(0, tc, issue, 0, unroll=8)
    pltpu.make_async_copy(ys_hbm.at[pl.ds(0, tc)], buf_ref.at[0], sem).wait()
    pltpu.make_async_copy(ys_hbm.at[pl.ds(0, tc)], buf_ref.at[1], sem).wait()
    rt = rt_ref[0]
    y = rt[:, TOP_K:TOP_K + 1] * buf_ref[0] + rt[:, TOP_K + 1:TOP_K + 2] * buf_ref[1]
    o_ref[0] = x_ref[0] + ga_ref[0] * y


def _moe(h2, x1, ga, rt, w1, w3, w2, tme, tc):
    b, n_b, d = h2.shape
    n = b * n_b
    n_exp, _, ff = w1.shape
    fc = 512 if ff % 512 == 0 else ff
    e_idx = rt[..., 0:TOP_K].astype(jnp.int32).reshape(-1)
    onehot = (e_idx[:, None] == jnp.arange(n_exp)[None, :]).astype(jnp.int32)
    pos = jnp.take_along_axis(jnp.cumsum(onehot, axis=0) - onehot, e_idx[:, None], axis=1)[:, 0]
    counts = jnp.sum(onehot, axis=0)
    padded = ((counts + tme - 1) // tme) * tme
    ends = jnp.cumsum(padded)
    offs = ends - padded
    dest = offs[e_idx] + pos
    n_rows = n * TOP_K + n_exp * tme
    n_tiles = n_rows // tme
    src = jnp.zeros((n_rows,), jnp.int32).at[dest].set(jnp.arange(n * TOP_K, dtype=jnp.int32) // TOP_K)
    tile_start = jnp.arange(n_tiles, dtype=jnp.int32) * tme
    tvalid = (tile_start < ends[-1]).astype(jnp.int32)
    texp = jnp.minimum(jnp.sum((tile_start[:, None] >= ends[None, :]).astype(jnp.int32), axis=1), n_exp - 1)
    texp = jnp.where(tvalid > 0, texp, texp[jnp.maximum(ends[-1] // tme - 1, 0)])
    wspec = lambda shape: pl.BlockSpec((1,) + shape, lambda i, te, tv, s: (te[i], 0, 0),
                                       pipeline_mode=pl.Buffered(1))
    ys = pl.pallas_call(
        functools.partial(_expert_kernel, fc=fc, n_tiles=n_tiles),
        grid_spec=pltpu.PrefetchScalarGridSpec(
            num_scalar_prefetch=3,
            grid=(n_tiles,),
            in_specs=[pl.BlockSpec(memory_space=pl.ANY), wspec((d, ff)), wspec((d, ff)), wspec((ff, d))],
            out_specs=pl.BlockSpec((tme, d), lambda i, te, tv, s: (i, 0)),
            scratch_shapes=[pltpu.VMEM((2, tme, d), F32), pltpu.SemaphoreType.DMA((2,))]),
        out_shape=jax.ShapeDtypeStruct((n_rows, d), F32),
        compiler_params=_cparams(("arbitrary",), VMEM_LIMIT),
        name="experts",
    )(texp, tvalid, src, h2.reshape(n, d), w1, w3, w2)

    d2 = dest.reshape(n, TOP_K)
    tok = lambda w: pl.BlockSpec((1, tc, w), lambda i, j, a, b_: (i, j, 0))
    return pl.pallas_call(
        _combine_kernel,
        grid_spec=pltpu.PrefetchScalarGridSpec(
            num_scalar_prefetch=2,
            grid=(b, n_b // tc),
            in_specs=[pl.BlockSpec(memory_space=pl.ANY), tok(d),
                      pl.BlockSpec((1, 1, d), lambda i, j, a, b_: (i, 0, 0)), tok(LANES)],
            out_specs=tok(d),
            scratch_shapes=[pltpu.VMEM((2, tc, d), F32), pltpu.SemaphoreType.DMA(())]),
        out_shape=jax.ShapeDtypeStruct((b, n_b, d), F32),
        compiler_params=_cparams(("arbitrary", "arbitrary"), VMEM_LIMIT),
        name="moe_combine",
    )(d2[:, 0], d2[:, 1], ys, x1, ga, rt)


def _pick_tile(n, pref):
    t = pref
    while n % t:
        t //= 2
    return t


def kernel(x, c, ctx, c_ctx, w_ada, b_ada, g_mix, g_ffn, w_in, w_out, g_cq, g_ckv, w_uq, w_ukv,
           g_mla_qn, g_mla_qr, g_mla_kn, g_mla_kr, w_fourier, w_pool, pool_scale, g_na_q, g_na_k,
           na_rpb, w1_dense, w3_dense, w2_dense, w_router, w1_moe, w3_moe, w2_moe):
    b, n, d = x.shape
    nctx = ctx.shape[1]
    depth = w_ada.shape[0]
    assert b <= 7 and n % GRID_W == 0

    cc = jnp.zeros((8, d), F32).at[:b].set(c).at[b].set(c_ctx)
    mods = _adaln(cc, w_ada, b_ada)
    cos_x, sin_x = _rope_tables(n)
    cos_c = jnp.ones((nctx, LANES), F32)
    sin_c = jnp.zeros((nctx, LANES), F32)
    sw = _rope_swap_perm()
    tm_x = _pick_tile(n, 256)
    tm_c = _pick_tile(nctx, 256)
    amax = lambda g: jnp.max(jnp.abs(g.astype(F32)))

    for l in range(depth):
        last = l == depth - 1
        mx = mods[l, :b].reshape(b, 1, 6 * d)
        mc = jnp.broadcast_to(mods[l, b].reshape(1, 1, 6 * d), (b, 1, 6 * d))
        part = lambda m, k: m[:, :, k * d:(k + 1) * d]

        bound_mla = (MLA_SCALE * LOG2E) * (MLA_NOPE * amax(g_mla_qn[l]) * amax(g_mla_kn[l])
                                           + MLA_ROPE * amax(g_mla_qr[l]) * amax(g_mla_kr[l]))
        qk_na = (NA_SCALE * LOG2E) * NA_HEAD_DIM * amax(g_na_q[l]) * amax(g_na_k[l])
        rpb_hi = jnp.maximum(jnp.max(na_rpb[l]).astype(F32), 0.0) * LOG2E
        rpb_lo = jnp.minimum(jnp.min(na_rpb[l]).astype(F32), 0.0) * LOG2E
        bound_na = qk_na + rpb_hi
        mla_bounded = 2.0 * bound_mla <= MAX_SOFTMAX_GAP_LOG2
        na_bounded = 2.0 * qk_na + rpb_hi - rpb_lo <= MAX_SOFTMAX_GAP_LOG2
        bounds = jnp.zeros((1, LANES), F32).at[0, 0].set(bound_mla).at[0, 1].set(bound_na)

        wts = [
            _prep_in_weights(w_in[l]),
            g_cq[l].reshape(1, -1), _prep_uq(w_uq[l]),
            _lane_vec([(g_mla_qn[l], 64), (g_mla_qr[l], 32), (None, 32)]),
            _lane_vec([(None, 64), (g_mla_qr[l][sw], 32), (None, 32)]),
            g_ckv[l].reshape(1, -1), _prep_ukv(w_ukv[l]),
            _lane_vec([(g_mla_kn[l], 64), (None, 64)]),
            _lane_vec([(None, 64), (g_mla_kr[l], 32), (None, 32)]),
            _lane_vec([(None, 64), (g_mla_kr[l][sw], 32), (None, 32)]),
            jnp.tile(g_na_q[l], NA_HEADS).reshape(1, -1), jnp.tile(g_na_k[l], NA_HEADS).reshape(1, -1),
        ]
        gmix = g_mix[l].reshape(1, d)
        qx, kx, vx, fx, px, nqx, nkx, nvx = _inproj(x, part(mx, 0), part(mx, 1), gmix, cos_x, sin_x,
                                                    bounds, wts, tm_x)
        qc, kc, vc, fc_, pc, nqc, nkc, nvc = _inproj(ctx, part(mc, 0), part(mc, 1), gmix, cos_c, sin_c,
                                                     bounds, wts, tm_c)

        wf = w_fourier[l].astype(BF16)
        cg = w_pool.shape[-1]
        wp_bd = jnp.zeros((len(POOL_WINDOWS) * cg,) * 2, F32)
        for gi in range(len(POOL_WINDOWS)):
            wp_bd = wp_bd.at[gi * cg:(gi + 1) * cg, gi * cg:(gi + 1) * cg].set(w_pool[l, gi])
        wp_bd = wp_bd.astype(BF16)
        ps = pool_scale[l].reshape(1, -1)
        w_out_l = w_out[l].astype(BF16)
        gffn = g_ffn[l].reshape(1, d)

        o_mla = _flash_pairs(qx, [(kx, vx), (kc, vc)], _pick_tile(n, 256), mla_bounded)
        o_f = _fourier_latent(fx, wf)
        o_p = _pool(px, wp_bd, ps, _pick_tile(n, 512))
        o_na = _na_latent(nqx, nkx, nvx, nkc, nvc, _na_bias_tables(na_rpb[l]), na_bounded)

        moe_layer = l % 2 == 1
        i = l // 2
        res = _outproj(x, o_mla, o_f, o_p, o_na, w_out_l, part(mx, 2), gffn, part(mx, 3), part(mx, 4), tm_x,
                       w_router=w_router[i] if moe_layer else None)
        if moe_layer:
            x1, h2, rt = res
            w1, w3, w2 = w1_moe[i].astype(BF16), w3_moe[i].astype(BF16), w2_moe[i].astype(BF16)
            x = _moe(h2, x1, part(mx, 5), rt, w1, w3, w2, tme=512, tc=256)
        else:
            x1, h2 = res
            w1, w3, w2 = w1_dense[i].astype(BF16), w3_dense[i].astype(BF16), w2_dense[i].astype(BF16)
            x = _ffn_dense(h2, x1, part(mx, 5), w1, w3, w2, _pick_tile(n, 512))

        if not last:
            oc_mla = _flash_pairs(qc, [(kc, vc)], tm_c, mla_bounded)
            oc_f = _fourier_dense(fc_, wf)
            oc_p = _pool(pc, wp_bd, ps, tm_c)
            oc_na = _flash_pairs(nqc, [(nkc, nvc)], tm_c, na_bounded)
            resc = _outproj(ctx, oc_mla, oc_f, oc_p, oc_na, w_out_l, part(mc, 2), gffn, part(mc, 3),
                            part(mc, 4), tm_c, w_router=w_router[i] if moe_layer else None)
            if moe_layer:
                c1, hc2, rtc = resc
                ctx = _moe(hc2, c1, part(mc, 5), rtc, w1, w3, w2, tme=256, tc=256)
            else:
                c1, hc2 = resc
                ctx = _ffn_dense(hc2, c1, part(mc, 5), w1, w3, w2, tm_c)
    return x
```

```python
import functools
import math

import numpy as np
import jax
import jax.numpy as jnp
from jax import lax
from jax.experimental import pallas as pl
from jax.experimental.pallas import tpu as pltpu

F32 = jnp.float32
BF16 = jnp.bfloat16

GRID_W = 64
LANES = 128
EPS = 1e-6
NEG_INF = -1e30

MLA_HEADS = 4
MLA_NOPE = 64
MLA_ROPE = 32
MLA_V = 64
MLA_SCALE = (MLA_NOPE + MLA_ROPE) ** -0.5
MLA_SPARE_LANE = MLA_NOPE + MLA_ROPE
ROPE_BASE = 10000.0
LOG2E = 1.0 / math.log(2.0)
MAX_SOFTMAX_GAP_LOG2 = 120.0

FOURIER_GROUPS = 4
POOL_WINDOWS = (2, 4, 8, 16)
POOL_HALO = 8

NA_HEADS = 4
NA_HEAD_DIM = 64
NA_SCALE = NA_HEAD_DIM ** -0.5
NA_WIN_ROWS = 8
NA_WIN_COLS = 16
NA_QROWS = 8
NA_KROWS = 16

TOP_K = 2
VMEM_LIMIT = 56 * 1024 * 1024


def _cparams(sem, vmem=None):
    return pltpu.CompilerParams(dimension_semantics=sem, vmem_limit_bytes=vmem)


def _silu(a):
    return a / (1.0 + jnp.exp(-a))


def _rms(v, n):
    return lax.rsqrt(jnp.sum(v * v, axis=-1, keepdims=True) * (1.0 / n) + EPS)


def _adaln_kernel(c_ref, w_ref, b_ref, o_ref):
    o_ref[0] = jnp.dot(_silu(c_ref[...]), w_ref[0], preferred_element_type=F32) + b_ref[0]


def _adaln(cc, w_ada, b_ada):
    depth, d, d6 = w_ada.shape
    tn = 512
    return pl.pallas_call(
        _adaln_kernel,
        grid=(depth, d6 // tn),
        in_specs=[pl.BlockSpec((8, d), lambda l, j: (0, 0)),
                  pl.BlockSpec((1, d, tn), lambda l, j: (l, 0, j)),
                  pl.BlockSpec((1, 1, tn), lambda l, j: (l, 0, j))],
        out_specs=pl.BlockSpec((1, 8, tn), lambda l, j: (l, 0, j)),
        out_shape=jax.ShapeDtypeStruct((depth, 8, d6), F32),
        compiler_params=_cparams(("arbitrary", "arbitrary")),
        name="adaln",
    )(cc, w_ada, b_ada.reshape(depth, 1, d6))


IN_COLS = 1920


def _rope_swap_perm():
    j = np.arange(MLA_ROPE)
    return np.where((j % 16) < 8, j + 8, j - 8)


def _prep_in_weights(w_in_l):
    d = w_in_l.shape[0]
    s = [0, 256, 384, 416, 672, 928, 1184, 1440, 1696]
    cq, ckv, kr, f, p, nq, nk, nv = [w_in_l[:, s[i]:s[i + 1]] for i in range(8)]
    z = lambda n: jnp.zeros((d, n), w_in_l.dtype)
    krsw = kr[:, _rope_swap_perm()]
    return jnp.concatenate([cq, ckv, f, p, nq, nk, nv, z(64), kr, z(32), z(64), krsw, z(32)],
                           axis=1).astype(BF16)


def _prep_uq(w_uq_l):
    r = w_uq_l.shape[0]
    w = w_uq_l.reshape(r, MLA_HEADS, MLA_NOPE + MLA_ROPE)
    z = lambda n: jnp.zeros((r, MLA_HEADS, n), w.dtype)
    main = jnp.concatenate([w, z(32)], axis=-1)
    sw = jnp.concatenate([z(64), w[..., MLA_NOPE:][..., _rope_swap_perm()], z(32)], axis=-1)
    return jnp.concatenate([main.reshape(r, -1), sw.reshape(r, -1)], axis=1).astype(BF16)


def _prep_ukv(w_ukv_l):
    r = w_ukv_l.shape[0]
    w = w_ukv_l.reshape(r, MLA_HEADS, MLA_NOPE + MLA_V)
    z = jnp.zeros((r, 64), w.dtype)
    tiles = [jnp.concatenate([w[:, h, :MLA_NOPE], z], axis=1) for h in range(MLA_HEADS)]
    for h in range(MLA_HEADS):
        v = w[:, h, MLA_NOPE:]
        tiles.append(jnp.concatenate([v, z] if h % 2 == 0 else [z, v], axis=1))
    return jnp.concatenate(tiles, axis=1).astype(BF16)


def _lane_vec(parts):
    cols = [jnp.zeros((n,), F32) if a is None else a.astype(F32) for a, n in parts]
    return jnp.concatenate(cols).reshape(1, -1)


def _rope_tables(n_tokens):
    t = np.arange(n_tokens)
    pos = np.stack([t // GRID_W, t % GRID_W], axis=1).astype(np.float64)
    half = MLA_ROPE // 2
    inv = 1.0 / (ROPE_BASE ** (np.arange(0, half, 2, dtype=np.float64) / half))
    j = np.arange(MLA_ROPE)
    ang = pos[:, j // 16] * inv[j % 8]
    sign = np.where((j % 16) < 8, -1.0, 1.0)
    cos = np.ones((n_tokens, LANES), np.float32)
    sin = np.zeros((n_tokens, LANES), np.float32)
    cos[:, 64:96] = np.cos(ang)
    sin[:, 64:96] = np.sin(ang) * sign
    return jnp.asarray(cos), jnp.asarray(sin)


def _inproj_kernel(x_ref, sh_ref, sc_ref, g_ref, cos_ref, sin_ref, bound_ref, win_ref,
                   gcq_ref, wuq_ref, gq_ref, gqsw_ref,
                   gckv_ref, wukv_ref, gkn_ref, gkr_ref, gkrsw_ref, gnq_ref, gnk_ref,
                   q_ref, k_ref, v_ref, f_ref, p_ref, nq_ref, nk_ref, nv_ref):
    x = x_ref[0]
    d = x.shape[-1]
    h = (x * _rms(x, d)) * g_ref[...]
    h = h * (1.0 + sc_ref[0]) + sh_ref[0]
    u = jnp.dot(h.astype(BF16), win_ref[...], preferred_element_type=F32)
    cq, ckv = u[:, 0:256], u[:, 256:384]
    f_ref[0] = u[:, 384:640].astype(BF16)
    p_ref[0] = u[:, 640:896]
    nq, nk, nv = u[:, 896:1152], u[:, 1152:1408], u[:, 1408:1664]
    krb, krs = u[:, 1664:1792], u[:, 1792:1920]
    cos, sin = cos_ref[...], sin_ref[...]
    lane = lax.broadcasted_iota(jnp.int32, (1, LANES), 1)
    m_nope = lane < MLA_NOPE
    m_rope = jnp.logical_and(lane >= MLA_NOPE, lane < MLA_NOPE + MLA_ROPE)
    bound_mla, bound_na = bound_ref[:, 0:1], bound_ref[:, 1:2]

    cqn = (cq * _rms(cq, 256) * gcq_ref[...]).astype(BF16)
    qall = jnp.dot(cqn, wuq_ref[...], preferred_element_type=F32)
    for hd in range(MLA_HEADS):
        blk = qall[:, hd * LANES:(hd + 1) * LANES]
        sw = qall[:, (MLA_HEADS + hd) * LANES:(MLA_HEADS + hd + 1) * LANES]
        sq = blk * blk
        rn = lax.rsqrt(jnp.sum(jnp.where(m_nope, sq, 0.0), -1, keepdims=True) * (1.0 / MLA_NOPE) + EPS)
        rr = lax.rsqrt(jnp.sum(jnp.where(m_rope, sq, 0.0), -1, keepdims=True) * (1.0 / MLA_ROPE) + EPS)
        qh = blk * jnp.where(m_nope, rn, rr) * gq_ref[...]
        qs = sw * rr * gqsw_ref[...]
        q_ref[0, hd] = jnp.where(lane == MLA_SPARE_LANE, -bound_mla,
                                 (qh * cos + qs * sin) * (MLA_SCALE * LOG2E)).astype(BF16)

    ckvn = (ckv * _rms(ckv, 128) * gckv_ref[...]).astype(BF16)
    kvall = jnp.dot(ckvn, wukv_ref[...], preferred_element_type=F32)
    rkr = _rms(krb, MLA_ROPE)
    krot = (krb * rkr * gkr_ref[...]) * cos + (krs * rkr * gkrsw_ref[...]) * sin
    for hd in range(MLA_HEADS):
        blk = kvall[:, hd * LANES:(hd + 1) * LANES]
        k_ref[0, hd] = jnp.where(lane == MLA_SPARE_LANE, 1.0,
                                 blk * _rms(blk, MLA_NOPE) * gkn_ref[...] + krot).astype(BF16)
        vb = kvall[:, (MLA_HEADS + hd) * LANES:(MLA_HEADS + hd + 1) * LANES]
        one_lane = 64 if hd % 2 == 0 else 0
        v_ref[0, hd] = jnp.where(lane == one_lane, 1.0, vb).astype(BF16)

    lane2 = lax.broadcasted_iota(jnp.int32, (1, 2 * LANES), 1)

    def seg_rms(t):
        sq = t * t
        r = jnp.zeros_like(t)
        for s in range(NA_HEADS):
            m = (lane2 // NA_HEAD_DIM) == s
            ss = jnp.sum(jnp.where(m, sq, 0.0), -1, keepdims=True)
            r = jnp.where(m, lax.rsqrt(ss * (1.0 / NA_HEAD_DIM) + EPS), r)
        return r

    nqn = nq * seg_rms(nq) * gnq_ref[...] * (NA_SCALE * LOG2E)
    nkn = nk * seg_rms(nk) * gnk_ref[...]
    for hd in range(NA_HEADS):
        t0 = (hd // 2) * LANES
        own = (lane < 64) if hd % 2 == 0 else (lane >= 64)
        one_lane = 64 if hd % 2 == 0 else 0
        spare = lane == (LANES - 1 if hd % 2 == 0 else 0)
        nq_ref[0, hd] = jnp.where(own, nqn[:, t0:t0 + LANES], jnp.where(spare, -bound_na, 0.0)).astype(BF16)
        nk_ref[0, hd] = jnp.where(own, nkn[:, t0:t0 + LANES], jnp.where(spare, 1.0, 0.0)).astype(BF16)
        nv_ref[0, hd] = jnp.where(own, nv[:, t0:t0 + LANES],
                                  jnp.where(lane == one_lane, 1.0, 0.0)).astype(BF16)


def _inproj(x, sh, sc, g, cos, sin, bounds, wts, tm):
    b, n, d = x.shape
    heads = MLA_HEADS
    row = lambda a: pl.BlockSpec(a.shape, lambda i, j: (0,) * a.ndim)
    per_b = pl.BlockSpec((1, 1, d), lambda i, j: (i, 0, 0))
    tok = lambda w: pl.BlockSpec((1, tm, w), lambda i, j: (i, j, 0))
    hd_spec = pl.BlockSpec((1, heads, tm, LANES), lambda i, j: (i, 0, j, 0))
    tab = pl.BlockSpec((tm, LANES), lambda i, j: (j, 0))
    hshape = jax.ShapeDtypeStruct((b, heads, n, LANES), BF16)
    return pl.pallas_call(
        _inproj_kernel,
        grid=(b, n // tm),
        in_specs=[tok(d), per_b, per_b, row(g), tab, tab, row(bounds)] + [row(w) for w in wts],
        out_specs=[hd_spec, hd_spec, hd_spec, tok(256), tok(256), hd_spec, hd_spec, hd_spec],
        out_shape=[hshape, hshape, hshape,
                   jax.ShapeDtypeStruct((b, n, 256), BF16), jax.ShapeDtypeStruct((b, n, 256), F32),
                   hshape, hshape, hshape],
        compiler_params=_cparams(("parallel", "parallel"), VMEM_LIMIT),
        name="inproj",
    )(x, sh, sc, g, cos, sin, bounds, *wts)


_DN_T = (((1,), (1,)), ((), ()))


def _pair_output(accs):
    lane = lax.broadcasted_iota(jnp.int32, (1, LANES), 1)
    out = None
    for hh, acc in enumerate(accs):
        one_lane = 64 if hh == 0 else 0
        own = (lane < 64) if hh == 0 else (lane >= 64)
        o = jnp.where(own, acc / acc[:, one_lane:one_lane + 1], 0.0)
        out = o if out is None else out + o
    return out


def _flash_pair_kernel(q_ref, *refs, chunks, bounded):
    o_ref = refs[-1]
    tq = q_ref.shape[2]
    accs = []
    for hh in range(2):
        q = q_ref[0, hh]
        acc = jnp.zeros((tq, LANES), F32)
        m = jnp.full((tq, 1), NEG_INF, F32)
        for kset, start, size in chunks:
            ks = refs[2 * kset][0, hh, start:start + size, :]
            vs = refs[2 * kset + 1][0, hh, start:start + size, :]
            s = lax.dot_general(q, ks, _DN_T, preferred_element_type=F32)
            if bounded:
                acc = acc + jnp.dot(jnp.exp2(s).astype(BF16), vs, preferred_element_type=F32)
            else:
                m_new = jnp.maximum(m, jnp.max(s, axis=-1, keepdims=True))
                p = jnp.exp2(s - m_new).astype(BF16)
                acc = acc * jnp.exp2(m - m_new) + jnp.dot(p, vs, preferred_element_type=F32)
                m = m_new
        accs.append(acc)
    o_ref[0] = _pair_output(accs)


def _flash_pairs(q, kv_sets, tq, bounded, tk=2048):
    b, heads, lq, _ = q.shape
    chunks, ops, kv_specs = [], [], []
    for si, (k, v) in enumerate(kv_sets):
        lk = k.shape[2]
        step = tk if lk % tk == 0 else lk
        chunks += [(si, st, step) for st in range(0, lk, step)]
        ops += [k, v]
        kv_specs += [pl.BlockSpec((1, 2, lk, LANES), lambda i, p, j: (i, p, 0, 0))] * 2

    def call(flag):
        return pl.pallas_call(
            functools.partial(_flash_pair_kernel, chunks=tuple(chunks), bounded=flag),
            grid=(b, heads // 2, lq // tq),
            in_specs=[pl.BlockSpec((1, 2, tq, LANES), lambda i, p, j: (i, p, j, 0))] + kv_specs,
            out_specs=pl.BlockSpec((1, tq, LANES), lambda i, p, j: (i, j, p)),
            out_shape=jax.ShapeDtypeStruct((b, lq, heads * 64), F32),
            compiler_params=_cparams(("parallel", "parallel", "parallel"), VMEM_LIMIT),
            name="flash_pairs" if flag else "flash_pairs_online",
        )

    return lax.cond(bounded, call(True), call(False), q, *ops)


def _table(a):
    return jnp.asarray(a, F32).astype(BF16)


def _dft_consts(n_tokens):
    rows = n_tokens // GRID_W
    r = np.arange(rows)
    ang_r = 2 * np.pi * np.outer(r, r) / rows
    a_mat = np.concatenate([np.cos(ang_r), -np.sin(ang_r)], axis=0)
    c = np.arange(GRID_W)
    ang_t = 2 * np.pi * np.outer(r, c) / n_tokens
    ang_c = 2 * np.pi * np.outer(c, c) / GRID_W
    cc, sc = np.cos(ang_c), np.sin(ang_c)
    m_b = np.block([[cc, sc], [-sc, cc]])
    return a_mat, np.cos(ang_t), np.sin(ang_t), m_b


def _channel_dft(n_tokens, width):
    cg = width // FOURIER_GROUPS
    j = np.arange(cg)
    ang = 2 * np.pi * np.outer(j, j) / cg
    eye = np.eye(FOURIER_GROUPS)
    scale = 1.0 / math.sqrt(n_tokens * cg)
    return np.kron(eye, np.cos(ang)) * scale, np.kron(eye, np.sin(ang)) * scale


def _fourier_rows_kernel(u_ref, a_ref, tc_ref, ts_ref, o_ref):
    z = jnp.dot(a_ref[...], u_ref[0], preferred_element_type=F32)
    r = z.shape[0] // 2
    zr, zi = z[:r], z[r:]
    c, s = tc_ref[...], ts_ref[...]
    o_ref[0, :r] = (zr * c + zi * s).astype(BF16)
    o_ref[0, r:] = (zi * c - zr * s).astype(BF16)


def _fourier_cols_kernel(z_ref, mb_ref, cs_ref, wf_ref, o_ref):
    kb = z_ref.shape[1]
    w = GRID_W
    cw = wf_ref.shape[0]
    ys = [jnp.dot(mb_ref[...], z_ref[0, t], preferred_element_type=F32) for t in range(kb)]
    yr = jnp.concatenate([y[:w] for y in ys], axis=0).astype(BF16)
    yi = jnp.concatenate([y[w:] for y in ys], axis=0).astype(BF16)
    f = (jnp.dot(yr, cs_ref[:cw], preferred_element_type=F32)
         + jnp.dot(yi, cs_ref[cw:], preferred_element_type=F32))
    out = jnp.dot(f.astype(BF16), wf_ref[...], preferred_element_type=F32)
    for t in range(kb):
        o_ref[0, t] = out[t * w:(t + 1) * w]


def _fourier_latent(uf, wf):
    b, n, cw = uf.shape
    rows = n // GRID_W
    a_mat, tw_c, tw_s, m_b = _dft_consts(n)
    cc, sc = _channel_dft(n, cw)
    wide = GRID_W * cw
    tn = 2048
    expand = lambda t: jnp.broadcast_to(jnp.asarray(t, F32)[:, :, None], (rows, GRID_W, cw)).reshape(rows, wide)
    z = pl.pallas_call(
        _fourier_rows_kernel,
        grid=(wide // tn, b),
        in_specs=[pl.BlockSpec((1, rows, tn), lambda j, i: (i, 0, j)),
                  pl.BlockSpec((2 * rows, rows), lambda j, i: (0, 0)),
                  pl.BlockSpec((rows, tn), lambda j, i: (0, j)),
                  pl.BlockSpec((rows, tn), lambda j, i: (0, j))],
        out_specs=pl.BlockSpec((1, 2 * rows, tn), lambda j, i: (i, 0, j)),
        out_shape=jax.ShapeDtypeStruct((b, 2 * rows, wide), BF16),
        compiler_params=_cparams(("parallel", "parallel")),
        name="fourier_rows",
    )(uf.reshape(b, rows, wide), _table(a_mat), expand(tw_c), expand(tw_s))
    z = z.reshape(b, 2, rows, GRID_W, cw).transpose(0, 2, 1, 3, 4).reshape(b, rows, 2 * GRID_W, cw)
    kb = 16
    y = pl.pallas_call(
        _fourier_cols_kernel,
        grid=(b, rows // kb),
        in_specs=[pl.BlockSpec((1, kb, 2 * GRID_W, cw), lambda i, j: (i, j, 0, 0)),
                  pl.BlockSpec((2 * GRID_W, 2 * GRID_W), lambda i, j: (0, 0)),
                  pl.BlockSpec((2 * cw, cw), lambda i, j: (0, 0)),
                  pl.BlockSpec((cw, cw), lambda i, j: (0, 0))],
        out_specs=pl.BlockSpec((1, kb, GRID_W, cw), lambda i, j: (i, j, 0, 0)),
        out_shape=jax.ShapeDtypeStruct((b, rows, GRID_W, cw), F32),
        compiler_params=_cparams(("parallel", "parallel")),
        name="fourier_cols",
    )(z, _table(m_b), _table(np.concatenate([cc, sc], axis=0)), wf)
    return y.transpose(0, 2, 1, 3).reshape(b, n, cw)


def _fourier_dense_kernel(u_ref, cl_ref, sl_ref, cs_ref, wf_ref, o_ref):
    u = u_ref[0]
    cw = wf_ref.shape[0]
    a = jnp.dot(u, cs_ref[:cw], preferred_element_type=F32).astype(BF16)
    bb = jnp.dot(u, cs_ref[cw:], preferred_element_type=F32).astype(BF16)
    f = (jnp.dot(cl_ref[...], a, preferred_element_type=F32)
         - jnp.dot(sl_ref[...], bb, preferred_element_type=F32))
    o_ref[0] = jnp.dot(f.astype(BF16), wf_ref[...], preferred_element_type=F32)


def _fourier_dense(uf, wf):
    b, n, cw = uf.shape
    t = np.arange(n)
    ang = 2 * np.pi * np.outer(t, t) / n
    cc, sc = _channel_dft(n, cw)
    full = lambda shape: pl.BlockSpec(shape, lambda i: (0,) * len(shape))
    return pl.pallas_call(
        _fourier_dense_kernel,
        grid=(b,),
        in_specs=[pl.BlockSpec((1, n, cw), lambda i: (i, 0, 0)), full((n, n)), full((n, n)),
                  full((2 * cw, cw)), full((cw, cw))],
        out_specs=pl.BlockSpec((1, n, cw), lambda i: (i, 0, 0)),
        out_shape=jax.ShapeDtypeStruct((b, n, cw), F32),
        compiler_params=_cparams(("parallel",)),
        name="fourier_dense",
    )(uf, _table(np.cos(ang)), _table(np.sin(ang)), _table(np.concatenate([cc, sc], axis=0)), wf)


def _pool_kernel(x_ref, prev_ref, next_ref, wp_ref, ps_ref, o_ref, pad_ref, *, n_tokens):
    i = pl.program_id(1)
    tm = x_ref.shape[1]
    h = POOL_HALO
    pad_ref[0:h] = jnp.where(i > 0, prev_ref[0], 0.0)
    pad_ref[h:h + tm] = x_ref[0]
    pad_ref[h + tm:2 * h + tm] = jnp.where(i < pl.num_programs(1) - 1, next_ref[0], 0.0)
    ld = lambda off: pad_ref[h + off:h + off + tm, :]
    x0 = ld(0)
    sums = []
    acc = x0
    lo, hi = 0, 0
    for w in POOL_WINDOWS:
        for off in list(range(-(w // 2), lo)) + list(range(hi + 1, w // 2)):
            acc = acc + ld(off)
        lo, hi = -(w // 2), w // 2 - 1
        sums.append(acc)
    t = i * tm + lax.broadcasted_iota(jnp.int32, (tm, 1), 0)
    grp = lax.broadcasted_iota(jnp.int32, (1, x0.shape[1]), 1) // (x0.shape[1] // len(POOL_WINDOWS))
    pooled = jnp.zeros_like(x0)
    for gi, w in enumerate(POOL_WINDOWS):
        cnt = (jnp.minimum(t + w // 2, n_tokens) - jnp.maximum(t - w // 2, 0)).astype(F32)
        pooled = jnp.where(grp == gi, sums[gi] / cnt, pooled)
    pooled = pooled - x0
    o_ref[0] = jnp.dot(pooled.astype(BF16), wp_ref[...], preferred_element_type=F32) * ps_ref[...]


def _pool(up, wp_bd, ps, tm):
    b, n, cw = up.shape
    h = POOL_HALO
    nblk = n // h
    return pl.pallas_call(
        functools.partial(_pool_kernel, n_tokens=n),
        grid=(b, n // tm),
        in_specs=[pl.BlockSpec((1, tm, cw), lambda i, j: (i, j, 0)),
                  pl.BlockSpec((1, h, cw), lambda i, j: (i, jnp.maximum(j * (tm // h) - 1, 0), 0)),
                  pl.BlockSpec((1, h, cw), lambda i, j: (i, jnp.minimum((j + 1) * (tm // h), nblk - 1), 0)),
                  pl.BlockSpec((cw, cw), lambda i, j: (0, 0)),
                  pl.BlockSpec((1, cw), lambda i, j: (0, 0))],
        out_specs=pl.BlockSpec((1, tm, cw), lambda i, j: (i, j, 0)),
        out_shape=jax.ShapeDtypeStruct((b, n, cw), F32),
        scratch_shapes=[pltpu.VMEM((tm + 2 * h, cw), F32)],
        compiler_params=_cparams(("parallel", "parallel")),
        name="pool",
    )(up, up, up, wp_bd, ps)


def _na_row_cases():
    qr = np.arange(NA_QROWS)[:, None]
    kr = np.arange(NA_KROWS)[None, :]
    masked = 2 * NA_WIN_ROWS - 1
    cases = []
    for case in range(3):
        krel = kr - 4 * case
        r0 = qr - NA_WIN_ROWS // 2
        r0 = np.maximum(r0, 0) if case == 0 else (np.minimum(r0, 0) if case == 2 else r0)
        row_ok = (krel >= r0) & (krel < r0 + NA_WIN_ROWS)
        cases.append(np.where(row_ok, krel - qr + NA_WIN_ROWS - 1, masked))
    return cases


def _na_bias_kernel(lo_ref, hi_ref, o_ref):
    for case, ro in enumerate(_na_row_cases()):
        for qr in range(NA_QROWS):
            for j in range(NA_KROWS // 2):
                o_ref[case, 0, qr * GRID_W:(qr + 1) * GRID_W, j * LANES:(j + 1) * LANES] = (
                    lo_ref[0, int(ro[qr, 2 * j])] + hi_ref[0, int(ro[qr, 2 * j + 1])])


def _na_bias_tables(rpb):
    heads = rpb.shape[0]
    qc = np.arange(GRID_W)[:, None]
    kc = np.arange(GRID_W)[None, :]
    c0 = np.clip(qc - NA_WIN_COLS // 2, 0, GRID_W - NA_WIN_COLS)
    col_ok = (kc >= c0) & (kc < c0 + NA_WIN_COLS)
    co = np.clip(kc - qc + NA_WIN_COLS - 1, 0, 2 * NA_WIN_COLS - 2)
    pick_c = (co.reshape(1, -1) == np.arange(2 * NA_WIN_COLS - 1)[:, None]).astype(np.float32)
    by_col = jnp.einsum("hrc,cx->hrx", rpb.astype(F32) * LOG2E, jnp.asarray(pick_c),
                        precision=lax.Precision.HIGHEST).reshape(heads, -1, GRID_W, GRID_W)
    blocks = jnp.where(jnp.asarray(col_ok)[None, None], by_col, NEG_INF)
    blocks = jnp.concatenate([blocks, jnp.full((heads, 1, GRID_W, GRID_W), NEG_INF, F32)], axis=1)
    zeros = jnp.zeros_like(blocks)
    lo = jnp.concatenate([blocks, zeros], axis=-1)
    hi = jnp.concatenate([zeros, blocks], axis=-1)
    n_off = blocks.shape[1]
    blk = pl.BlockSpec((1, n_off, GRID_W, LANES), lambda h: (h, 0, 0, 0))
    return pl.pallas_call(
        _na_bias_kernel,
        grid=(heads,),
        in_specs=[blk, blk],
        out_specs=pl.BlockSpec((3, 1, NA_QROWS * GRID_W, NA_KROWS * GRID_W), lambda h: (0, h, 0, 0)),
        out_shape=jax.ShapeDtypeStruct((3, heads, NA_QROWS * GRID_W, NA_KROWS * GRID_W), F32),
        compiler_params=_cparams(("parallel",), VMEM_LIMIT),
        name="na_bias",
    )(lo, hi)


def _na_kernel(q_ref, k_ref, v_ref, kc_ref, vc_ref, bias_ref, o_ref, *, rows, bounded):
    rb = pl.program_id(2)
    kr0 = jnp.clip(rb * NA_QROWS - NA_WIN_ROWS // 2, 0, rows - NA_KROWS)
    start = pl.multiple_of(kr0 * GRID_W, GRID_W)
    nkeys = NA_KROWS * GRID_W
    accs = []
    for hh in range(2):
        q = q_ref[0, hh]
        kb = k_ref[0, hh, pl.ds(start, nkeys), :]
        vb = v_ref[0, hh, pl.ds(start, nkeys), :]
        s = lax.dot_general(q, kb, _DN_T, preferred_element_type=F32) + bias_ref[0, hh]
        sc = lax.dot_general(q, kc_ref[0, hh], _DN_T, preferred_element_type=F32)
        if not bounded:
            m = jnp.maximum(jnp.max(s, -1, keepdims=True), jnp.max(sc, -1, keepdims=True))
            s, sc = s - m, sc - m
        accs.append(jnp.dot(jnp.exp2(s).astype(BF16), vb, preferred_element_type=F32)
                    + jnp.dot(jnp.exp2(sc).astype(BF16), vc_ref[0, hh], preferred_element_type=F32))
    o_ref[0] = _pair_output(accs)


def _na_latent(nq, nk, nv, nkc, nvc, bias, bounded):
    b, heads, n, _ = nq.shape
    rows = n // GRID_W
    assert rows >= NA_KROWS and rows % NA_QROWS == 0
    tq = NA_QROWS * GRID_W
    nctx = nkc.shape[2]

    def bias_idx(i, p, j):
        kr0 = jnp.clip(j * NA_QROWS - NA_WIN_ROWS // 2, 0, rows - NA_KROWS)
        return ((j * NA_QROWS - kr0) // 4, p, 0, 0)

    res = pl.BlockSpec((1, 2, n, LANES), lambda i, p, j: (i, p, 0, 0))
    ctx = pl.BlockSpec((1, 2, nctx, LANES), lambda i, p, j: (i, p, 0, 0))

    def call(flag):
        return pl.pallas_call(
            functools.partial(_na_kernel, rows=rows, bounded=flag),
            grid=(b, heads // 2, rows // NA_QROWS),
            in_specs=[pl.BlockSpec((1, 2, tq, LANES), lambda i, p, j: (i, p, j, 0)), res, res, ctx, ctx,
                      pl.BlockSpec((1, 2, tq, NA_KROWS * GRID_W), bias_idx)],
            out_specs=pl.BlockSpec((1, tq, LANES), lambda i, p, j: (i, j, p)),
            out_shape=jax.ShapeDtypeStruct((b, n, heads * 64), F32),
            compiler_params=_cparams(("parallel", "parallel", "parallel"), VMEM_LIMIT),
            name="na_latent" if flag else "na_latent_rowmax",
        )

    return lax.cond(bounded, call(True), call(False), nq, nk, nv, nkc, nvc, bias)


def _outproj_kernel(x_ref, oa_ref, of_ref, op_ref, on_ref, w_ref, ga_ref, g_ref, sh_ref, sc_ref, *rest,
                    route):
    if route:
        wrh_ref, wrl_ref, x1_ref, h2_ref, rt_ref = rest
    else:
        x1_ref, h2_ref = rest
    gw = oa_ref.shape[-1]
    o = None
    for gi, r in enumerate((oa_ref, of_ref, op_ref, on_ref)):
        t = jnp.dot(r[0].astype(BF16), w_ref[gi * gw:(gi + 1) * gw, :], preferred_element_type=F32)
        o = t if o is None else o + t
    x1 = x_ref[0] + ga_ref[0] * o
    x1_ref[0] = x1
    h = (x1 * _rms(x1, x1.shape[-1])) * g_ref[...]
    h = h * (1.0 + sc_ref[0]) + sh_ref[0]
    h2_ref[0] = h.astype(h2_ref.dtype)
    if route:
        n_exp = 8
        lane = lax.broadcasted_iota(jnp.int32, (1, LANES), 1)
        hh = h.astype(BF16)
        hl = (h - hh.astype(F32)).astype(BF16)
        lg = (jnp.dot(hh, wrh_ref[...], preferred_element_type=F32)
              + (jnp.dot(hl, wrh_ref[...], preferred_element_type=F32)
                 + jnp.dot(hh, wrl_ref[...], preferred_element_type=F32)))
        lg = jnp.where(lane < n_exp, lg, NEG_INF)
        v1 = jnp.max(lg, -1, keepdims=True)
        i1 = jnp.min(jnp.where(lg == v1, lane, LANES), -1, keepdims=True)
        lg2 = jnp.where(lane == i1, NEG_INF, lg)
        v2 = jnp.max(lg2, -1, keepdims=True)
        i2 = jnp.min(jnp.where(lg2 == v2, lane, LANES), -1, keepdims=True)
        e2 = jnp.exp(v2 - v1)
        g1 = 1.0 / (1.0 + e2)
        g2 = e2 / (1.0 + e2)
        rt_ref[0] = jnp.where(lane == 0, i1.astype(F32),
                              jnp.where(lane == 1, i2.astype(F32),
                                        jnp.where(lane == 2, g1, jnp.where(lane == 3, g2, 0.0))))


def _outproj(x, oa, of, op, on, w_out, ga, g, sh, sc, tm, w_router=None):
    b, n, d = x.shape
    route = w_router is not None
    gw = oa.shape[-1]
    tok = lambda w: pl.BlockSpec((1, tm, w), lambda i, j: (i, j, 0))
    per_b = pl.BlockSpec((1, 1, d), lambda i, j: (i, 0, 0))
    full = lambda a: pl.BlockSpec(a.shape, lambda i, j: (0,) * a.ndim)
    ins = [x, oa, of, op, on, w_out, ga, g, sh, sc]
    in_specs = [tok(d), tok(gw), tok(gw), tok(gw), tok(gw), full(w_out), per_b, full(g), per_b, per_b]
    out_specs = [tok(d), tok(d)]
    out_shape = [jax.ShapeDtypeStruct((b, n, d), F32),
                 jax.ShapeDtypeStruct((b, n, d), F32 if route else BF16)]
    if route:
        wr = jnp.pad(w_router.astype(F32), ((0, 0), (0, LANES - w_router.shape[-1])))
        wr_hi = wr.astype(BF16)
        wr_lo = (wr - wr_hi.astype(F32)).astype(BF16)
        ins += [wr_hi, wr_lo]
        in_specs += [full(wr_hi), full(wr_lo)]
        out_specs.append(tok(LANES))
        out_shape.append(jax.ShapeDtypeStruct((b, n, LANES), F32))
    return pl.pallas_call(
        functools.partial(_outproj_kernel, route=route),
        grid=(b, n // tm),
        in_specs=in_specs, out_specs=out_specs, out_shape=out_shape,
        compiler_params=_cparams(("parallel", "parallel"), VMEM_LIMIT),
        name="outproj",
    )(*ins)


def _ffn_kernel(h_ref, x_ref, ga_ref, w1_ref, w3_ref, w2_ref, o_ref, *, fc):
    h = h_ref[0]
    ff = w1_ref.shape[1]
    gated = []
    for j in range(ff // fc):
        a = jnp.dot(h, w1_ref[:, j * fc:(j + 1) * fc], preferred_element_type=F32)
        bb = jnp.dot(h, w3_ref[:, j * fc:(j + 1) * fc], preferred_element_type=F32)
        gated.append((_silu(a) * bb).astype(BF16))
    y = jnp.dot(jnp.concatenate(gated, axis=1), w2_ref[...], preferred_element_type=F32)
    o_ref[0] = x_ref[0] + ga_ref[0] * y


def _ffn_dense(h2, x1, ga, w1, w3, w2, tm):
    b, n, d = x1.shape
    ff = w1.shape[1]
    fc = 256 if ff % 256 == 0 else ff
    tok = pl.BlockSpec((1, tm, d), lambda i, j: (i, j, 0))
    per_b = pl.BlockSpec((1, 1, d), lambda i, j: (i, 0, 0))
    full = lambda a: pl.BlockSpec(a.shape, lambda i, j: (0, 0), pipeline_mode=pl.Buffered(1))
    return pl.pallas_call(
        functools.partial(_ffn_kernel, fc=fc),
        grid=(b, n // tm),
        in_specs=[tok, tok, per_b, full(w1), full(w3), full(w2)],
        out_specs=tok,
        out_shape=jax.ShapeDtypeStruct((b, n, d), F32),
        compiler_params=_cparams(("parallel", "parallel"), VMEM_LIMIT),
        name="ffn_dense",
    )(h2, x1, ga, w1, w3, w2)


def _expert_kernel(te_ref, tv_ref, rows_ref, h_hbm, w1_ref, w3_ref, w2_ref, y_hbm, xs_ref, ys_ref,
                   gsem, ssem, zsem, *, fc, n_tiles, n_slots, n_dump_tiles):
    i = pl.program_id(0)
    tme = xs_ref.shape[1]
    nf = w1_ref.shape[2] // fc
    valid = tv_ref[i] > 0
    prev_valid = jnp.logical_and(i > 0, tv_ref[jnp.maximum(i - 1, 0)] > 0)
    slot = i % 2
    prev = jnp.maximum(i - 1, 0)

    def token_of(v):
        if n_slots & (n_slots - 1) == 0:
            return jnp.bitwise_and(v, n_slots - 1)
        return lax.rem(v, n_slots)

    def gather_row(tile, r, sl):
        pltpu.make_async_copy(h_hbm.at[token_of(rows_ref[tile * tme + r])], xs_ref.at[sl, r],
                              gsem.at[sl]).start()

    def scatter_row(r, sl):
        pltpu.make_async_copy(ys_ref.at[sl, r], y_hbm.at[rows_ref[prev * tme + r]], ssem.at[sl]).start()

    def wait_gather(sl):
        pltpu.make_async_copy(h_hbm.at[pl.ds(0, tme)], xs_ref.at[sl], gsem.at[sl]).wait()

    def wait_scatter(sl):
        pltpu.make_async_copy(ys_ref.at[sl], y_hbm.at[pl.ds(0, tme)], ssem.at[sl]).wait()

    @pl.when(i == 0)
    def _():
        ys_ref[1] = jnp.zeros(ys_ref.shape[1:], F32)
        fills = [pltpu.make_async_copy(ys_ref.at[1], y_hbm.at[pl.ds(2 * n_slots + e * tme, tme)], zsem)
                 for e in range(n_dump_tiles)]
        for cp in fills:
            cp.start()
        for cp in fills:
            cp.wait()

        def issue(r, c):
            gather_row(0, r, 0)
            return c

        lax.fori_loop(0, tme, issue, 0, unroll=8)

    @pl.when(jnp.logical_or(i == 0, prev_valid))
    def _():
        wait_gather(slot)

    @pl.when(prev_valid)
    def _():
        wait_scatter(slot)

    @pl.when(valid)
    def _():
        nxt = jnp.minimum(i + 1, n_tiles - 1)
        xb = xs_ref[slot].astype(BF16)
        cuts = [(k * tme) // nf for k in range(nf + 1)]
        gated = []
        for j in range(nf):
            for r in range(cuts[j], cuts[j + 1]):
                gather_row(nxt, r, 1 - slot)
                scatter_row(r, 1 - slot)
            a = jnp.dot(xb, w1_ref[0, :, j * fc:(j + 1) * fc], preferred_element_type=F32)
            bb = jnp.dot(xb, w3_ref[0, :, j * fc:(j + 1) * fc], preferred_element_type=F32)
            gated.append((_silu(a) * bb).astype(BF16))
        ys_ref[slot] = jnp.dot(jnp.concatenate(gated, axis=1), w2_ref[0], preferred_element_type=F32)

    @pl.when(jnp.logical_and(jnp.logical_not(valid), prev_valid))
    def _():
        def issue(r, c):
            scatter_row(r, 1 - slot)
            return c

        lax.fori_loop(0, tme, issue, 0, unroll=8)
        wait_scatter(1 - slot)


def _combine_kernel(x_ref, y0_ref, y1_ref, ga_ref, rt_ref, o_ref):
    rt = rt_ref[0]
    y = rt[:, TOP_K:TOP_K + 1] * y0_ref[...] + rt[:, TOP_K + 1:TOP_K + 2] * y1_ref[...]
    o_ref[0] = x_ref[0] + ga_ref[0] * y


def _moe(h2, x1, ga, rt, w1, w3, w2, tme, tc):
    b, n_b, d = h2.shape
    n = b * n_b
    n_exp, _, ff = w1.shape
    fc = 512 if ff % 512 == 0 else ff
    e_idx = rt[..., 0:TOP_K].astype(jnp.int32).reshape(-1)
    onehot = (e_idx[:, None] == jnp.arange(n_exp)[None, :]).astype(jnp.int32)
    pos = jnp.take_along_axis(jnp.cumsum(onehot, axis=0) - onehot, e_idx[:, None], axis=1)[:, 0]
    counts = jnp.sum(onehot, axis=0)
    padded = ((counts + tme - 1) // tme) * tme
    ends = jnp.cumsum(padded)
    offs = ends - padded
    dest = offs[e_idx] + pos
    n_rows = n * TOP_K + n_exp * tme
    n_tiles = n_rows // tme
    tile_start = jnp.arange(n_tiles + 1, dtype=jnp.int32) * tme
    tvalid = (tile_start < ends[-1]).astype(jnp.int32)
    texp = jnp.minimum(jnp.sum((tile_start[:, None] >= ends[None, :]).astype(jnp.int32), axis=1), n_exp - 1)
    texp = jnp.where(tvalid > 0, texp, texp[jnp.maximum(ends[-1] // tme - 1, 0)])
    flat = jnp.arange(n * TOP_K, dtype=jnp.int32)
    row_e = jnp.repeat(texp[:n_tiles], tme)
    first_pad = jnp.sum(jnp.where(row_e[:, None] == jnp.arange(n_exp)[None, :], (offs + counts)[None, :], 0),
                        axis=1)
    pad_idx = jnp.clip(jnp.arange(n_rows, dtype=jnp.int32) - first_pad, 0, tme - 1)
    rows = (2 * n + row_e * tme + pad_idx).astype(jnp.int32).at[dest].set((flat % TOP_K) * n + flat // TOP_K)
    wspec = lambda shape: pl.BlockSpec((1,) + shape, lambda i, te, tv, s: (te[i], 0, 0),
                                       pipeline_mode=pl.Buffered(1))
    ys = pl.pallas_call(
        functools.partial(_expert_kernel, fc=fc, n_tiles=n_tiles, n_slots=n, n_dump_tiles=n_exp),
        grid_spec=pltpu.PrefetchScalarGridSpec(
            num_scalar_prefetch=3,
            grid=(n_tiles + 1,),
            in_specs=[pl.BlockSpec(memory_space=pl.ANY), wspec((d, ff)), wspec((d, ff)), wspec((ff, d))],
            out_specs=pl.BlockSpec(memory_space=pl.ANY),
            scratch_shapes=[pltpu.VMEM((2, tme, d), F32), pltpu.VMEM((2, tme, d), F32),
                            pltpu.SemaphoreType.DMA((2,)), pltpu.SemaphoreType.DMA((2,)),
                            pltpu.SemaphoreType.DMA(())]),
        out_shape=jax.ShapeDtypeStruct((2 * n + n_exp * tme, d), F32),
        compiler_params=_cparams(("arbitrary",), VMEM_LIMIT),
        name="experts",
    )(texp, tvalid, rows, h2.reshape(n, d), w1, w3, w2)

    nb = n_b // tc
    tok = lambda w: pl.BlockSpec((1, tc, w), lambda i, j: (i, j, 0))
    return pl.pallas_call(
        _combine_kernel,
        grid=(b, nb),
        in_specs=[tok(d),
                  pl.BlockSpec((tc, d), lambda i, j: (i * nb + j, 0)),
                  pl.BlockSpec((tc, d), lambda i, j: (n // tc + i * nb + j, 0)),
                  pl.BlockSpec((1, 1, d), lambda i, j: (i, 0, 0)), tok(LANES)],
        out_specs=tok(d),
        out_shape=jax.ShapeDtypeStruct((b, n_b, d), F32),
        compiler_params=_cparams(("parallel", "parallel"), VMEM_LIMIT),
        name="moe_combine",
    )(x1, ys, ys, ga, rt)


def _pick_tile(n, pref):
    t = pref
    while n % t:
        t //= 2
    return t


def kernel(x, c, ctx, c_ctx, w_ada, b_ada, g_mix, g_ffn, w_in, w_out, g_cq, g_ckv, w_uq, w_ukv,
           g_mla_qn, g_mla_qr, g_mla_kn, g_mla_kr, w_fourier, w_pool, pool_scale, g_na_q, g_na_k,
           na_rpb, w1_dense, w3_dense, w2_dense, w_router, w1_moe, w3_moe, w2_moe):
    b, n, d = x.shape
    nctx = ctx.shape[1]
    depth = w_ada.shape[0]
    assert b <= 7 and n % GRID_W == 0

    cc = jnp.zeros((8, d), F32).at[:b].set(c).at[b].set(c_ctx)
    mods = _adaln(cc, w_ada, b_ada)
    cos_x, sin_x = _rope_tables(n)
    cos_c = jnp.ones((nctx, LANES), F32)
    sin_c = jnp.zeros((nctx, LANES), F32)
    sw = _rope_swap_perm()
    tm_x = _pick_tile(n, 512)
    tm_c = _pick_tile(nctx, 256)
    amax = lambda g: jnp.max(jnp.abs(g.astype(F32)))

    for l in range(depth):
        last = l == depth - 1
        mx = mods[l, :b].reshape(b, 1, 6 * d)
        mc = jnp.broadcast_to(mods[l, b].reshape(1, 1, 6 * d), (b, 1, 6 * d))
        part = lambda m, k: m[:, :, k * d:(k + 1) * d]

        bound_mla = (MLA_SCALE * LOG2E) * (MLA_NOPE * amax(g_mla_qn[l]) * amax(g_mla_kn[l])
                                           + MLA_ROPE * amax(g_mla_qr[l]) * amax(g_mla_kr[l]))
        qk_na = (NA_SCALE * LOG2E) * NA_HEAD_DIM * amax(g_na_q[l]) * amax(g_na_k[l])
        rpb_hi = jnp.maximum(jnp.max(na_rpb[l]).astype(F32), 0.0) * LOG2E
        rpb_lo = jnp.minimum(jnp.min(na_rpb[l]).astype(F32), 0.0) * LOG2E
        bound_na = qk_na + rpb_hi
        mla_bounded = 2.0 * bound_mla <= MAX_SOFTMAX_GAP_LOG2
        na_bounded = 2.0 * qk_na + rpb_hi - rpb_lo <= MAX_SOFTMAX_GAP_LOG2
        bounds = jnp.zeros((1, LANES), F32).at[0, 0].set(bound_mla).at[0, 1].set(bound_na)

        wts = [
            _prep_in_weights(w_in[l]),
            g_cq[l].reshape(1, -1), _prep_uq(w_uq[l]),
            _lane_vec([(g_mla_qn[l], 64), (g_mla_qr[l], 32), (None, 32)]),
            _lane_vec([(None, 64), (g_mla_qr[l][sw], 32), (None, 32)]),
            g_ckv[l].reshape(1, -1), _prep_ukv(w_ukv[l]),
            _lane_vec([(g_mla_kn[l], 64), (None, 64)]),
            _lane_vec([(None, 64), (g_mla_kr[l], 32), (None, 32)]),
            _lane_vec([(None, 64), (g_mla_kr[l][sw], 32), (None, 32)]),
            jnp.tile(g_na_q[l], NA_HEADS).reshape(1, -1), jnp.tile(g_na_k[l], NA_HEADS).reshape(1, -1),
        ]
        gmix = g_mix[l].reshape(1, d)
        qx, kx, vx, fx, px, nqx, nkx, nvx = _inproj(x, part(mx, 0), part(mx, 1), gmix, cos_x, sin_x,
                                                    bounds, wts, tm_x)
        qc, kc, vc, fc_, pc, nqc, nkc, nvc = _inproj(ctx, part(mc, 0), part(mc, 1), gmix, cos_c, sin_c,
                                                     bounds, wts, tm_c)

        wf = w_fourier[l].astype(BF16)
        cg = w_pool.shape[-1]
        wp_bd = jnp.zeros((len(POOL_WINDOWS) * cg,) * 2, F32)
        for gi in range(len(POOL_WINDOWS)):
            wp_bd = wp_bd.at[gi * cg:(gi + 1) * cg, gi * cg:(gi + 1) * cg].set(w_pool[l, gi])
        wp_bd = wp_bd.astype(BF16)
        ps = pool_scale[l].reshape(1, -1)
        w_out_l = w_out[l].astype(BF16)
        gffn = g_ffn[l].reshape(1, d)

        o_mla = _flash_pairs(qx, [(kx, vx), (kc, vc)], _pick_tile(n, 256), mla_bounded)
        o_f = _fourier_latent(fx, wf)
        o_p = _pool(px, wp_bd, ps, _pick_tile(n, 512))
        o_na = _na_latent(nqx, nkx, nvx, nkc, nvc, _na_bias_tables(na_rpb[l]), na_bounded)

        moe_layer = l % 2 == 1
        i = l // 2
        res = _outproj(x, o_mla, o_f, o_p, o_na, w_out_l, part(mx, 2), gffn, part(mx, 3), part(mx, 4),
                       _pick_tile(n, 256) if moe_layer else tm_x,
                       w_router=w_router[i] if moe_layer else None)
        if moe_layer:
            x1, h2, rt = res
            w1, w3, w2 = w1_moe[i].astype(BF16), w3_moe[i].astype(BF16), w2_moe[i].astype(BF16)
            x = _moe(h2, x1, part(mx, 5), rt, w1, w3, w2, tme=512, tc=256)
        else:
            x1, h2 = res
            w1, w3, w2 = w1_dense[i].astype(BF16), w3_dense[i].astype(BF16), w2_dense[i].astype(BF16)
            x = _ffn_dense(h2, x1, part(mx, 5), w1, w3, w2, _pick_tile(n, 512))

        if not last:
            oc_mla = _flash_pairs(qc, [(kc, vc)], tm_c, mla_bounded)
            oc_f = _fourier_dense(fc_, wf)
            oc_p = _pool(pc, wp_bd, ps, tm_c)
            oc_na = _flash_pairs(nqc, [(nkc, nvc)], tm_c, na_bounded)
            resc = _outproj(ctx, oc_mla, oc_f, oc_p, oc_na, w_out_l, part(mc, 2), gffn, part(mc, 3),
                            part(mc, 4), tm_c, w_router=w_router[i] if moe_layer else None)
            if moe_layer:
                c1, hc2, rtc = resc
                ctx = _moe(hc2, c1, part(mc, 5), rtc, w1, w3, w2, tme=256, tc=256)
            else:
                c1, hc2 = resc
                ctx = _ffn_dense(hc2, c1, part(mc, 5), w1, w3, w2, tm_c)
    return x
```

```python
import functools
import math

import numpy as np
import jax
import jax.numpy as jnp
from jax import lax
from jax.experimental import pallas as pl
from jax.experimental.pallas import tpu as pltpu

F32 = jnp.float32
BF16 = jnp.bfloat16

GRID_W = 64
LANES = 128
EPS = 1e-6
NEG_INF = -1e30

MLA_HEADS = 4
MLA_NOPE = 64
MLA_ROPE = 32
MLA_V = 64
MLA_SCALE = (MLA_NOPE + MLA_ROPE) ** -0.5
MLA_SPARE_LANE = MLA_NOPE + MLA_ROPE
ROPE_BASE = 10000.0
LOG2E = 1.0 / math.log(2.0)
MAX_SOFTMAX_GAP_LOG2 = 120.0

FOURIER_GROUPS = 4
POOL_WINDOWS = (2, 4, 8, 16)
POOL_HALO = 8

NA_HEADS = 4
NA_HEAD_DIM = 64
NA_SCALE = NA_HEAD_DIM ** -0.5
NA_WIN_ROWS = 8
NA_WIN_COLS = 16
NA_QROWS = 8
NA_KROWS = 16
NA_BLOCKS_PER_STEP = 4

TOP_K = 2
VMEM_LIMIT = 56 * 1024 * 1024


def _cparams(sem, vmem=None):
    return pltpu.CompilerParams(dimension_semantics=sem, vmem_limit_bytes=vmem)


def _silu(a):
    return a / (1.0 + jnp.exp(-a))


def _rms(v, n):
    return lax.rsqrt(jnp.sum(v * v, axis=-1, keepdims=True) * (1.0 / n) + EPS)


def _adaln_kernel(c_ref, w_ref, b_ref, o_ref):
    o_ref[0] = jnp.dot(_silu(c_ref[...]), w_ref[0], preferred_element_type=F32) + b_ref[0]


def _adaln(cc, w_ada, b_ada):
    depth, d, d6 = w_ada.shape
    tn = 512
    return pl.pallas_call(
        _adaln_kernel,
        grid=(depth, d6 // tn),
        in_specs=[pl.BlockSpec((8, d), lambda l, j: (0, 0)),
                  pl.BlockSpec((1, d, tn), lambda l, j: (l, 0, j)),
                  pl.BlockSpec((1, 1, tn), lambda l, j: (l, 0, j))],
        out_specs=pl.BlockSpec((1, 8, tn), lambda l, j: (l, 0, j)),
        out_shape=jax.ShapeDtypeStruct((depth, 8, d6), F32),
        compiler_params=_cparams(("arbitrary", "arbitrary")),
        name="adaln",
    )(cc, w_ada, b_ada.reshape(depth, 1, d6))


IN_COLS = 1920


def _rope_swap_perm():
    j = np.arange(MLA_ROPE)
    return np.where((j % 16) < 8, j + 8, j - 8)


def _prep_in_weights(w_in_l):
    d = w_in_l.shape[0]
    s = [0, 256, 384, 416, 672, 928, 1184, 1440, 1696]
    cq, ckv, kr, f, p, nq, nk, nv = [w_in_l[:, s[i]:s[i + 1]] for i in range(8)]
    z = lambda n: jnp.zeros((d, n), w_in_l.dtype)
    krsw = kr[:, _rope_swap_perm()]
    return jnp.concatenate([cq, ckv, f, p, nq, nk, nv, z(64), kr, z(32), z(64), krsw, z(32)],
                           axis=1).astype(BF16)


def _prep_uq(w_uq_l):
    r = w_uq_l.shape[0]
    w = w_uq_l.reshape(r, MLA_HEADS, MLA_NOPE + MLA_ROPE)
    z = lambda n: jnp.zeros((r, MLA_HEADS, n), w.dtype)
    main = jnp.concatenate([w, z(32)], axis=-1)
    sw = jnp.concatenate([z(64), w[..., MLA_NOPE:][..., _rope_swap_perm()], z(32)], axis=-1)
    return jnp.concatenate([main.reshape(r, -1), sw.reshape(r, -1)], axis=1).astype(BF16)


def _prep_ukv(w_ukv_l):
    r = w_ukv_l.shape[0]
    w = w_ukv_l.reshape(r, MLA_HEADS, MLA_NOPE + MLA_V)
    z = jnp.zeros((r, 64), w.dtype)
    tiles = [jnp.concatenate([w[:, h, :MLA_NOPE], z], axis=1) for h in range(MLA_HEADS)]
    for h in range(MLA_HEADS):
        v = w[:, h, MLA_NOPE:]
        tiles.append(jnp.concatenate([v, z] if h % 2 == 0 else [z, v], axis=1))
    return jnp.concatenate(tiles, axis=1).astype(BF16)


def _lane_vec(parts):
    cols = [jnp.zeros((n,), F32) if a is None else a.astype(F32) for a, n in parts]
    return jnp.concatenate(cols).reshape(1, -1)


def _rope_tables(n_tokens):
    t = np.arange(n_tokens)
    pos = np.stack([t // GRID_W, t % GRID_W], axis=1).astype(np.float64)
    half = MLA_ROPE // 2
    inv = 1.0 / (ROPE_BASE ** (np.arange(0, half, 2, dtype=np.float64) / half))
    j = np.arange(MLA_ROPE)
    ang = pos[:, j // 16] * inv[j % 8]
    sign = np.where((j % 16) < 8, -1.0, 1.0)
    cos = np.ones((n_tokens, LANES), np.float32)
    sin = np.zeros((n_tokens, LANES), np.float32)
    cos[:, 64:96] = np.cos(ang)
    sin[:, 64:96] = np.sin(ang) * sign
    return jnp.asarray(cos), jnp.asarray(sin)


def _inproj_kernel(x_ref, sh_ref, sc_ref, g_ref, cos_ref, sin_ref, bound_ref, win_ref,
                   gcq_ref, wuq_ref, gq_ref, gqsw_ref,
                   gckv_ref, wukv_ref, gkn_ref, gkr_ref, gkrsw_ref, gnq_ref, gnk_ref,
                   q_ref, k_ref, v_ref, f_ref, p_ref, nq_ref, nk_ref, nv_ref):
    x = x_ref[0]
    d = x.shape[-1]
    h = (x * _rms(x, d)) * g_ref[...]
    h = h * (1.0 + sc_ref[0]) + sh_ref[0]
    u = jnp.dot(h.astype(BF16), win_ref[...], preferred_element_type=F32)
    cq, ckv = u[:, 0:256], u[:, 256:384]
    f_ref[0] = u[:, 384:640].astype(BF16)
    p_ref[0] = u[:, 640:896]
    nq, nk, nv = u[:, 896:1152], u[:, 1152:1408], u[:, 1408:1664]
    krb, krs = u[:, 1664:1792], u[:, 1792:1920]
    cos, sin = cos_ref[...], sin_ref[...]
    lane = lax.broadcasted_iota(jnp.int32, (1, LANES), 1)
    m_nope = lane < MLA_NOPE
    m_rope = jnp.logical_and(lane >= MLA_NOPE, lane < MLA_NOPE + MLA_ROPE)
    bound_mla, bound_na = bound_ref[:, 0:1], bound_ref[:, 1:2]

    cqn = (cq * _rms(cq, 256) * gcq_ref[...]).astype(BF16)
    qall = jnp.dot(cqn, wuq_ref[...], preferred_element_type=F32)
    for hd in range(MLA_HEADS):
        blk = qall[:, hd * LANES:(hd + 1) * LANES]
        sw = qall[:, (MLA_HEADS + hd) * LANES:(MLA_HEADS + hd + 1) * LANES]
        sq = blk * blk
        rn = lax.rsqrt(jnp.sum(jnp.where(m_nope, sq, 0.0), -1, keepdims=True) * (1.0 / MLA_NOPE) + EPS)
        rr = lax.rsqrt(jnp.sum(jnp.where(m_rope, sq, 0.0), -1, keepdims=True) * (1.0 / MLA_ROPE) + EPS)
        qh = blk * jnp.where(m_nope, rn, rr) * gq_ref[...]
        qs = sw * rr * gqsw_ref[...]
        q_ref[0, hd] = jnp.where(lane == MLA_SPARE_LANE, -bound_mla,
                                 (qh * cos + qs * sin) * (MLA_SCALE * LOG2E)).astype(BF16)

    ckvn = (ckv * _rms(ckv, 128) * gckv_ref[...]).astype(BF16)
    kvall = jnp.dot(ckvn, wukv_ref[...], preferred_element_type=F32)
    rkr = _rms(krb, MLA_ROPE)
    krot = (krb * rkr * gkr_ref[...]) * cos + (krs * rkr * gkrsw_ref[...]) * sin
    for hd in range(MLA_HEADS):
        blk = kvall[:, hd * LANES:(hd + 1) * LANES]
        k_ref[0, hd] = jnp.where(lane == MLA_SPARE_LANE, 1.0,
                                 blk * _rms(blk, MLA_NOPE) * gkn_ref[...] + krot).astype(BF16)
        vb = kvall[:, (MLA_HEADS + hd) * LANES:(MLA_HEADS + hd + 1) * LANES]
        one_lane = 64 if hd % 2 == 0 else 0
        v_ref[0, hd] = jnp.where(lane == one_lane, 1.0, vb).astype(BF16)

    lane2 = lax.broadcasted_iota(jnp.int32, (1, 2 * LANES), 1)

    def seg_rms(t):
        sq = t * t
        r = jnp.zeros_like(t)
        for s in range(NA_HEADS):
            m = (lane2 // NA_HEAD_DIM) == s
            ss = jnp.sum(jnp.where(m, sq, 0.0), -1, keepdims=True)
            r = jnp.where(m, lax.rsqrt(ss * (1.0 / NA_HEAD_DIM) + EPS), r)
        return r

    nqn = nq * seg_rms(nq) * gnq_ref[...] * (NA_SCALE * LOG2E)
    nkn = nk * seg_rms(nk) * gnk_ref[...]
    for hd in range(NA_HEADS):
        t0 = (hd // 2) * LANES
        own = (lane < 64) if hd % 2 == 0 else (lane >= 64)
        one_lane = 64 if hd % 2 == 0 else 0
        spare = lane == (LANES - 1 if hd % 2 == 0 else 0)
        nq_ref[0, hd] = jnp.where(own, nqn[:, t0:t0 + LANES], jnp.where(spare, -bound_na, 0.0)).astype(BF16)
        nk_ref[0, hd] = jnp.where(own, nkn[:, t0:t0 + LANES], jnp.where(spare, 1.0, 0.0)).astype(BF16)
        nv_ref[0, hd] = jnp.where(own, nv[:, t0:t0 + LANES],
                                  jnp.where(lane == one_lane, 1.0, 0.0)).astype(BF16)


def _inproj(x, sh, sc, g, cos, sin, bounds, wts, tm):
    b, n, d = x.shape
    heads = MLA_HEADS
    row = lambda a: pl.BlockSpec(a.shape, lambda i, j: (0,) * a.ndim)
    per_b = pl.BlockSpec((1, 1, d), lambda i, j: (i, 0, 0))
    tok = lambda w: pl.BlockSpec((1, tm, w), lambda i, j: (i, j, 0))
    hd_spec = pl.BlockSpec((1, heads, tm, LANES), lambda i, j: (i, 0, j, 0))
    tab = pl.BlockSpec((tm, LANES), lambda i, j: (j, 0))
    hshape = jax.ShapeDtypeStruct((b, heads, n, LANES), BF16)
    return pl.pallas_call(
        _inproj_kernel,
        grid=(b, n // tm),
        in_specs=[tok(d), per_b, per_b, row(g), tab, tab, row(bounds)] + [row(w) for w in wts],
        out_specs=[hd_spec, hd_spec, hd_spec, tok(256), tok(256), hd_spec, hd_spec, hd_spec],
        out_shape=[hshape, hshape, hshape,
                   jax.ShapeDtypeStruct((b, n, 256), BF16), jax.ShapeDtypeStruct((b, n, 256), F32),
                   hshape, hshape, hshape],
        compiler_params=_cparams(("parallel", "parallel"), VMEM_LIMIT),
        name="inproj",
    )(x, sh, sc, g, cos, sin, bounds, *wts)


_DN_T = (((1,), (1,)), ((), ()))


def _pair_output(accs):
    lane = lax.broadcasted_iota(jnp.int32, (1, LANES), 1)
    out = None
    for hh, acc in enumerate(accs):
        one_lane = 64 if hh == 0 else 0
        own = (lane < 64) if hh == 0 else (lane >= 64)
        o = jnp.where(own, acc / acc[:, one_lane:one_lane + 1], 0.0)
        out = o if out is None else out + o
    return out


def _flash_pair_kernel(q_ref, *refs, chunks, bounded):
    o_ref = refs[-1]
    tq = q_ref.shape[2]
    accs = []
    for hh in range(2):
        q = q_ref[0, hh]
        acc = jnp.zeros((tq, LANES), F32)
        m = jnp.full((tq, 1), NEG_INF, F32)
        for kset, start, size in chunks:
            ks = refs[2 * kset][0, hh, start:start + size, :]
            vs = refs[2 * kset + 1][0, hh, start:start + size, :]
            s = lax.dot_general(q, ks, _DN_T, preferred_element_type=F32)
            if bounded:
                acc = acc + jnp.dot(jnp.exp2(s).astype(BF16), vs, preferred_element_type=F32)
            else:
                m_new = jnp.maximum(m, jnp.max(s, axis=-1, keepdims=True))
                p = jnp.exp2(s - m_new).astype(BF16)
                acc = acc * jnp.exp2(m - m_new) + jnp.dot(p, vs, preferred_element_type=F32)
                m = m_new
        accs.append(acc)
    o_ref[0] = _pair_output(accs).astype(o_ref.dtype)


def _flash_pairs(q, kv_sets, tq, bounded, tk=2048):
    b, heads, lq, _ = q.shape
    chunks, ops, kv_specs = [], [], []
    for si, (k, v) in enumerate(kv_sets):
        lk = k.shape[2]
        step = tk if lk % tk == 0 else lk
        chunks += [(si, st, step) for st in range(0, lk, step)]
        ops += [k, v]
        kv_specs += [pl.BlockSpec((1, 2, lk, LANES), lambda i, p, j: (i, p, 0, 0))] * 2

    def call(flag):
        return pl.pallas_call(
            functools.partial(_flash_pair_kernel, chunks=tuple(chunks), bounded=flag),
            grid=(b, heads // 2, lq // tq),
            in_specs=[pl.BlockSpec((1, 2, tq, LANES), lambda i, p, j: (i, p, j, 0))] + kv_specs,
            out_specs=pl.BlockSpec((1, tq, LANES), lambda i, p, j: (i, j, p)),
            out_shape=jax.ShapeDtypeStruct((b, lq, heads * 64), BF16),
            compiler_params=_cparams(("parallel", "parallel", "parallel"), VMEM_LIMIT),
            name="flash_pairs" if flag else "flash_pairs_online",
        )

    return lax.cond(bounded, call(True), call(False), q, *ops)


def _table(a):
    return jnp.asarray(a, F32).astype(BF16)


def _dft_consts(n_tokens):
    rows = n_tokens // GRID_W
    r = np.arange(rows)
    ang_r = 2 * np.pi * np.outer(r, r) / rows
    a_mat = np.concatenate([np.cos(ang_r), -np.sin(ang_r)], axis=0)
    c = np.arange(GRID_W)
    ang_t = 2 * np.pi * np.outer(r, c) / n_tokens
    ang_c = 2 * np.pi * np.outer(c, c) / GRID_W
    cc, sc = np.cos(ang_c), np.sin(ang_c)
    m_b = np.block([[cc, sc], [-sc, cc]])
    return a_mat, np.cos(ang_t), np.sin(ang_t), m_b


def _channel_dft(n_tokens, width):
    cg = width // FOURIER_GROUPS
    j = np.arange(cg)
    ang = 2 * np.pi * np.outer(j, j) / cg
    eye = np.eye(FOURIER_GROUPS)
    scale = 1.0 / math.sqrt(n_tokens * cg)
    return np.kron(eye, np.cos(ang)) * scale, np.kron(eye, np.sin(ang)) * scale


def _fourier_rows_kernel(u_ref, a_ref, tc_ref, ts_ref, o_ref):
    z = jnp.dot(a_ref[...], u_ref[0], preferred_element_type=F32)
    r = z.shape[0] // 2
    zr, zi = z[:r], z[r:]
    c, s = tc_ref[...], ts_ref[...]
    o_ref[0, :r] = (zr * c + zi * s).astype(BF16)
    o_ref[0, r:] = (zi * c - zr * s).astype(BF16)


def _fourier_cols_kernel(z_ref, mb_ref, cs_ref, wf_ref, o_ref):
    kb = z_ref.shape[1]
    w = GRID_W
    cw = wf_ref.shape[0]
    ys = [jnp.dot(mb_ref[...], z_ref[0, t], preferred_element_type=F32) for t in range(kb)]
    yr = jnp.concatenate([y[:w] for y in ys], axis=0).astype(BF16)
    yi = jnp.concatenate([y[w:] for y in ys], axis=0).astype(BF16)
    f = (jnp.dot(yr, cs_ref[:cw], preferred_element_type=F32)
         + jnp.dot(yi, cs_ref[cw:], preferred_element_type=F32))
    out = jnp.dot(f.astype(BF16), wf_ref[...], preferred_element_type=F32)
    for t in range(kb):
        o_ref[0, t] = out[t * w:(t + 1) * w].astype(o_ref.dtype)


def _fourier_latent(uf, wf):
    b, n, cw = uf.shape
    rows = n // GRID_W
    a_mat, tw_c, tw_s, m_b = _dft_consts(n)
    cc, sc = _channel_dft(n, cw)
    wide = GRID_W * cw
    tn = 2048
    expand = lambda t: jnp.broadcast_to(jnp.asarray(t, F32)[:, :, None], (rows, GRID_W, cw)).reshape(rows, wide)
    z = pl.pallas_call(
        _fourier_rows_kernel,
        grid=(wide // tn, b),
        in_specs=[pl.BlockSpec((1, rows, tn), lambda j, i: (i, 0, j)),
                  pl.BlockSpec((2 * rows, rows), lambda j, i: (0, 0)),
                  pl.BlockSpec((rows, tn), lambda j, i: (0, j)),
                  pl.BlockSpec((rows, tn), lambda j, i: (0, j))],
        out_specs=pl.BlockSpec((1, 2 * rows, tn), lambda j, i: (i, 0, j)),
        out_shape=jax.ShapeDtypeStruct((b, 2 * rows, wide), BF16),
        compiler_params=_cparams(("parallel", "parallel")),
        name="fourier_rows",
    )(uf.reshape(b, rows, wide), _table(a_mat), expand(tw_c), expand(tw_s))
    z = z.reshape(b, 2, rows, GRID_W, cw).transpose(0, 2, 1, 3, 4).reshape(b, rows, 2 * GRID_W, cw)
    kb = 16
    y = pl.pallas_call(
        _fourier_cols_kernel,
        grid=(b, rows // kb),
        in_specs=[pl.BlockSpec((1, kb, 2 * GRID_W, cw), lambda i, j: (i, j, 0, 0)),
                  pl.BlockSpec((2 * GRID_W, 2 * GRID_W), lambda i, j: (0, 0)),
                  pl.BlockSpec((2 * cw, cw), lambda i, j: (0, 0)),
                  pl.BlockSpec((cw, cw), lambda i, j: (0, 0))],
        out_specs=pl.BlockSpec((1, kb, GRID_W, cw), lambda i, j: (i, j, 0, 0)),
        out_shape=jax.ShapeDtypeStruct((b, rows, GRID_W, cw), BF16),
        compiler_params=_cparams(("parallel", "parallel")),
        name="fourier_cols",
    )(z, _table(m_b), _table(np.concatenate([cc, sc], axis=0)), wf)
    return y.transpose(0, 2, 1, 3).reshape(b, n, cw)


def _fourier_dense_kernel(u_ref, cl_ref, sl_ref, cs_ref, wf_ref, o_ref):
    u = u_ref[0]
    cw = wf_ref.shape[0]
    a = jnp.dot(u, cs_ref[:cw], preferred_element_type=F32).astype(BF16)
    bb = jnp.dot(u, cs_ref[cw:], preferred_element_type=F32).astype(BF16)
    f = (jnp.dot(cl_ref[...], a, preferred_element_type=F32)
         - jnp.dot(sl_ref[...], bb, preferred_element_type=F32))
    o_ref[0] = jnp.dot(f.astype(BF16), wf_ref[...], preferred_element_type=F32).astype(o_ref.dtype)


def _fourier_dense(uf, wf):
    b, n, cw = uf.shape
    t = np.arange(n)
    ang = 2 * np.pi * np.outer(t, t) / n
    cc, sc = _channel_dft(n, cw)
    full = lambda shape: pl.BlockSpec(shape, lambda i: (0,) * len(shape))
    return pl.pallas_call(
        _fourier_dense_kernel,
        grid=(b,),
        in_specs=[pl.BlockSpec((1, n, cw), lambda i: (i, 0, 0)), full((n, n)), full((n, n)),
                  full((2 * cw, cw)), full((cw, cw))],
        out_specs=pl.BlockSpec((1, n, cw), lambda i: (i, 0, 0)),
        out_shape=jax.ShapeDtypeStruct((b, n, cw), BF16),
        compiler_params=_cparams(("parallel",)),
        name="fourier_dense",
    )(uf, _table(np.cos(ang)), _table(np.sin(ang)), _table(np.concatenate([cc, sc], axis=0)), wf)


def _pool_kernel(x_ref, prev_ref, next_ref, wp_ref, ps_ref, o_ref, pad_ref, *, n_tokens):
    i = pl.program_id(1)
    tm = x_ref.shape[1]
    h = POOL_HALO
    pad_ref[0:h] = jnp.where(i > 0, prev_ref[0], 0.0)
    pad_ref[h:h + tm] = x_ref[0]
    pad_ref[h + tm:2 * h + tm] = jnp.where(i < pl.num_programs(1) - 1, next_ref[0], 0.0)
    ld = lambda off: pad_ref[h + off:h + off + tm, :]
    x0 = ld(0)
    sums = []
    acc = x0
    lo, hi = 0, 0
    for w in POOL_WINDOWS:
        for off in list(range(-(w // 2), lo)) + list(range(hi + 1, w // 2)):
            acc = acc + ld(off)
        lo, hi = -(w // 2), w // 2 - 1
        sums.append(acc)
    t = i * tm + lax.broadcasted_iota(jnp.int32, (tm, 1), 0)
    grp = lax.broadcasted_iota(jnp.int32, (1, x0.shape[1]), 1) // (x0.shape[1] // len(POOL_WINDOWS))
    pooled = jnp.zeros_like(x0)
    for gi, w in enumerate(POOL_WINDOWS):
        cnt = (jnp.minimum(t + w // 2, n_tokens) - jnp.maximum(t - w // 2, 0)).astype(F32)
        pooled = jnp.where(grp == gi, sums[gi] / cnt, pooled)
    pooled = pooled - x0
    y = jnp.dot(pooled.astype(BF16), wp_ref[...], preferred_element_type=F32) * ps_ref[...]
    o_ref[0] = y.astype(o_ref.dtype)


def _pool(up, wp_bd, ps, tm):
    b, n, cw = up.shape
    h = POOL_HALO
    nblk = n // h
    return pl.pallas_call(
        functools.partial(_pool_kernel, n_tokens=n),
        grid=(b, n // tm),
        in_specs=[pl.BlockSpec((1, tm, cw), lambda i, j: (i, j, 0)),
                  pl.BlockSpec((1, h, cw), lambda i, j: (i, jnp.maximum(j * (tm // h) - 1, 0), 0)),
                  pl.BlockSpec((1, h, cw), lambda i, j: (i, jnp.minimum((j + 1) * (tm // h), nblk - 1), 0)),
                  pl.BlockSpec((cw, cw), lambda i, j: (0, 0)),
                  pl.BlockSpec((1, cw), lambda i, j: (0, 0))],
        out_specs=pl.BlockSpec((1, tm, cw), lambda i, j: (i, j, 0)),
        out_shape=jax.ShapeDtypeStruct((b, n, cw), BF16),
        scratch_shapes=[pltpu.VMEM((tm + 2 * h, cw), F32)],
        compiler_params=_cparams(("parallel", "parallel")),
        name="pool",
    )(up, up, up, wp_bd, ps)


def _na_row_cases():
    qr = np.arange(NA_QROWS)[:, None]
    kr = np.arange(NA_KROWS)[None, :]
    masked = 2 * NA_WIN_ROWS - 1
    cases = []
    for case in range(3):
        krel = kr - 4 * case
        r0 = qr - NA_WIN_ROWS // 2
        r0 = np.maximum(r0, 0) if case == 0 else (np.minimum(r0, 0) if case == 2 else r0)
        row_ok = (krel >= r0) & (krel < r0 + NA_WIN_ROWS)
        cases.append(np.where(row_ok, krel - qr + NA_WIN_ROWS - 1, masked))
    return cases


def _na_bias_kernel(lo_ref, hi_ref, o_ref):
    for case, ro in enumerate(_na_row_cases()):
        for qr in range(NA_QROWS):
            for j in range(NA_KROWS // 2):
                o_ref[case, 0, qr * GRID_W:(qr + 1) * GRID_W, j * LANES:(j + 1) * LANES] = (
                    lo_ref[0, int(ro[qr, 2 * j])] + hi_ref[0, int(ro[qr, 2 * j + 1])])


def _na_bias_tables(rpb):
    heads = rpb.shape[0]
    qc = np.arange(GRID_W)[:, None]
    kc = np.arange(GRID_W)[None, :]
    c0 = np.clip(qc - NA_WIN_COLS // 2, 0, GRID_W - NA_WIN_COLS)
    col_ok = (kc >= c0) & (kc < c0 + NA_WIN_COLS)
    co = np.clip(kc - qc + NA_WIN_COLS - 1, 0, 2 * NA_WIN_COLS - 2)
    pick_c = (co.reshape(1, -1) == np.arange(2 * NA_WIN_COLS - 1)[:, None]).astype(np.float32)
    by_col = jnp.einsum("hrc,cx->hrx", rpb.astype(F32) * LOG2E, jnp.asarray(pick_c),
                        precision=lax.Precision.HIGHEST).reshape(heads, -1, GRID_W, GRID_W)
    blocks = jnp.where(jnp.asarray(col_ok)[None, None], by_col, NEG_INF)
    blocks = jnp.concatenate([blocks, jnp.full((heads, 1, GRID_W, GRID_W), NEG_INF, F32)], axis=1)
    zeros = jnp.zeros_like(blocks)
    lo = jnp.concatenate([blocks, zeros], axis=-1)
    hi = jnp.concatenate([zeros, blocks], axis=-1)
    n_off = blocks.shape[1]
    blk = pl.BlockSpec((1, n_off, GRID_W, LANES), lambda h: (h, 0, 0, 0))
    return pl.pallas_call(
        _na_bias_kernel,
        grid=(heads,),
        in_specs=[blk, blk],
        out_specs=pl.BlockSpec((3, 1, NA_QROWS * GRID_W, NA_KROWS * GRID_W), lambda h: (0, h, 0, 0)),
        out_shape=jax.ShapeDtypeStruct((3, heads, NA_QROWS * GRID_W, NA_KROWS * GRID_W), F32),
        compiler_params=_cparams(("parallel",), VMEM_LIMIT),
        name="na_bias",
    )(lo, hi)


def _na_kernel(q_ref, k_ref, v_ref, kc_ref, vc_ref, bias_ref, o_ref, *, rows, blocks, bounded):
    nkeys = NA_KROWS * GRID_W
    tq = NA_QROWS * GRID_W
    for sub in range(blocks):
        rb = pl.program_id(2) * blocks + sub
        kr0 = jnp.clip(rb * NA_QROWS - NA_WIN_ROWS // 2, 0, rows - NA_KROWS)
        case = lax.shift_right_logical(rb * NA_QROWS - kr0, 2)
        start = pl.multiple_of(kr0 * GRID_W, GRID_W)
        accs = []
        for hh in range(2):
            q = q_ref[0, hh, sub * tq:(sub + 1) * tq, :]
            kb = k_ref[0, hh, pl.ds(start, nkeys), :]
            vb = v_ref[0, hh, pl.ds(start, nkeys), :]
            s = lax.dot_general(q, kb, _DN_T, preferred_element_type=F32) + bias_ref[case, hh]
            sc = lax.dot_general(q, kc_ref[0, hh], _DN_T, preferred_element_type=F32)
            if not bounded:
                m = jnp.maximum(jnp.max(s, -1, keepdims=True), jnp.max(sc, -1, keepdims=True))
                s, sc = s - m, sc - m
            accs.append(jnp.dot(jnp.exp2(s).astype(BF16), vb, preferred_element_type=F32)
                        + jnp.dot(jnp.exp2(sc).astype(BF16), vc_ref[0, hh], preferred_element_type=F32))
        o_ref[0, sub * tq:(sub + 1) * tq, :] = _pair_output(accs).astype(o_ref.dtype)


def _na_latent(nq, nk, nv, nkc, nvc, bias, bounded):
    b, heads, n, _ = nq.shape
    rows = n // GRID_W
    assert rows >= NA_KROWS and rows % NA_QROWS == 0
    blocks = NA_BLOCKS_PER_STEP
    while (rows // NA_QROWS) % blocks:
        blocks //= 2
    tq = NA_QROWS * GRID_W * blocks
    nctx = nkc.shape[2]
    res = pl.BlockSpec((1, 2, n, LANES), lambda i, p, j: (i, p, 0, 0))
    ctx = pl.BlockSpec((1, 2, nctx, LANES), lambda i, p, j: (i, p, 0, 0))
    bias_spec = pl.BlockSpec((3, 2) + bias.shape[2:], lambda i, p, j: (0, p, 0, 0),
                             pipeline_mode=pl.Buffered(1))

    def call(flag):
        return pl.pallas_call(
            functools.partial(_na_kernel, rows=rows, blocks=blocks, bounded=flag),
            grid=(b, heads // 2, n // tq),
            in_specs=[pl.BlockSpec((1, 2, tq, LANES), lambda i, p, j: (i, p, j, 0)), res, res, ctx, ctx,
                      bias_spec],
            out_specs=pl.BlockSpec((1, tq, LANES), lambda i, p, j: (i, j, p)),
            out_shape=jax.ShapeDtypeStruct((b, n, heads * 64), BF16),
            compiler_params=_cparams(("parallel", "parallel", "parallel"), VMEM_LIMIT),
            name="na_latent" if flag else "na_latent_rowmax",
        )

    return lax.cond(bounded, call(True), call(False), nq, nk, nv, nkc, nvc, bias)


def _outproj_kernel(x_ref, oa_ref, of_ref, op_ref, on_ref, w_ref, ga_ref, g_ref, sh_ref, sc_ref, *rest,
                    route):
    if route:
        wrh_ref, wrl_ref, x1_ref, h2_ref, rt_ref = rest
    else:
        x1_ref, h2_ref = rest
    gw = oa_ref.shape[-1]
    o = None
    for gi, r in enumerate((oa_ref, of_ref, op_ref, on_ref)):
        t = jnp.dot(r[0].astype(BF16), w_ref[gi * gw:(gi + 1) * gw, :], preferred_element_type=F32)
        o = t if o is None else o + t
    x1 = x_ref[0] + ga_ref[0] * o
    x1_ref[0] = x1
    h = (x1 * _rms(x1, x1.shape[-1])) * g_ref[...]
    h = h * (1.0 + sc_ref[0]) + sh_ref[0]
    h2_ref[0] = h.astype(h2_ref.dtype)
    if route:
        n_exp = 8
        lane = lax.broadcasted_iota(jnp.int32, (1, LANES), 1)
        hh = h.astype(BF16)
        hl = (h - hh.astype(F32)).astype(BF16)
        lg = (jnp.dot(hh, wrh_ref[...], preferred_element_type=F32)
              + (jnp.dot(hl, wrh_ref[...], preferred_element_type=F32)
                 + jnp.dot(hh, wrl_ref[...], preferred_element_type=F32)))
        lg = jnp.where(lane < n_exp, lg, NEG_INF)
        v1 = jnp.max(lg, -1, keepdims=True)
        i1 = jnp.min(jnp.where(lg == v1, lane, LANES), -1, keepdims=True)
        lg2 = jnp.where(lane == i1, NEG_INF, lg)
        v2 = jnp.max(lg2, -1, keepdims=True)
        i2 = jnp.min(jnp.where(lg2 == v2, lane, LANES), -1, keepdims=True)
        e2 = jnp.exp(v2 - v1)
        g1 = 1.0 / (1.0 + e2)
        g2 = e2 / (1.0 + e2)
        rt_ref[0] = jnp.where(lane == 0, i1.astype(F32),
                              jnp.where(lane == 1, i2.astype(F32),
                                        jnp.where(lane == 2, g1, jnp.where(lane == 3, g2, 0.0))))


def _outproj(x, oa, of, op, on, w_out, ga, g, sh, sc, tm, w_router=None):
    b, n, d = x.shape
    route = w_router is not None
    gw = oa.shape[-1]
    tok = lambda w: pl.BlockSpec((1, tm, w), lambda i, j: (i, j, 0))
    per_b = pl.BlockSpec((1, 1, d), lambda i, j: (i, 0, 0))
    full = lambda a: pl.BlockSpec(a.shape, lambda i, j: (0,) * a.ndim)
    ins = [x, oa, of, op, on, w_out, ga, g, sh, sc]
    in_specs = [tok(d), tok(gw), tok(gw), tok(gw), tok(gw), full(w_out), per_b, full(g), per_b, per_b]
    out_specs = [tok(d), tok(d)]
    out_shape = [jax.ShapeDtypeStruct((b, n, d), F32),
                 jax.ShapeDtypeStruct((b, n, d), F32 if route else BF16)]
    if route:
        wr = jnp.pad(w_router.astype(F32), ((0, 0), (0, LANES - w_router.shape[-1])))
        wr_hi = wr.astype(BF16)
        wr_lo = (wr - wr_hi.astype(F32)).astype(BF16)
        ins += [wr_hi, wr_lo]
        in_specs += [full(wr_hi), full(wr_lo)]
        out_specs.append(tok(LANES))
        out_shape.append(jax.ShapeDtypeStruct((b, n, LANES), F32))
    return pl.pallas_call(
        functools.partial(_outproj_kernel, route=route),
        grid=(b, n // tm),
        in_specs=in_specs, out_specs=out_specs, out_shape=out_shape,
        compiler_params=_cparams(("parallel", "parallel"), VMEM_LIMIT),
        name="outproj",
    )(*ins)


def _ffn_kernel(h_ref, x_ref, ga_ref, w1_ref, w3_ref, w2_ref, o_ref, *, fc):
    h = h_ref[0]
    ff = w1_ref.shape[1]
    gated = []
    for j in range(ff // fc):
        a = jnp.dot(h, w1_ref[:, j * fc:(j + 1) * fc], preferred_element_type=F32)
        bb = jnp.dot(h, w3_ref[:, j * fc:(j + 1) * fc], preferred_element_type=F32)
        gated.append((_silu(a) * bb).astype(BF16))
    y = jnp.dot(jnp.concatenate(gated, axis=1), w2_ref[...], preferred_element_type=F32)
    o_ref[0] = x_ref[0] + ga_ref[0] * y


def _ffn_dense(h2, x1, ga, w1, w3, w2, tm):
    b, n, d = x1.shape
    ff = w1.shape[1]
    fc = 256 if ff % 256 == 0 else ff
    tok = pl.BlockSpec((1, tm, d), lambda i, j: (i, j, 0))
    per_b = pl.BlockSpec((1, 1, d), lambda i, j: (i, 0, 0))
    full = lambda a: pl.BlockSpec(a.shape, lambda i, j: (0, 0), pipeline_mode=pl.Buffered(1))
    return pl.pallas_call(
        functools.partial(_ffn_kernel, fc=fc),
        grid=(b, n // tm),
        in_specs=[tok, tok, per_b, full(w1), full(w3), full(w2)],
        out_specs=tok,
        out_shape=jax.ShapeDtypeStruct((b, n, d), F32),
        compiler_params=_cparams(("parallel", "parallel"), VMEM_LIMIT),
        name="ffn_dense",
    )(h2, x1, ga, w1, w3, w2)


def _expert_kernel(te_ref, tv_ref, rows_ref, h_hbm, w1_ref, w3_ref, w2_ref, y_hbm, xs_ref, ys_ref,
                   gsem, ssem, zsem, *, fc, n_tiles, n_slots, n_dump_tiles):
    i = pl.program_id(0)
    tme = xs_ref.shape[1]
    nf = w1_ref.shape[2] // fc
    valid = tv_ref[i] > 0
    prev_valid = jnp.logical_and(i > 0, tv_ref[jnp.maximum(i - 1, 0)] > 0)
    slot = i % 2
    prev = jnp.maximum(i - 1, 0)

    def token_of(v):
        if n_slots & (n_slots - 1) == 0:
            return jnp.bitwise_and(v, n_slots - 1)
        return lax.rem(v, n_slots)

    def gather_row(tile, r, sl):
        pltpu.make_async_copy(h_hbm.at[token_of(rows_ref[tile * tme + r])], xs_ref.at[sl, r],
                              gsem.at[sl]).start()

    def scatter_row(r, sl):
        pltpu.make_async_copy(ys_ref.at[sl, r], y_hbm.at[rows_ref[prev * tme + r]], ssem.at[sl]).start()

    def wait_gather(sl):
        pltpu.make_async_copy(h_hbm.at[pl.ds(0, tme)], xs_ref.at[sl], gsem.at[sl]).wait()

    def wait_scatter(sl):
        pltpu.make_async_copy(ys_ref.at[sl], y_hbm.at[pl.ds(0, tme)], ssem.at[sl]).wait()

    @pl.when(i == 0)
    def _():
        ys_ref[1] = jnp.zeros(ys_ref.shape[1:], F32)
        fills = [pltpu.make_async_copy(ys_ref.at[1], y_hbm.at[pl.ds(2 * n_slots + e * tme, tme)], zsem)
                 for e in range(n_dump_tiles)]
        for cp in fills:
            cp.start()
        for cp in fills:
            cp.wait()

        def issue(r, c):
            gather_row(0, r, 0)
            return c

        lax.fori_loop(0, tme, issue, 0, unroll=8)

    @pl.when(jnp.logical_or(i == 0, prev_valid))
    def _():
        wait_gather(slot)

    @pl.when(prev_valid)
    def _():
        wait_scatter(slot)

    @pl.when(valid)
    def _():
        nxt = jnp.minimum(i + 1, n_tiles - 1)
        xb = xs_ref[slot].astype(BF16)
        cuts = [(k * tme) // nf for k in range(nf + 1)]
        gated = []
        for j in range(nf):
            for r in range(cuts[j], cuts[j + 1]):
                gather_row(nxt, r, 1 - slot)
                scatter_row(r, 1 - slot)
            a = jnp.dot(xb, w1_ref[0, :, j * fc:(j + 1) * fc], preferred_element_type=F32)
            bb = jnp.dot(xb, w3_ref[0, :, j * fc:(j + 1) * fc], preferred_element_type=F32)
            gated.append((_silu(a) * bb).astype(BF16))
        ys_ref[slot] = jnp.dot(jnp.concatenate(gated, axis=1), w2_ref[0], preferred_element_type=F32)

    @pl.when(jnp.logical_and(jnp.logical_not(valid), prev_valid))
    def _():
        def issue(r, c):
            scatter_row(r, 1 - slot)
            return c

        lax.fori_loop(0, tme, issue, 0, unroll=8)
        wait_scatter(1 - slot)


def _combine_kernel(x_ref, y0_ref, y1_ref, ga_ref, rt_ref, o_ref):
    rt = rt_ref[0]
    y = rt[:, TOP_K:TOP_K + 1] * y0_ref[...] + rt[:, TOP_K + 1:TOP_K + 2] * y1_ref[...]
    o_ref[0] = x_ref[0] + ga_ref[0] * y


def _moe(h2, x1, ga, rt, w1, w3, w2, tme, tc):
    b, n_b, d = h2.shape
    n = b * n_b
    n_exp, _, ff = w1.shape
    fc = 512 if ff % 512 == 0 else ff
    e_idx = rt[..., 0:TOP_K].astype(jnp.int32).reshape(-1)
    onehot = (e_idx[:, None] == jnp.arange(n_exp)[None, :]).astype(jnp.int32)
    pos = jnp.take_along_axis(jnp.cumsum(onehot, axis=0) - onehot, e_idx[:, None], axis=1)[:, 0]
    counts = jnp.sum(onehot, axis=0)
    padded = ((counts + tme - 1) // tme) * tme
    ends = jnp.cumsum(padded)
    offs = ends - padded
    dest = offs[e_idx] + pos
    n_rows = n * TOP_K + n_exp * tme
    n_tiles = n_rows // tme
    tile_start = jnp.arange(n_tiles + 1, dtype=jnp.int32) * tme
    tvalid = (tile_start < ends[-1]).astype(jnp.int32)
    texp = jnp.minimum(jnp.sum((tile_start[:, None] >= ends[None, :]).astype(jnp.int32), axis=1), n_exp - 1)
    texp = jnp.where(tvalid > 0, texp, texp[jnp.maximum(ends[-1] // tme - 1, 0)])
    flat = jnp.arange(n * TOP_K, dtype=jnp.int32)
    row_e = jnp.repeat(texp[:n_tiles], tme)
    first_pad = jnp.sum(jnp.where(row_e[:, None] == jnp.arange(n_exp)[None, :], (offs + counts)[None, :], 0),
                        axis=1)
    pad_idx = jnp.clip(jnp.arange(n_rows, dtype=jnp.int32) - first_pad, 0, tme - 1)
    rows = (2 * n + row_e * tme + pad_idx).astype(jnp.int32).at[dest].set((flat % TOP_K) * n + flat // TOP_K)
    wspec = lambda shape: pl.BlockSpec((1,) + shape, lambda i, te, tv, s: (te[i], 0, 0),
                                       pipeline_mode=pl.Buffered(1))
    ys = pl.pallas_call(
        functools.partial(_expert_kernel, fc=fc, n_tiles=n_tiles, n_slots=n, n_dump_tiles=n_exp),
        grid_spec=pltpu.PrefetchScalarGridSpec(
            num_scalar_prefetch=3,
            grid=(n_tiles + 1,),
            in_specs=[pl.BlockSpec(memory_space=pl.ANY), wspec((d, ff)), wspec((d, ff)), wspec((ff, d))],
            out_specs=pl.BlockSpec(memory_space=pl.ANY),
            scratch_shapes=[pltpu.VMEM((2, tme, d), F32), pltpu.VMEM((2, tme, d), F32),
                            pltpu.SemaphoreType.DMA((2,)), pltpu.SemaphoreType.DMA((2,)),
                            pltpu.SemaphoreType.DMA(())]),
        out_shape=jax.ShapeDtypeStruct((2 * n + n_exp * tme, d), F32),
        compiler_params=_cparams(("arbitrary",), VMEM_LIMIT),
        name="experts",
    )(texp, tvalid, rows, h2.reshape(n, d), w1, w3, w2)

    nb = n_b // tc
    tok = lambda w: pl.BlockSpec((1, tc, w), lambda i, j: (i, j, 0))
    return pl.pallas_call(
        _combine_kernel,
        grid=(b, nb),
        in_specs=[tok(d),
                  pl.BlockSpec((tc, d), lambda i, j: (i * nb + j, 0)),
                  pl.BlockSpec((tc, d), lambda i, j: (n // tc + i * nb + j, 0)),
                  pl.BlockSpec((1, 1, d), lambda i, j: (i, 0, 0)), tok(LANES)],
        out_specs=tok(d),
        out_shape=jax.ShapeDtypeStruct((b, n_b, d), F32),
        compiler_params=_cparams(("parallel", "parallel"), VMEM_LIMIT),
        name="moe_combine",
    )(x1, ys, ys, ga, rt)


def _pick_tile(n, pref):
    t = pref
    while n % t:
        t //= 2
    return t


def kernel(x, c, ctx, c_ctx, w_ada, b_ada, g_mix, g_ffn, w_in, w_out, g_cq, g_ckv, w_uq, w_ukv,
           g_mla_qn, g_mla_qr, g_mla_kn, g_mla_kr, w_fourier, w_pool, pool_scale, g_na_q, g_na_k,
           na_rpb, w1_dense, w3_dense, w2_dense, w_router, w1_moe, w3_moe, w2_moe):
    b, n, d = x.shape
    nctx = ctx.shape[1]
    depth = w_ada.shape[0]
    assert b <= 7 and n % GRID_W == 0

    cc = jnp.zeros((8, d), F32).at[:b].set(c).at[b].set(c_ctx)
    mods = _adaln(cc, w_ada, b_ada)
    cos_x, sin_x = _rope_tables(n)
    cos_c = jnp.ones((nctx, LANES), F32)
    sin_c = jnp.zeros((nctx, LANES), F32)
    sw = _rope_swap_perm()
    tm_x = _pick_tile(n, 512)
    tm_c = _pick_tile(nctx, 256)
    amax = lambda g: jnp.max(jnp.abs(g.astype(F32)))

    for l in range(depth):
        last = l == depth - 1
        mx = mods[l, :b].reshape(b, 1, 6 * d)
        mc = jnp.broadcast_to(mods[l, b].reshape(1, 1, 6 * d), (b, 1, 6 * d))
        part = lambda m, k: m[:, :, k * d:(k + 1) * d]

        bound_mla = (MLA_SCALE * LOG2E) * (MLA_NOPE * amax(g_mla_qn[l]) * amax(g_mla_kn[l])
                                           + MLA_ROPE * amax(g_mla_qr[l]) * amax(g_mla_kr[l]))
        qk_na = (NA_SCALE * LOG2E) * NA_HEAD_DIM * amax(g_na_q[l]) * amax(g_na_k[l])
        rpb_hi = jnp.maximum(jnp.max(na_rpb[l]).astype(F32), 0.0) * LOG2E
        rpb_lo = jnp.minimum(jnp.min(na_rpb[l]).astype(F32), 0.0) * LOG2E
        bound_na = qk_na + rpb_hi
        mla_bounded = 2.0 * bound_mla <= MAX_SOFTMAX_GAP_LOG2
        na_bounded = 2.0 * qk_na + rpb_hi - rpb_lo <= MAX_SOFTMAX_GAP_LOG2
        bounds = jnp.zeros((1, LANES), F32).at[0, 0].set(bound_mla).at[0, 1].set(bound_na)

        wts = [
            _prep_in_weights(w_in[l]),
            g_cq[l].reshape(1, -1), _prep_uq(w_uq[l]),
            _lane_vec([(g_mla_qn[l], 64), (g_mla_qr[l], 32), (None, 32)]),
            _lane_vec([(None, 64), (g_mla_qr[l][sw], 32), (None, 32)]),
            g_ckv[l].reshape(1, -1), _prep_ukv(w_ukv[l]),
            _lane_vec([(g_mla_kn[l], 64), (None, 64)]),
            _lane_vec([(None, 64), (g_mla_kr[l], 32), (None, 32)]),
            _lane_vec([(None, 64), (g_mla_kr[l][sw], 32), (None, 32)]),
            jnp.tile(g_na_q[l], NA_HEADS).reshape(1, -1), jnp.tile(g_na_k[l], NA_HEADS).reshape(1, -1),
        ]
        gmix = g_mix[l].reshape(1, d)
        qx, kx, vx, fx, px, nqx, nkx, nvx = _inproj(x, part(mx, 0), part(mx, 1), gmix, cos_x, sin_x,
                                                    bounds, wts, tm_x)
        qc, kc, vc, fc_, pc, nqc, nkc, nvc = _inproj(ctx, part(mc, 0), part(mc, 1), gmix, cos_c, sin_c,
                                                     bounds, wts, tm_c)

        wf = w_fourier[l].astype(BF16)
        cg = w_pool.shape[-1]
        wp_bd = jnp.zeros((len(POOL_WINDOWS) * cg,) * 2, F32)
        for gi in range(len(POOL_WINDOWS)):
            wp_bd = wp_bd.at[gi * cg:(gi + 1) * cg, gi * cg:(gi + 1) * cg].set(w_pool[l, gi])
        wp_bd = wp_bd.astype(BF16)
        ps = pool_scale[l].reshape(1, -1)
        w_out_l = w_out[l].astype(BF16)
        gffn = g_ffn[l].reshape(1, d)

        o_mla = _flash_pairs(qx, [(kx, vx), (kc, vc)], _pick_tile(n, 256), mla_bounded)
        o_f = _fourier_latent(fx, wf)
        o_p = _pool(px, wp_bd, ps, _pick_tile(n, 512))
        o_na = _na_latent(nqx, nkx, nvx, nkc, nvc, _na_bias_tables(na_rpb[l]), na_bounded)

        moe_layer = l % 2 == 1
        i = l // 2
        res = _outproj(x, o_mla, o_f, o_p, o_na, w_out_l, part(mx, 2), gffn, part(mx, 3), part(mx, 4),
                       _pick_tile(n, 256) if moe_layer else tm_x,
                       w_router=w_router[i] if moe_layer else None)
        if moe_layer:
            x1, h2, rt = res
            w1, w3, w2 = w1_moe[i].astype(BF16), w3_moe[i].astype(BF16), w2_moe[i].astype(BF16)
            x = _moe(h2, x1, part(mx, 5), rt, w1, w3, w2, tme=512, tc=256)
        else:
            x1, h2 = res
            w1, w3, w2 = w1_dense[i].astype(BF16), w3_dense[i].astype(BF16), w2_dense[i].astype(BF16)
            x = _ffn_dense(h2, x1, part(mx, 5), w1, w3, w2, _pick_tile(n, 512))

        if not last:
            oc_mla = _flash_pairs(qc, [(kc, vc)], tm_c, mla_bounded)
            oc_f = _fourier_dense(fc_, wf)
            oc_p = _pool(pc, wp_bd, ps, tm_c)
            oc_na = _flash_pairs(nqc, [(nkc, nvc)], tm_c, na_bounded)
            resc = _outproj(ctx, oc_mla, oc_f, oc_p, oc_na, w_out_l, part(mc, 2), gffn, part(mc, 3),
                            part(mc, 4), tm_c, w_router=w_router[i] if moe_layer else None)
            if moe_layer:
                c1, hc2, rtc = resc
                ctx = _moe(hc2, c1, part(mc, 5), rtc, w1, w3, w2, tme=256, tc=256)
            else:
                c1, hc2 = resc
                ctx = _ffn_dense(hc2, c1, part(mc, 5), w1, w3, w2, tm_c)
    return x
```

```python
import functools
import math

import numpy as np
import jax
import jax.numpy as jnp
from jax import lax
from jax.experimental import pallas as pl
from jax.experimental.pallas import tpu as pltpu

F32 = jnp.float32
BF16 = jnp.bfloat16

GRID_W = 64
LANES = 128
EPS = 1e-6
NEG_INF = -1e30

MLA_HEADS = 4
MLA_NOPE = 64
MLA_ROPE = 32
MLA_V = 64
MLA_SCALE = (MLA_NOPE + MLA_ROPE) ** -0.5
MLA_SPARE_LANE = MLA_NOPE + MLA_ROPE
ROPE_BASE = 10000.0
LOG2E = 1.0 / math.log(2.0)
MAX_SOFTMAX_GAP_LOG2 = 120.0

FOURIER_GROUPS = 4
POOL_WINDOWS = (2, 4, 8, 16)
POOL_HALO = 8

NA_HEADS = 4
NA_HEAD_DIM = 64
NA_SCALE = NA_HEAD_DIM ** -0.5
NA_WIN_ROWS = 8
NA_WIN_COLS = 16
NA_QROWS = 8
NA_KROWS = 16
NA_BLOCKS_PER_STEP = 8

TOP_K = 2
VMEM_LIMIT = 56 * 1024 * 1024


def _cparams(sem, vmem=None):
    return pltpu.CompilerParams(dimension_semantics=sem, vmem_limit_bytes=vmem)


def _silu(a):
    return a / (1.0 + jnp.exp(-a))


def _rms(v, n):
    return lax.rsqrt(jnp.sum(v * v, axis=-1, keepdims=True) * (1.0 / n) + EPS)


def _adaln_kernel(c_ref, w_ref, b_ref, o_ref):
    o_ref[0] = jnp.dot(_silu(c_ref[...]), w_ref[0], preferred_element_type=F32) + b_ref[0]


def _adaln(cc, w_ada, b_ada):
    depth, d, d6 = w_ada.shape
    tn = 512
    return pl.pallas_call(
        _adaln_kernel,
        grid=(depth, d6 // tn),
        in_specs=[pl.BlockSpec((8, d), lambda l, j: (0, 0)),
                  pl.BlockSpec((1, d, tn), lambda l, j: (l, 0, j)),
                  pl.BlockSpec((1, 1, tn), lambda l, j: (l, 0, j))],
        out_specs=pl.BlockSpec((1, 8, tn), lambda l, j: (l, 0, j)),
        out_shape=jax.ShapeDtypeStruct((depth, 8, d6), F32),
        compiler_params=_cparams(("arbitrary", "arbitrary")),
        name="adaln",
    )(cc, w_ada, b_ada.reshape(depth, 1, d6))


IN_COLS = 1920


def _rope_swap_perm():
    j = np.arange(MLA_ROPE)
    return np.where((j % 16) < 8, j + 8, j - 8)


def _prep_in_weights(w_in_l):
    d = w_in_l.shape[0]
    s = [0, 256, 384, 416, 672, 928, 1184, 1440, 1696]
    cq, ckv, kr, f, p, nq, nk, nv = [w_in_l[:, s[i]:s[i + 1]] for i in range(8)]
    z = lambda n: jnp.zeros((d, n), w_in_l.dtype)
    krsw = kr[:, _rope_swap_perm()]
    return jnp.concatenate([cq, ckv, f, p, nq, nk, nv, z(64), kr, z(32), z(64), krsw, z(32)],
                           axis=1).astype(BF16)


def _prep_uq(w_uq_l):
    r = w_uq_l.shape[0]
    w = w_uq_l.reshape(r, MLA_HEADS, MLA_NOPE + MLA_ROPE)
    z = lambda n: jnp.zeros((r, MLA_HEADS, n), w.dtype)
    main = jnp.concatenate([w, z(32)], axis=-1)
    sw = jnp.concatenate([z(64), w[..., MLA_NOPE:][..., _rope_swap_perm()], z(32)], axis=-1)
    return jnp.concatenate([main.reshape(r, -1), sw.reshape(r, -1)], axis=1).astype(BF16)


def _prep_ukv(w_ukv_l):
    r = w_ukv_l.shape[0]
    w = w_ukv_l.reshape(r, MLA_HEADS, MLA_NOPE + MLA_V)
    z = jnp.zeros((r, 64), w.dtype)
    tiles = [jnp.concatenate([w[:, h, :MLA_NOPE], z], axis=1) for h in range(MLA_HEADS)]
    for h in range(MLA_HEADS):
        v = w[:, h, MLA_NOPE:]
        tiles.append(jnp.concatenate([v, z] if h % 2 == 0 else [z, v], axis=1))
    return jnp.concatenate(tiles, axis=1).astype(BF16)


def _lane_vec(parts):
    cols = [jnp.zeros((n,), F32) if a is None else a.astype(F32) for a, n in parts]
    return jnp.concatenate(cols).reshape(1, -1)


def _rope_tables(n_tokens):
    t = np.arange(n_tokens)
    pos = np.stack([t // GRID_W, t % GRID_W], axis=1).astype(np.float64)
    half = MLA_ROPE // 2
    inv = 1.0 / (ROPE_BASE ** (np.arange(0, half, 2, dtype=np.float64) / half))
    j = np.arange(MLA_ROPE)
    ang = pos[:, j // 16] * inv[j % 8]
    sign = np.where((j % 16) < 8, -1.0, 1.0)
    cos = np.ones((n_tokens, LANES), np.float32)
    sin = np.zeros((n_tokens, LANES), np.float32)
    cos[:, 64:96] = np.cos(ang)
    sin[:, 64:96] = np.sin(ang) * sign
    return jnp.asarray(cos), jnp.asarray(sin)


def _inproj_kernel(x_ref, sh_ref, sc_ref, g_ref, cos_ref, sin_ref, bound_ref, win_ref,
                   gcq_ref, wuq_ref, gq_ref, gqsw_ref,
                   gckv_ref, wukv_ref, gkn_ref, gkr_ref, gkrsw_ref, gnq_ref, gnk_ref,
                   q_ref, k_ref, v_ref, f_ref, p_ref, nq_ref, nk_ref, nv_ref):
    x = x_ref[0]
    d = x.shape[-1]
    h = (x * _rms(x, d)) * g_ref[...]
    h = h * (1.0 + sc_ref[0]) + sh_ref[0]
    u = jnp.dot(h.astype(BF16), win_ref[...], preferred_element_type=F32)
    cq, ckv = u[:, 0:256], u[:, 256:384]
    f_ref[0] = u[:, 384:640].astype(BF16)
    p_ref[0] = u[:, 640:896]
    nq, nk, nv = u[:, 896:1152], u[:, 1152:1408], u[:, 1408:1664]
    krb, krs = u[:, 1664:1792], u[:, 1792:1920]
    cos, sin = cos_ref[...], sin_ref[...]
    lane = lax.broadcasted_iota(jnp.int32, (1, LANES), 1)
    m_nope = lane < MLA_NOPE
    m_rope = jnp.logical_and(lane >= MLA_NOPE, lane < MLA_NOPE + MLA_ROPE)
    bound_mla, bound_na = bound_ref[:, 0:1], bound_ref[:, 1:2]

    cqn = (cq * _rms(cq, 256) * gcq_ref[...]).astype(BF16)
    qall = jnp.dot(cqn, wuq_ref[...], preferred_element_type=F32)
    for hd in range(MLA_HEADS):
        blk = qall[:, hd * LANES:(hd + 1) * LANES]
        sw = qall[:, (MLA_HEADS + hd) * LANES:(MLA_HEADS + hd + 1) * LANES]
        sq = blk * blk
        rn = lax.rsqrt(jnp.sum(jnp.where(m_nope, sq, 0.0), -1, keepdims=True) * (1.0 / MLA_NOPE) + EPS)
        rr = lax.rsqrt(jnp.sum(jnp.where(m_rope, sq, 0.0), -1, keepdims=True) * (1.0 / MLA_ROPE) + EPS)
        qh = blk * jnp.where(m_nope, rn, rr) * gq_ref[...]
        qs = sw * rr * gqsw_ref[...]
        q_ref[0, hd] = jnp.where(lane == MLA_SPARE_LANE, -bound_mla,
                                 (qh * cos + qs * sin) * (MLA_SCALE * LOG2E)).astype(BF16)

    ckvn = (ckv * _rms(ckv, 128) * gckv_ref[...]).astype(BF16)
    kvall = jnp.dot(ckvn, wukv_ref[...], preferred_element_type=F32)
    rkr = _rms(krb, MLA_ROPE)
    krot = (krb * rkr * gkr_ref[...]) * cos + (krs * rkr * gkrsw_ref[...]) * sin
    for hd in range(MLA_HEADS):
        blk = kvall[:, hd * LANES:(hd + 1) * LANES]
        k_ref[0, hd] = jnp.where(lane == MLA_SPARE_LANE, 1.0,
                                 blk * _rms(blk, MLA_NOPE) * gkn_ref[...] + krot).astype(BF16)
        vb = kvall[:, (MLA_HEADS + hd) * LANES:(MLA_HEADS + hd + 1) * LANES]
        one_lane = 64 if hd % 2 == 0 else 0
        v_ref[0, hd] = jnp.where(lane == one_lane, 1.0, vb).astype(BF16)

    lane2 = lax.broadcasted_iota(jnp.int32, (1, 2 * LANES), 1)

    def seg_rms(t):
        sq = t * t
        r = jnp.zeros_like(t)
        for s in range(NA_HEADS):
            m = (lane2 // NA_HEAD_DIM) == s
            ss = jnp.sum(jnp.where(m, sq, 0.0), -1, keepdims=True)
            r = jnp.where(m, lax.rsqrt(ss * (1.0 / NA_HEAD_DIM) + EPS), r)
        return r

    nqn = nq * seg_rms(nq) * gnq_ref[...] * (NA_SCALE * LOG2E)
    nkn = nk * seg_rms(nk) * gnk_ref[...]
    for hd in range(NA_HEADS):
        t0 = (hd // 2) * LANES
        own = (lane < 64) if hd % 2 == 0 else (lane >= 64)
        one_lane = 64 if hd % 2 == 0 else 0
        spare = lane == (LANES - 1 if hd % 2 == 0 else 0)
        nq_ref[0, hd] = jnp.where(own, nqn[:, t0:t0 + LANES], jnp.where(spare, -bound_na, 0.0)).astype(BF16)
        nk_ref[0, hd] = jnp.where(own, nkn[:, t0:t0 + LANES], jnp.where(spare, 1.0, 0.0)).astype(BF16)
        nv_ref[0, hd] = jnp.where(own, nv[:, t0:t0 + LANES],
                                  jnp.where(lane == one_lane, 1.0, 0.0)).astype(BF16)


def _inproj(x, sh, sc, g, cos, sin, bounds, wts, tm):
    b, n, d = x.shape
    heads = MLA_HEADS
    row = lambda a: pl.BlockSpec(a.shape, lambda i, j: (0,) * a.ndim)
    per_b = pl.BlockSpec((1, 1, d), lambda i, j: (i, 0, 0))
    tok = lambda w: pl.BlockSpec((1, tm, w), lambda i, j: (i, j, 0))
    hd_spec = pl.BlockSpec((1, heads, tm, LANES), lambda i, j: (i, 0, j, 0))
    tab = pl.BlockSpec((tm, LANES), lambda i, j: (j, 0))
    hshape = jax.ShapeDtypeStruct((b, heads, n, LANES), BF16)
    return pl.pallas_call(
        _inproj_kernel,
        grid=(b, n // tm),
        in_specs=[tok(d), per_b, per_b, row(g), tab, tab, row(bounds)] + [row(w) for w in wts],
        out_specs=[hd_spec, hd_spec, hd_spec, tok(256), tok(256), hd_spec, hd_spec, hd_spec],
        out_shape=[hshape, hshape, hshape,
                   jax.ShapeDtypeStruct((b, n, 256), BF16), jax.ShapeDtypeStruct((b, n, 256), F32),
                   hshape, hshape, hshape],
        compiler_params=_cparams(("parallel", "parallel"), VMEM_LIMIT),
        name="inproj",
    )(x, sh, sc, g, cos, sin, bounds, *wts)


_DN_T = (((1,), (1,)), ((), ()))


def _pair_output(accs):
    lane = lax.broadcasted_iota(jnp.int32, (1, LANES), 1)
    out = None
    for hh, acc in enumerate(accs):
        one_lane = 64 if hh == 0 else 0
        own = (lane < 64) if hh == 0 else (lane >= 64)
        o = jnp.where(own, acc / acc[:, one_lane:one_lane + 1], 0.0)
        out = o if out is None else out + o
    return out


def _flash_pair_kernel(q_ref, *refs, chunks, bounded):
    o_ref = refs[-1]
    tq = q_ref.shape[2]
    accs = []
    for hh in range(2):
        q = q_ref[0, hh]
        acc = jnp.zeros((tq, LANES), F32)
        m = jnp.full((tq, 1), NEG_INF, F32)
        for kset, start, size in chunks:
            ks = refs[2 * kset][0, hh, start:start + size, :]
            vs = refs[2 * kset + 1][0, hh, start:start + size, :]
            s = lax.dot_general(q, ks, _DN_T, preferred_element_type=F32)
            if bounded:
                acc = acc + jnp.dot(jnp.exp2(s).astype(BF16), vs, preferred_element_type=F32)
            else:
                m_new = jnp.maximum(m, jnp.max(s, axis=-1, keepdims=True))
                p = jnp.exp2(s - m_new).astype(BF16)
                acc = acc * jnp.exp2(m - m_new) + jnp.dot(p, vs, preferred_element_type=F32)
                m = m_new
        accs.append(acc)
    o_ref[0] = _pair_output(accs).astype(o_ref.dtype)


def _flash_pairs(q, kv_sets, tq, bounded, tk=2048):
    b, heads, lq, _ = q.shape
    chunks, ops, kv_specs = [], [], []
    for si, (k, v) in enumerate(kv_sets):
        lk = k.shape[2]
        step = tk if lk % tk == 0 else lk
        chunks += [(si, st, step) for st in range(0, lk, step)]
        ops += [k, v]
        kv_specs += [pl.BlockSpec((1, 2, lk, LANES), lambda i, p, j: (i, p, 0, 0))] * 2

    def call(flag):
        return pl.pallas_call(
            functools.partial(_flash_pair_kernel, chunks=tuple(chunks), bounded=flag),
            grid=(b, heads // 2, lq // tq),
            in_specs=[pl.BlockSpec((1, 2, tq, LANES), lambda i, p, j: (i, p, j, 0))] + kv_specs,
            out_specs=pl.BlockSpec((1, tq, LANES), lambda i, p, j: (i, j, p)),
            out_shape=jax.ShapeDtypeStruct((b, lq, heads * 64), BF16),
            compiler_params=_cparams(("parallel", "parallel", "parallel"), VMEM_LIMIT),
            name="flash_pairs" if flag else "flash_pairs_online",
        )

    return lax.cond(bounded, call(True), call(False), q, *ops)


def _table(a):
    return jnp.asarray(a, F32).astype(BF16)


def _dft_consts(n_tokens):
    rows = n_tokens // GRID_W
    r = np.arange(rows)
    ang_r = 2 * np.pi * np.outer(r, r) / rows
    a_mat = np.concatenate([np.cos(ang_r), -np.sin(ang_r)], axis=0)
    c = np.arange(GRID_W)
    ang_t = 2 * np.pi * np.outer(r, c) / n_tokens
    ang_c = 2 * np.pi * np.outer(c, c) / GRID_W
    cc, sc = np.cos(ang_c), np.sin(ang_c)
    m_b = np.block([[cc, sc], [-sc, cc]])
    return a_mat, np.cos(ang_t), np.sin(ang_t), m_b


def _channel_dft(n_tokens, width):
    cg = width // FOURIER_GROUPS
    j = np.arange(cg)
    ang = 2 * np.pi * np.outer(j, j) / cg
    eye = np.eye(FOURIER_GROUPS)
    scale = 1.0 / math.sqrt(n_tokens * cg)
    return np.kron(eye, np.cos(ang)) * scale, np.kron(eye, np.sin(ang)) * scale


def _fourier_rows_kernel(u_ref, a_ref, tc_ref, ts_ref, o_ref):
    z = jnp.dot(a_ref[...], u_ref[0], preferred_element_type=F32)
    r = z.shape[0] // 2
    zr, zi = z[:r], z[r:]
    c, s = tc_ref[...], ts_ref[...]
    o_ref[0, :r] = (zr * c + zi * s).astype(BF16)
    o_ref[0, r:] = (zi * c - zr * s).astype(BF16)


def _fourier_cols_kernel(z_ref, mb_ref, cs_ref, wf_ref, o_ref):
    kb = z_ref.shape[1]
    w = GRID_W
    cw = wf_ref.shape[0]
    ys = [jnp.dot(mb_ref[...], z_ref[0, t], preferred_element_type=F32) for t in range(kb)]
    yr = jnp.concatenate([y[:w] for y in ys], axis=0).astype(BF16)
    yi = jnp.concatenate([y[w:] for y in ys], axis=0).astype(BF16)
    f = (jnp.dot(yr, cs_ref[:cw], preferred_element_type=F32)
         + jnp.dot(yi, cs_ref[cw:], preferred_element_type=F32))
    out = jnp.dot(f.astype(BF16), wf_ref[...], preferred_element_type=F32)
    for t in range(kb):
        o_ref[0, t] = out[t * w:(t + 1) * w].astype(o_ref.dtype)


def _fourier_latent(uf, wf):
    b, n, cw = uf.shape
    rows = n // GRID_W
    a_mat, tw_c, tw_s, m_b = _dft_consts(n)
    cc, sc = _channel_dft(n, cw)
    wide = GRID_W * cw
    tn = 2048
    expand = lambda t: jnp.broadcast_to(jnp.asarray(t, F32)[:, :, None], (rows, GRID_W, cw)).reshape(rows, wide)
    z = pl.pallas_call(
        _fourier_rows_kernel,
        grid=(wide // tn, b),
        in_specs=[pl.BlockSpec((1, rows, tn), lambda j, i: (i, 0, j)),
                  pl.BlockSpec((2 * rows, rows), lambda j, i: (0, 0)),
                  pl.BlockSpec((rows, tn), lambda j, i: (0, j)),
                  pl.BlockSpec((rows, tn), lambda j, i: (0, j))],
        out_specs=pl.BlockSpec((1, 2 * rows, tn), lambda j, i: (i, 0, j)),
        out_shape=jax.ShapeDtypeStruct((b, 2 * rows, wide), BF16),
        compiler_params=_cparams(("parallel", "parallel")),
        name="fourier_rows",
    )(uf.reshape(b, rows, wide), _table(a_mat), expand(tw_c), expand(tw_s))
    z = z.reshape(b, 2, rows, GRID_W, cw).transpose(0, 2, 1, 3, 4).reshape(b, rows, 2 * GRID_W, cw)
    kb = 16
    y = pl.pallas_call(
        _fourier_cols_kernel,
        grid=(b, rows // kb),
        in_specs=[pl.BlockSpec((1, kb, 2 * GRID_W, cw), lambda i, j: (i, j, 0, 0)),
                  pl.BlockSpec((2 * GRID_W, 2 * GRID_W), lambda i, j: (0, 0)),
                  pl.BlockSpec((2 * cw, cw), lambda i, j: (0, 0)),
                  pl.BlockSpec((cw, cw), lambda i, j: (0, 0))],
        out_specs=pl.BlockSpec((1, kb, GRID_W, cw), lambda i, j: (i, j, 0, 0)),
        out_shape=jax.ShapeDtypeStruct((b, rows, GRID_W, cw), BF16),
        compiler_params=_cparams(("parallel", "parallel")),
        name="fourier_cols",
    )(z, _table(m_b), _table(np.concatenate([cc, sc], axis=0)), wf)
    return y.transpose(0, 2, 1, 3).reshape(b, n, cw)


def _fourier_dense_kernel(u_ref, cl_ref, sl_ref, cs_ref, wf_ref, o_ref):
    u = u_ref[0]
    cw = wf_ref.shape[0]
    a = jnp.dot(u, cs_ref[:cw], preferred_element_type=F32).astype(BF16)
    bb = jnp.dot(u, cs_ref[cw:], preferred_element_type=F32).astype(BF16)
    f = (jnp.dot(cl_ref[...], a, preferred_element_type=F32)
         - jnp.dot(sl_ref[...], bb, preferred_element_type=F32))
    o_ref[0] = jnp.dot(f.astype(BF16), wf_ref[...], preferred_element_type=F32).astype(o_ref.dtype)


def _fourier_dense(uf, wf):
    b, n, cw = uf.shape
    t = np.arange(n)
    ang = 2 * np.pi * np.outer(t, t) / n
    cc, sc = _channel_dft(n, cw)
    full = lambda shape: pl.BlockSpec(shape, lambda i: (0,) * len(shape))
    return pl.pallas_call(
        _fourier_dense_kernel,
        grid=(b,),
        in_specs=[pl.BlockSpec((1, n, cw), lambda i: (i, 0, 0)), full((n, n)), full((n, n)),
                  full((2 * cw, cw)), full((cw, cw))],
        out_specs=pl.BlockSpec((1, n, cw), lambda i: (i, 0, 0)),
        out_shape=jax.ShapeDtypeStruct((b, n, cw), BF16),
        compiler_params=_cparams(("parallel",)),
        name="fourier_dense",
    )(uf, _table(np.cos(ang)), _table(np.sin(ang)), _table(np.concatenate([cc, sc], axis=0)), wf)


def _pool_kernel(x_ref, prev_ref, next_ref, wp_ref, ps_ref, o_ref, pad_ref, *, n_tokens):
    i = pl.program_id(1)
    tm = x_ref.shape[1]
    h = POOL_HALO
    pad_ref[0:h] = jnp.where(i > 0, prev_ref[0], 0.0)
    pad_ref[h:h + tm] = x_ref[0]
    pad_ref[h + tm:2 * h + tm] = jnp.where(i < pl.num_programs(1) - 1, next_ref[0], 0.0)
    ld = lambda off: pad_ref[h + off:h + off + tm, :]
    x0 = ld(0)
    sums = []
    acc = x0
    lo, hi = 0, 0
    for w in POOL_WINDOWS:
        for off in list(range(-(w // 2), lo)) + list(range(hi + 1, w // 2)):
            acc = acc + ld(off)
        lo, hi = -(w // 2), w // 2 - 1
        sums.append(acc)
    t = i * tm + lax.broadcasted_iota(jnp.int32, (tm, 1), 0)
    grp = lax.broadcasted_iota(jnp.int32, (1, x0.shape[1]), 1) // (x0.shape[1] // len(POOL_WINDOWS))
    pooled = jnp.zeros_like(x0)
    for gi, w in enumerate(POOL_WINDOWS):
        cnt = (jnp.minimum(t + w // 2, n_tokens) - jnp.maximum(t - w // 2, 0)).astype(F32)
        pooled = jnp.where(grp == gi, sums[gi] / cnt, pooled)
    pooled = pooled - x0
    y = jnp.dot(pooled.astype(BF16), wp_ref[...], preferred_element_type=F32) * ps_ref[...]
    o_ref[0] = y.astype(o_ref.dtype)


def _pool(up, wp_bd, ps, tm):
    b, n, cw = up.shape
    h = POOL_HALO
    nblk = n // h
    return pl.pallas_call(
        functools.partial(_pool_kernel, n_tokens=n),
        grid=(b, n // tm),
        in_specs=[pl.BlockSpec((1, tm, cw), lambda i, j: (i, j, 0)),
                  pl.BlockSpec((1, h, cw), lambda i, j: (i, jnp.maximum(j * (tm // h) - 1, 0), 0)),
                  pl.BlockSpec((1, h, cw), lambda i, j: (i, jnp.minimum((j + 1) * (tm // h), nblk - 1), 0)),
                  pl.BlockSpec((cw, cw), lambda i, j: (0, 0)),
                  pl.BlockSpec((1, cw), lambda i, j: (0, 0))],
        out_specs=pl.BlockSpec((1, tm, cw), lambda i, j: (i, j, 0)),
        out_shape=jax.ShapeDtypeStruct((b, n, cw), BF16),
        scratch_shapes=[pltpu.VMEM((tm + 2 * h, cw), F32)],
        compiler_params=_cparams(("parallel", "parallel")),
        name="pool",
    )(up, up, up, wp_bd, ps)


def _na_row_cases():
    qr = np.arange(NA_QROWS)[:, None]
    kr = np.arange(NA_KROWS)[None, :]
    masked = 2 * NA_WIN_ROWS - 1
    cases = []
    for case in range(3):
        krel = kr - 4 * case
        r0 = qr - NA_WIN_ROWS // 2
        r0 = np.maximum(r0, 0) if case == 0 else (np.minimum(r0, 0) if case == 2 else r0)
        row_ok = (krel >= r0) & (krel < r0 + NA_WIN_ROWS)
        cases.append(np.where(row_ok, krel - qr + NA_WIN_ROWS - 1, masked))
    return cases


def _na_bias_kernel(lo_ref, hi_ref, o_ref):
    for case, ro in enumerate(_na_row_cases()):
        for qr in range(NA_QROWS):
            for j in range(NA_KROWS // 2):
                o_ref[case, 0, qr * GRID_W:(qr + 1) * GRID_W, j * LANES:(j + 1) * LANES] = (
                    lo_ref[0, int(ro[qr, 2 * j])] + hi_ref[0, int(ro[qr, 2 * j + 1])])


def _na_bias_tables(rpb):
    heads = rpb.shape[0]
    qc = np.arange(GRID_W)[:, None]
    kc = np.arange(GRID_W)[None, :]
    c0 = np.clip(qc - NA_WIN_COLS // 2, 0, GRID_W - NA_WIN_COLS)
    col_ok = (kc >= c0) & (kc < c0 + NA_WIN_COLS)
    co = np.clip(kc - qc + NA_WIN_COLS - 1, 0, 2 * NA_WIN_COLS - 2)
    pick_c = (co.reshape(1, -1) == np.arange(2 * NA_WIN_COLS - 1)[:, None]).astype(np.float32)
    by_col = jnp.einsum("hrc,cx->hrx", rpb.astype(F32) * LOG2E, jnp.asarray(pick_c),
                        precision=lax.Precision.HIGHEST).reshape(heads, -1, GRID_W, GRID_W)
    blocks = jnp.where(jnp.asarray(col_ok)[None, None], by_col, NEG_INF)
    blocks = jnp.concatenate([blocks, jnp.full((heads, 1, GRID_W, GRID_W), NEG_INF, F32)], axis=1)
    zeros = jnp.zeros_like(blocks)
    lo = jnp.concatenate([blocks, zeros], axis=-1)
    hi = jnp.concatenate([zeros, blocks], axis=-1)
    n_off = blocks.shape[1]
    blk = pl.BlockSpec((1, n_off, GRID_W, LANES), lambda h: (h, 0, 0, 0))
    return pl.pallas_call(
        _na_bias_kernel,
        grid=(heads,),
        in_specs=[blk, blk],
        out_specs=pl.BlockSpec((3, 1, NA_QROWS * GRID_W, NA_KROWS * GRID_W), lambda h: (0, h, 0, 0)),
        out_shape=jax.ShapeDtypeStruct((3, heads, NA_QROWS * GRID_W, NA_KROWS * GRID_W), F32),
        compiler_params=_cparams(("parallel",), VMEM_LIMIT),
        name="na_bias",
    )(lo, hi)


def _na_kernel(q_ref, k_ref, v_ref, kc_ref, vc_ref, bias_ref, o_ref, *, rows, blocks, bounded):
    nkeys = NA_KROWS * GRID_W
    tq = NA_QROWS * GRID_W
    for sub in range(blocks):
        rb = pl.program_id(2) * blocks + sub
        kr0 = jnp.clip(rb * NA_QROWS - NA_WIN_ROWS // 2, 0, rows - NA_KROWS)
        case = lax.shift_right_logical(rb * NA_QROWS - kr0, 2)
        start = pl.multiple_of(kr0 * GRID_W, GRID_W)
        accs = []
        for hh in range(2):
            q = q_ref[0, hh, sub * tq:(sub + 1) * tq, :]
            kb = k_ref[0, hh, pl.ds(start, nkeys), :]
            vb = v_ref[0, hh, pl.ds(start, nkeys), :]
            s = lax.dot_general(q, kb, _DN_T, preferred_element_type=F32) + bias_ref[case, hh]
            sc = lax.dot_general(q, kc_ref[0, hh], _DN_T, preferred_element_type=F32)
            if not bounded:
                m = jnp.maximum(jnp.max(s, -1, keepdims=True), jnp.max(sc, -1, keepdims=True))
                s, sc = s - m, sc - m
            accs.append(jnp.dot(jnp.exp2(s).astype(BF16), vb, preferred_element_type=F32)
                        + jnp.dot(jnp.exp2(sc).astype(BF16), vc_ref[0, hh], preferred_element_type=F32))
        o_ref[0, sub * tq:(sub + 1) * tq, :] = _pair_output(accs).astype(o_ref.dtype)


def _na_latent(nq, nk, nv, nkc, nvc, bias, bounded):
    b, heads, n, _ = nq.shape
    rows = n // GRID_W
    assert rows >= NA_KROWS and rows % NA_QROWS == 0
    blocks = NA_BLOCKS_PER_STEP
    while (rows // NA_QROWS) % blocks:
        blocks //= 2
    tq = NA_QROWS * GRID_W * blocks
    nctx = nkc.shape[2]
    res = pl.BlockSpec((1, 2, n, LANES), lambda i, p, j: (i, p, 0, 0))
    ctx = pl.BlockSpec((1, 2, nctx, LANES), lambda i, p, j: (i, p, 0, 0))
    bias_spec = pl.BlockSpec((3, 2) + bias.shape[2:], lambda i, p, j: (0, p, 0, 0),
                             pipeline_mode=pl.Buffered(1))

    def call(flag):
        return pl.pallas_call(
            functools.partial(_na_kernel, rows=rows, blocks=blocks, bounded=flag),
            grid=(b, heads // 2, n // tq),
            in_specs=[pl.BlockSpec((1, 2, tq, LANES), lambda i, p, j: (i, p, j, 0)), res, res, ctx, ctx,
                      bias_spec],
            out_specs=pl.BlockSpec((1, tq, LANES), lambda i, p, j: (i, j, p)),
            out_shape=jax.ShapeDtypeStruct((b, n, heads * 64), BF16),
            compiler_params=_cparams(("parallel", "parallel", "parallel"), VMEM_LIMIT),
            name="na_latent" if flag else "na_latent_rowmax",
        )

    return lax.cond(bounded, call(True), call(False), nq, nk, nv, nkc, nvc, bias)


def _mix_residual(x_ref, o_refs, w_ref, ga_ref, g_ref, sh_ref, sc_ref):
    o = jnp.dot(jnp.concatenate([r[0] for r in o_refs], axis=1), w_ref[...], preferred_element_type=F32)
    x1 = x_ref[0] + ga_ref[0] * o
    h = (x1 * _rms(x1, x1.shape[-1])) * g_ref[...]
    return x1, h * (1.0 + sc_ref[0]) + sh_ref[0]


def _outproj_kernel(x_ref, oa_ref, of_ref, op_ref, on_ref, w_ref, ga_ref, g_ref, sh_ref, sc_ref,
                    wrh_ref, wrl_ref, x1_ref, h2_ref, rt_ref, *, n_exp):
    x1, h = _mix_residual(x_ref, (oa_ref, of_ref, op_ref, on_ref), w_ref, ga_ref, g_ref, sh_ref, sc_ref)
    x1_ref[0] = x1
    h2_ref[0] = h
    lane = lax.broadcasted_iota(jnp.int32, (1, LANES), 1)
    hh = h.astype(BF16)
    hl = (h - hh.astype(F32)).astype(BF16)
    lg = (jnp.dot(hh, wrh_ref[...], preferred_element_type=F32)
          + (jnp.dot(hl, wrh_ref[...], preferred_element_type=F32)
             + jnp.dot(hh, wrl_ref[...], preferred_element_type=F32)))
    lg = jnp.where(lane < n_exp, lg, NEG_INF)
    v1 = jnp.max(lg, -1, keepdims=True)
    i1 = jnp.min(jnp.where(lg == v1, lane, LANES), -1, keepdims=True)
    lg2 = jnp.where(lane == i1, NEG_INF, lg)
    v2 = jnp.max(lg2, -1, keepdims=True)
    i2 = jnp.min(jnp.where(lg2 == v2, lane, LANES), -1, keepdims=True)
    e2 = jnp.exp(v2 - v1)
    g1 = 1.0 / (1.0 + e2)
    g2 = e2 / (1.0 + e2)
    rt_ref[0] = jnp.where(lane == 0, i1.astype(F32),
                          jnp.where(lane == 1, i2.astype(F32),
                                    jnp.where(lane == 2, g1, jnp.where(lane == 3, g2, 0.0))))


def _mix_specs(x, oa, w_out, g, tm):
    b, n, d = x.shape
    tok = lambda w: pl.BlockSpec((1, tm, w), lambda i, j: (i, j, 0))
    per_b = pl.BlockSpec((1, 1, d), lambda i, j: (i, 0, 0))
    full = lambda a: pl.BlockSpec(a.shape, lambda i, j: (0,) * a.ndim, pipeline_mode=pl.Buffered(1))
    gw = oa.shape[-1]
    specs = [tok(d), tok(gw), tok(gw), tok(gw), tok(gw), full(w_out), per_b, full(g), per_b, per_b]
    return tok, per_b, full, specs


def _outproj(x, oa, of, op, on, w_out, ga, g, sh, sc, tm, w_router):
    b, n, d = x.shape
    tok, per_b, full, specs = _mix_specs(x, oa, w_out, g, tm)
    n_exp = w_router.shape[-1]
    wr = jnp.pad(w_router.astype(F32), ((0, 0), (0, LANES - n_exp)))
    wr_hi = wr.astype(BF16)
    wr_lo = (wr - wr_hi.astype(F32)).astype(BF16)
    return pl.pallas_call(
        functools.partial(_outproj_kernel, n_exp=n_exp),
        grid=(b, n // tm),
        in_specs=specs + [full(wr_hi), full(wr_lo)],
        out_specs=[tok(d), tok(d), tok(LANES)],
        out_shape=[jax.ShapeDtypeStruct((b, n, d), F32), jax.ShapeDtypeStruct((b, n, d), F32),
                   jax.ShapeDtypeStruct((b, n, LANES), F32)],
        compiler_params=_cparams(("parallel", "parallel"), VMEM_LIMIT),
        name="outproj",
    )(x, oa, of, op, on, w_out, ga, g, sh, sc, wr_hi, wr_lo)


def _outffn_kernel(x_ref, oa_ref, of_ref, op_ref, on_ref, w_ref, ga_ref, g_ref, sh_ref, sc_ref,
                   ga2_ref, w1_ref, w3_ref, w2_ref, o_ref, *, fc):
    x1, h = _mix_residual(x_ref, (oa_ref, of_ref, op_ref, on_ref), w_ref, ga_ref, g_ref, sh_ref, sc_ref)
    hb = h.astype(BF16)
    ff = w1_ref.shape[1]
    gated = []
    for j in range(ff // fc):
        a = jnp.dot(hb, w1_ref[:, j * fc:(j + 1) * fc], preferred_element_type=F32)
        bb = jnp.dot(hb, w3_ref[:, j * fc:(j + 1) * fc], preferred_element_type=F32)
        gated.append((_silu(a) * bb).astype(BF16))
    y = jnp.dot(jnp.concatenate(gated, axis=1), w2_ref[...], preferred_element_type=F32)
    o_ref[0] = x1 + ga2_ref[0] * y


def _outffn(x, oa, of, op, on, w_out, ga, g, sh, sc, ga2, w1, w3, w2, tm):
    b, n, d = x.shape
    ff = w1.shape[1]
    fc = 256 if ff % 256 == 0 else ff
    tok, per_b, full, specs = _mix_specs(x, oa, w_out, g, tm)
    return pl.pallas_call(
        functools.partial(_outffn_kernel, fc=fc),
        grid=(b, n // tm),
        in_specs=specs + [per_b, full(w1), full(w3), full(w2)],
        out_specs=tok(d),
        out_shape=jax.ShapeDtypeStruct((b, n, d), F32),
        compiler_params=_cparams(("parallel", "parallel"), VMEM_LIMIT),
        name="outproj_ffn",
    )(x, oa, of, op, on, w_out, ga, g, sh, sc, ga2, w1, w3, w2)


def _expert_kernel(te_ref, tv_ref, rows_ref, h_hbm, w1_ref, w3_ref, w2_ref, y_hbm, xs_ref, ys_ref,
                   gsem, ssem, zsem, *, fc, n_tiles, n_slots, n_dump_tiles):
    i = pl.program_id(0)
    tme = xs_ref.shape[1]
    nf = w1_ref.shape[2] // fc
    valid = tv_ref[i] > 0
    prev_valid = jnp.logical_and(i > 0, tv_ref[jnp.maximum(i - 1, 0)] > 0)
    slot = i % 2
    prev = jnp.maximum(i - 1, 0)

    def token_of(v):
        if n_slots & (n_slots - 1) == 0:
            return jnp.bitwise_and(v, n_slots - 1)
        return lax.rem(v, n_slots)

    def gather_row(tile, r, sl):
        pltpu.make_async_copy(h_hbm.at[token_of(rows_ref[tile * tme + r])], xs_ref.at[sl, r],
                              gsem.at[sl]).start()

    def scatter_row(r, sl):
        pltpu.make_async_copy(ys_ref.at[sl, r], y_hbm.at[rows_ref[prev * tme + r]], ssem.at[sl]).start()

    def wait_gather(sl):
        pltpu.make_async_copy(h_hbm.at[pl.ds(0, tme)], xs_ref.at[sl], gsem.at[sl]).wait()

    def wait_scatter(sl):
        pltpu.make_async_copy(ys_ref.at[sl], y_hbm.at[pl.ds(0, tme)], ssem.at[sl]).wait()

    @pl.when(i == 0)
    def _():
        ys_ref[1] = jnp.zeros(ys_ref.shape[1:], F32)
        fills = [pltpu.make_async_copy(ys_ref.at[1], y_hbm.at[pl.ds(2 * n_slots + e * tme, tme)], zsem)
                 for e in range(n_dump_tiles)]
        for cp in fills:
            cp.start()
        for cp in fills:
            cp.wait()

        def issue(r, c):
            gather_row(0, r, 0)
            return c

        lax.fori_loop(0, tme, issue, 0, unroll=8)

    @pl.when(jnp.logical_or(i == 0, prev_valid))
    def _():
        wait_gather(slot)

    @pl.when(prev_valid)
    def _():
        wait_scatter(slot)

    @pl.when(valid)
    def _():
        nxt = jnp.minimum(i + 1, n_tiles - 1)
        xb = xs_ref[slot].astype(BF16)
        cuts = [(k * tme) // nf for k in range(nf + 1)]
        gated = []
        for j in range(nf):
            for r in range(cuts[j], cuts[j + 1]):
                gather_row(nxt, r, 1 - slot)
                scatter_row(r, 1 - slot)
            a = jnp.dot(xb, w1_ref[0, :, j * fc:(j + 1) * fc], preferred_element_type=F32)
            bb = jnp.dot(xb, w3_ref[0, :, j * fc:(j + 1) * fc], preferred_element_type=F32)
            gated.append((_silu(a) * bb).astype(BF16))
        ys_ref[slot] = jnp.dot(jnp.concatenate(gated, axis=1), w2_ref[0], preferred_element_type=F32)

    @pl.when(jnp.logical_and(jnp.logical_not(valid), prev_valid))
    def _():
        def issue(r, c):
            scatter_row(r, 1 - slot)
            return c

        lax.fori_loop(0, tme, issue, 0, unroll=8)
        wait_scatter(1 - slot)


def _combine_kernel(x_ref, y0_ref, y1_ref, ga_ref, rt_ref, o_ref):
    rt = rt_ref[0]
    y = rt[:, TOP_K:TOP_K + 1] * y0_ref[...] + rt[:, TOP_K + 1:TOP_K + 2] * y1_ref[...]
    o_ref[0] = x_ref[0] + ga_ref[0] * y


def _moe(h2, x1, ga, rt, w1, w3, w2, tme, tc):
    b, n_b, d = h2.shape
    n = b * n_b
    n_exp, _, ff = w1.shape
    fc = 512 if ff % 512 == 0 else ff
    e_idx = rt[..., 0:TOP_K].astype(jnp.int32).reshape(-1)
    onehot = (e_idx[:, None] == jnp.arange(n_exp)[None, :]).astype(jnp.int32)
    pos = jnp.take_along_axis(jnp.cumsum(onehot, axis=0) - onehot, e_idx[:, None], axis=1)[:, 0]
    counts = jnp.sum(onehot, axis=0)
    padded = ((counts + tme - 1) // tme) * tme
    ends = jnp.cumsum(padded)
    offs = ends - padded
    dest = offs[e_idx] + pos
    n_rows = n * TOP_K + n_exp * tme
    n_tiles = n_rows // tme
    tile_start = jnp.arange(n_tiles + 1, dtype=jnp.int32) * tme
    tvalid = (tile_start < ends[-1]).astype(jnp.int32)
    texp = jnp.minimum(jnp.sum((tile_start[:, None] >= ends[None, :]).astype(jnp.int32), axis=1), n_exp - 1)
    texp = jnp.where(tvalid > 0, texp, texp[jnp.maximum(ends[-1] // tme - 1, 0)])
    flat = jnp.arange(n * TOP_K, dtype=jnp.int32)
    row_e = jnp.repeat(texp[:n_tiles], tme)
    first_pad = jnp.sum(jnp.where(row_e[:, None] == jnp.arange(n_exp)[None, :], (offs + counts)[None, :], 0),
                        axis=1)
    pad_idx = jnp.clip(jnp.arange(n_rows, dtype=jnp.int32) - first_pad, 0, tme - 1)
    rows = (2 * n + row_e * tme + pad_idx).astype(jnp.int32).at[dest].set((flat % TOP_K) * n + flat // TOP_K)
    wspec = lambda shape: pl.BlockSpec((1,) + shape, lambda i, te, tv, s: (te[i], 0, 0),
                                       pipeline_mode=pl.Buffered(1))
    ys = pl.pallas_call(
        functools.partial(_expert_kernel, fc=fc, n_tiles=n_tiles, n_slots=n, n_dump_tiles=n_exp),
        grid_spec=pltpu.PrefetchScalarGridSpec(
            num_scalar_prefetch=3,
            grid=(n_tiles + 1,),
            in_specs=[pl.BlockSpec(memory_space=pl.ANY), wspec((d, ff)), wspec((d, ff)), wspec((ff, d))],
            out_specs=pl.BlockSpec(memory_space=pl.ANY),
            scratch_shapes=[pltpu.VMEM((2, tme, d), F32), pltpu.VMEM((2, tme, d), F32),
                            pltpu.SemaphoreType.DMA((2,)), pltpu.SemaphoreType.DMA((2,)),
                            pltpu.SemaphoreType.DMA(())]),
        out_shape=jax.ShapeDtypeStruct((2 * n + n_exp * tme, d), F32),
        compiler_params=_cparams(("arbitrary",), VMEM_LIMIT),
        name="experts",
    )(texp, tvalid, rows, h2.reshape(n, d), w1, w3, w2)

    nb = n_b // tc
    tok = lambda w: pl.BlockSpec((1, tc, w), lambda i, j: (i, j, 0))
    return pl.pallas_call(
        _combine_kernel,
        grid=(b, nb),
        in_specs=[tok(d),
                  pl.BlockSpec((tc, d), lambda i, j: (i * nb + j, 0)),
                  pl.BlockSpec((tc, d), lambda i, j: (n // tc + i * nb + j, 0)),
                  pl.BlockSpec((1, 1, d), lambda i, j: (i, 0, 0)), tok(LANES)],
        out_specs=tok(d),
        out_shape=jax.ShapeDtypeStruct((b, n_b, d), F32),
        compiler_params=_cparams(("parallel", "parallel"), VMEM_LIMIT),
        name="moe_combine",
    )(x1, ys, ys, ga, rt)


def _pick_tile(n, pref):
    t = pref
    while n % t:
        t //= 2
    return t


def kernel(x, c, ctx, c_ctx, w_ada, b_ada, g_mix, g_ffn, w_in, w_out, g_cq, g_ckv, w_uq, w_ukv,
           g_mla_qn, g_mla_qr, g_mla_kn, g_mla_kr, w_fourier, w_pool, pool_scale, g_na_q, g_na_k,
           na_rpb, w1_dense, w3_dense, w2_dense, w_router, w1_moe, w3_moe, w2_moe):
    b, n, d = x.shape
    nctx = ctx.shape[1]
    depth = w_ada.shape[0]
    assert b <= 7 and n % GRID_W == 0

    cc = jnp.zeros((8, d), F32).at[:b].set(c).at[b].set(c_ctx)
    mods = _adaln(cc, w_ada, b_ada)
    cos_x, sin_x = _rope_tables(n)
    cos_c = jnp.ones((nctx, LANES), F32)
    sin_c = jnp.zeros((nctx, LANES), F32)
    sw = _rope_swap_perm()
    tm_x = _pick_tile(n, 512)
    tm_c = _pick_tile(nctx, 256)
    amax = lambda g: jnp.max(jnp.abs(g.astype(F32)))

    for l in range(depth):
        last = l == depth - 1
        mx = mods[l, :b].reshape(b, 1, 6 * d)
        mc = jnp.broadcast_to(mods[l, b].reshape(1, 1, 6 * d), (b, 1, 6 * d))
        part = lambda m, k: m[:, :, k * d:(k + 1) * d]

        bound_mla = (MLA_SCALE * LOG2E) * (MLA_NOPE * amax(g_mla_qn[l]) * amax(g_mla_kn[l])
                                           + MLA_ROPE * amax(g_mla_qr[l]) * amax(g_mla_kr[l]))
        qk_na = (NA_SCALE * LOG2E) * NA_HEAD_DIM * amax(g_na_q[l]) * amax(g_na_k[l])
        rpb_hi = jnp.maximum(jnp.max(na_rpb[l]).astype(F32), 0.0) * LOG2E
        rpb_lo = jnp.minimum(jnp.min(na_rpb[l]).astype(F32), 0.0) * LOG2E
        bound_na = qk_na + rpb_hi
        mla_bounded = 2.0 * bound_mla <= MAX_SOFTMAX_GAP_LOG2
        na_bounded = 2.0 * qk_na + rpb_hi - rpb_lo <= MAX_SOFTMAX_GAP_LOG2
        bounds = jnp.zeros((1, LANES), F32).at[0, 0].set(bound_mla).at[0, 1].set(bound_na)

        wts = [
            _prep_in_weights(w_in[l]),
            g_cq[l].reshape(1, -1), _prep_uq(w_uq[l]),
            _lane_vec([(g_mla_qn[l], 64), (g_mla_qr[l], 32), (None, 32)]),
            _lane_vec([(None, 64), (g_mla_qr[l][sw], 32), (None, 32)]),
            g_ckv[l].reshape(1, -1), _prep_ukv(w_ukv[l]),
            _lane_vec([(g_mla_kn[l], 64), (None, 64)]),
            _lane_vec([(None, 64), (g_mla_kr[l], 32), (None, 32)]),
            _lane_vec([(None, 64), (g_mla_kr[l][sw], 32), (None, 32)]),
            jnp.tile(g_na_q[l], NA_HEADS).reshape(1, -1), jnp.tile(g_na_k[l], NA_HEADS).reshape(1, -1),
        ]
        gmix = g_mix[l].reshape(1, d)
        qx, kx, vx, fx, px, nqx, nkx, nvx = _inproj(x, part(mx, 0), part(mx, 1), gmix, cos_x, sin_x,
                                                    bounds, wts, tm_x)
        qc, kc, vc, fc_, pc, nqc, nkc, nvc = _inproj(ctx, part(mc, 0), part(mc, 1), gmix, cos_c, sin_c,
                                                     bounds, wts, tm_c)

        wf = w_fourier[l].astype(BF16)
        cg = w_pool.shape[-1]
        wp_bd = jnp.zeros((len(POOL_WINDOWS) * cg,) * 2, F32)
        for gi in range(len(POOL_WINDOWS)):
            wp_bd = wp_bd.at[gi * cg:(gi + 1) * cg, gi * cg:(gi + 1) * cg].set(w_pool[l, gi])
        wp_bd = wp_bd.astype(BF16)
        ps = pool_scale[l].reshape(1, -1)
        w_out_l = w_out[l].astype(BF16)
        gffn = g_ffn[l].reshape(1, d)

        o_mla = _flash_pairs(qx, [(kx, vx), (kc, vc)], _pick_tile(n, 256), mla_bounded)
        o_f = _fourier_latent(fx, wf)
        o_p = _pool(px, wp_bd, ps, _pick_tile(n, 512))
        o_na = _na_latent(nqx, nkx, nvx, nkc, nvc, _na_bias_tables(na_rpb[l]), na_bounded)

        moe_layer = l % 2 == 1
        i = l // 2
        if moe_layer:
            w1, w3, w2 = w1_moe[i].astype(BF16), w3_moe[i].astype(BF16), w2_moe[i].astype(BF16)
            x1, h2, rt = _outproj(x, o_mla, o_f, o_p, o_na, w_out_l, part(mx, 2), gffn, part(mx, 3),
                                  part(mx, 4), _pick_tile(n, 256), w_router[i])
            x = _moe(h2, x1, part(mx, 5), rt, w1, w3, w2, tme=512, tc=256)
        else:
            w1, w3, w2 = w1_dense[i].astype(BF16), w3_dense[i].astype(BF16), w2_dense[i].astype(BF16)
            x = _outffn(x, o_mla, o_f, o_p, o_na, w_out_l, part(mx, 2), gffn, part(mx, 3), part(mx, 4),
                        part(mx, 5), w1, w3, w2, tm_x)

        if not last:
            oc_mla = _flash_pairs(qc, [(kc, vc)], tm_c, mla_bounded)
            oc_f = _fourier_dense(fc_, wf)
            oc_p = _pool(pc, wp_bd, ps, tm_c)
            oc_na = _flash_pairs(nqc, [(nkc, nvc)], tm_c, na_bounded)
            if moe_layer:
                c1, hc2, rtc = _outproj(ctx, oc_mla, oc_f, oc_p, oc_na, w_out_l, part(mc, 2), gffn,
                                        part(mc, 3), part(mc, 4), tm_c, w_router[i])
                ctx = _moe(hc2, c1, part(mc, 5), rtc, w1, w3, w2, tme=256, tc=256)
            else:
                ctx = _outffn(ctx, oc_mla, oc_f, oc_p, oc_na, w_out_l, part(mc, 2), gffn, part(mc, 3),
                              part(mc, 4), part(mc, 5), w1, w3, w2, tm_c)
    return x
```

```python
import functools
import math

import numpy as np
import jax
import jax.numpy as jnp
from jax import lax
from jax.experimental import pallas as pl
from jax.experimental.pallas import tpu as pltpu

F32 = jnp.float32
BF16 = jnp.bfloat16

GRID_W = 64
LANES = 128
EPS = 1e-6
NEG_INF = -1e30

MLA_HEADS = 4
MLA_NOPE = 64
MLA_ROPE = 32
MLA_V = 64
MLA_SCALE = (MLA_NOPE + MLA_ROPE) ** -0.5
MLA_SPARE_LANE = MLA_NOPE + MLA_ROPE
ROPE_BASE = 10000.0
LOG2E = 1.0 / math.log(2.0)
MAX_SOFTMAX_GAP_LOG2 = 120.0

FOURIER_GROUPS = 4
POOL_WINDOWS = (2, 4, 8, 16)
POOL_HALO = 8

NA_HEADS = 4
NA_HEAD_DIM = 64
NA_SCALE = NA_HEAD_DIM ** -0.5
NA_WIN_ROWS = 8
NA_WIN_COLS = 16
NA_QROWS = 8
NA_KROWS = 16
NA_BLOCKS_PER_STEP = 8

TOP_K = 2
VMEM_LIMIT = 56 * 1024 * 1024


def _cparams(sem, vmem=None):
    return pltpu.CompilerParams(dimension_semantics=sem, vmem_limit_bytes=vmem)


def _silu(a):
    return a / (1.0 + jnp.exp(-a))


def _rms(v, n):
    return lax.rsqrt(jnp.sum(v * v, axis=-1, keepdims=True) * (1.0 / n) + EPS)


def _adaln_kernel(c_ref, w_ref, b_ref, o_ref):
    o_ref[0] = jnp.dot(_silu(c_ref[...]), w_ref[0], preferred_element_type=F32) + b_ref[0]


def _adaln(cc, w_ada, b_ada):
    depth, d, d6 = w_ada.shape
    tn = 512
    return pl.pallas_call(
        _adaln_kernel,
        grid=(depth, d6 // tn),
        in_specs=[pl.BlockSpec((8, d), lambda l, j: (0, 0)),
                  pl.BlockSpec((1, d, tn), lambda l, j: (l, 0, j)),
                  pl.BlockSpec((1, 1, tn), lambda l, j: (l, 0, j))],
        out_specs=pl.BlockSpec((1, 8, tn), lambda l, j: (l, 0, j)),
        out_shape=jax.ShapeDtypeStruct((depth, 8, d6), F32),
        compiler_params=_cparams(("arbitrary", "arbitrary")),
        name="adaln",
    )(cc, w_ada, b_ada.reshape(depth, 1, d6))


IN_COLS = 1920


def _rope_swap_perm():
    j = np.arange(MLA_ROPE)
    return np.where((j % 16) < 8, j + 8, j - 8)


def _prep_in_weights(w_in_l):
    d = w_in_l.shape[0]
    s = [0, 256, 384, 416, 672, 928, 1184, 1440, 1696]
    cq, ckv, kr, f, p, nq, nk, nv = [w_in_l[:, s[i]:s[i + 1]] for i in range(8)]
    z = lambda n: jnp.zeros((d, n), w_in_l.dtype)
    krsw = kr[:, _rope_swap_perm()]
    return jnp.concatenate([cq, ckv, f, p, nq, nk, nv, z(64), kr, z(32), z(64), krsw, z(32)],
                           axis=1).astype(BF16)


def _prep_uq(w_uq_l):
    r = w_uq_l.shape[0]
    w = w_uq_l.reshape(r, MLA_HEADS, MLA_NOPE + MLA_ROPE)
    z = lambda n: jnp.zeros((r, MLA_HEADS, n), w.dtype)
    main = jnp.concatenate([w, z(32)], axis=-1)
    sw = jnp.concatenate([z(64), w[..., MLA_NOPE:][..., _rope_swap_perm()], z(32)], axis=-1)
    return jnp.concatenate([main.reshape(r, -1), sw.reshape(r, -1)], axis=1).astype(BF16)


def _prep_ukv(w_ukv_l):
    r = w_ukv_l.shape[0]
    w = w_ukv_l.reshape(r, MLA_HEADS, MLA_NOPE + MLA_V)
    z = jnp.zeros((r, 64), w.dtype)
    tiles = [jnp.concatenate([w[:, h, :MLA_NOPE], z], axis=1) for h in range(MLA_HEADS)]
    for h in range(MLA_HEADS):
        v = w[:, h, MLA_NOPE:]
        tiles.append(jnp.concatenate([v, z] if h % 2 == 0 else [z, v], axis=1))
    return jnp.concatenate(tiles, axis=1).astype(BF16)


def _lane_vec(parts):
    cols = [jnp.zeros((n,), F32) if a is None else a.astype(F32) for a, n in parts]
    return jnp.concatenate(cols).reshape(1, -1)


def _rope_tables(n_tokens):
    t = np.arange(n_tokens)
    pos = np.stack([t // GRID_W, t % GRID_W], axis=1).astype(np.float64)
    half = MLA_ROPE // 2
    inv = 1.0 / (ROPE_BASE ** (np.arange(0, half, 2, dtype=np.float64) / half))
    j = np.arange(MLA_ROPE)
    ang = pos[:, j // 16] * inv[j % 8]
    sign = np.where((j % 16) < 8, -1.0, 1.0)
    cos = np.ones((n_tokens, LANES), np.float32)
    sin = np.zeros((n_tokens, LANES), np.float32)
    cos[:, 64:96] = np.cos(ang)
    sin[:, 64:96] = np.sin(ang) * sign
    return jnp.asarray(cos), jnp.asarray(sin)


def _inproj_kernel(x_ref, sh_ref, sc_ref, g_ref, cos_ref, sin_ref, bound_ref, win_ref,
                   gcq_ref, wuq_ref, gq_ref, gqsw_ref,
                   gckv_ref, wukv_ref, gkn_ref, gkr_ref, gkrsw_ref, gnq_ref, gnk_ref,
                   q_ref, k_ref, v_ref, f_ref, p_ref, nq_ref, nk_ref, nv_ref):
    x = x_ref[0]
    d = x.shape[-1]
    h = (x * _rms(x, d)) * g_ref[...]
    h = h * (1.0 + sc_ref[0]) + sh_ref[0]
    u = jnp.dot(h.astype(BF16), win_ref[...], preferred_element_type=F32)
    cq, ckv = u[:, 0:256], u[:, 256:384]
    f_ref[0] = u[:, 384:640].astype(BF16)
    p_ref[0] = u[:, 640:896]
    nq, nk, nv = u[:, 896:1152], u[:, 1152:1408], u[:, 1408:1664]
    krb, krs = u[:, 1664:1792], u[:, 1792:1920]
    cos, sin = cos_ref[...], sin_ref[...]
    lane = lax.broadcasted_iota(jnp.int32, (1, LANES), 1)
    m_nope = lane < MLA_NOPE
    m_rope = jnp.logical_and(lane >= MLA_NOPE, lane < MLA_NOPE + MLA_ROPE)
    bound_mla, bound_na = bound_ref[:, 0:1], bound_ref[:, 1:2]

    cqn = (cq * _rms(cq, 256) * gcq_ref[...]).astype(BF16)
    qall = jnp.dot(cqn, wuq_ref[...], preferred_element_type=F32)
    for hd in range(MLA_HEADS):
        blk = qall[:, hd * LANES:(hd + 1) * LANES]
        sw = qall[:, (MLA_HEADS + hd) * LANES:(MLA_HEADS + hd + 1) * LANES]
        sq = blk * blk
        rn = lax.rsqrt(jnp.sum(jnp.where(m_nope, sq, 0.0), -1, keepdims=True) * (1.0 / MLA_NOPE) + EPS)
        rr = lax.rsqrt(jnp.sum(jnp.where(m_rope, sq, 0.0), -1, keepdims=True) * (1.0 / MLA_ROPE) + EPS)
        qh = blk * jnp.where(m_nope, rn, rr) * gq_ref[...]
        qs = sw * rr * gqsw_ref[...]
        q_ref[0, hd] = jnp.where(lane == MLA_SPARE_LANE, -bound_mla,
                                 (qh * cos + qs * sin) * (MLA_SCALE * LOG2E)).astype(BF16)

    ckvn = (ckv * _rms(ckv, 128) * gckv_ref[...]).astype(BF16)
    kvall = jnp.dot(ckvn, wukv_ref[...], preferred_element_type=F32)
    rkr = _rms(krb, MLA_ROPE)
    krot = (krb * rkr * gkr_ref[...]) * cos + (krs * rkr * gkrsw_ref[...]) * sin
    for hd in range(MLA_HEADS):
        blk = kvall[:, hd * LANES:(hd + 1) * LANES]
        k_ref[0, hd] = jnp.where(lane == MLA_SPARE_LANE, 1.0,
                                 blk * _rms(blk, MLA_NOPE) * gkn_ref[...] + krot).astype(BF16)
        vb = kvall[:, (MLA_HEADS + hd) * LANES:(MLA_HEADS + hd + 1) * LANES]
        one_lane = 64 if hd % 2 == 0 else 0
        v_ref[0, hd] = jnp.where(lane == one_lane, 1.0, vb).astype(BF16)

    lane2 = lax.broadcasted_iota(jnp.int32, (1, 2 * LANES), 1)

    def seg_rms(t):
        sq = t * t
        r = jnp.zeros_like(t)
        for s in range(NA_HEADS):
            m = (lane2 // NA_HEAD_DIM) == s
            ss = jnp.sum(jnp.where(m, sq, 0.0), -1, keepdims=True)
            r = jnp.where(m, lax.rsqrt(ss * (1.0 / NA_HEAD_DIM) + EPS), r)
        return r

    nqn = nq * seg_rms(nq) * gnq_ref[...] * (NA_SCALE * LOG2E)
    nkn = nk * seg_rms(nk) * gnk_ref[...]
    for hd in range(NA_HEADS):
        t0 = (hd // 2) * LANES
        own = (lane < 64) if hd % 2 == 0 else (lane >= 64)
        one_lane = 64 if hd % 2 == 0 else 0
        spare = lane == (LANES - 1 if hd % 2 == 0 else 0)
        nq_ref[0, hd] = jnp.where(own, nqn[:, t0:t0 + LANES], jnp.where(spare, -bound_na, 0.0)).astype(BF16)
        nk_ref[0, hd] = jnp.where(own, nkn[:, t0:t0 + LANES], jnp.where(spare, 1.0, 0.0)).astype(BF16)
        nv_ref[0, hd] = jnp.where(own, nv[:, t0:t0 + LANES],
                                  jnp.where(lane == one_lane, 1.0, 0.0)).astype(BF16)


def _inproj(x, sh, sc, g, cos, sin, bounds, wts, tm):
    b, n, d = x.shape
    heads = MLA_HEADS
    row = lambda a: pl.BlockSpec(a.shape, lambda i, j: (0,) * a.ndim)
    per_b = pl.BlockSpec((1, 1, d), lambda i, j: (i, 0, 0))
    tok = lambda w: pl.BlockSpec((1, tm, w), lambda i, j: (i, j, 0))
    hd_spec = pl.BlockSpec((1, heads, tm, LANES), lambda i, j: (i, 0, j, 0))
    tab = pl.BlockSpec((tm, LANES), lambda i, j: (j, 0))
    hshape = jax.ShapeDtypeStruct((b, heads, n, LANES), BF16)
    return pl.pallas_call(
        _inproj_kernel,
        grid=(b, n // tm),
        in_specs=[tok(d), per_b, per_b, row(g), tab, tab, row(bounds)] + [row(w) for w in wts],
        out_specs=[hd_spec, hd_spec, hd_spec, tok(256), tok(256), hd_spec, hd_spec, hd_spec],
        out_shape=[hshape, hshape, hshape,
                   jax.ShapeDtypeStruct((b, n, 256), BF16), jax.ShapeDtypeStruct((b, n, 256), F32),
                   hshape, hshape, hshape],
        compiler_params=_cparams(("parallel", "parallel"), VMEM_LIMIT),
        name="inproj",
    )(x, sh, sc, g, cos, sin, bounds, *wts)


_DN_T = (((1,), (1,)), ((), ()))


def _pair_output(accs):
    lane = lax.broadcasted_iota(jnp.int32, (1, LANES), 1)
    out = None
    for hh, acc in enumerate(accs):
        one_lane = 64 if hh == 0 else 0
        own = (lane < 64) if hh == 0 else (lane >= 64)
        o = jnp.where(own, acc / acc[:, one_lane:one_lane + 1], 0.0)
        out = o if out is None else out + o
    return out


def _flash_pair_kernel(q_ref, *refs, chunks, bounded):
    o_ref = refs[-1]
    tq = q_ref.shape[2]
    accs = []
    for hh in range(2):
        q = q_ref[0, hh]
        acc = jnp.zeros((tq, LANES), F32)
        m = jnp.full((tq, 1), NEG_INF, F32)
        for kset, start, size in chunks:
            ks = refs[2 * kset][0, hh, start:start + size, :]
            vs = refs[2 * kset + 1][0, hh, start:start + size, :]
            s = lax.dot_general(q, ks, _DN_T, preferred_element_type=F32)
            if bounded:
                acc = acc + jnp.dot(jnp.exp2(s).astype(BF16), vs, preferred_element_type=F32)
            else:
                m_new = jnp.maximum(m, jnp.max(s, axis=-1, keepdims=True))
                p = jnp.exp2(s - m_new).astype(BF16)
                acc = acc * jnp.exp2(m - m_new) + jnp.dot(p, vs, preferred_element_type=F32)
                m = m_new
        accs.append(acc)
    o_ref[0] = _pair_output(accs).astype(o_ref.dtype)


def _flash_pairs(q, kv_sets, tq, bounded, tk=2048):
    b, heads, lq, _ = q.shape
    chunks, ops, kv_specs = [], [], []
    for si, (k, v) in enumerate(kv_sets):
        lk = k.shape[2]
        step = tk if lk % tk == 0 else lk
        chunks += [(si, st, step) for st in range(0, lk, step)]
        ops += [k, v]
        kv_specs += [pl.BlockSpec((1, 2, lk, LANES), lambda i, p, j: (i, p, 0, 0))] * 2

    def call(flag):
        return pl.pallas_call(
            functools.partial(_flash_pair_kernel, chunks=tuple(chunks), bounded=flag),
            grid=(b, heads // 2, lq // tq),
            in_specs=[pl.BlockSpec((1, 2, tq, LANES), lambda i, p, j: (i, p, j, 0))] + kv_specs,
            out_specs=pl.BlockSpec((1, tq, LANES), lambda i, p, j: (i, j, p)),
            out_shape=jax.ShapeDtypeStruct((b, lq, heads * 64), BF16),
            compiler_params=_cparams(("parallel", "parallel", "parallel"), VMEM_LIMIT),
            name="flash_pairs" if flag else "flash_pairs_online",
        )

    return lax.cond(bounded, call(True), call(False), q, *ops)


def _table(a):
    return jnp.asarray(a, F32).astype(BF16)


def _dft_consts(n_tokens):
    rows = n_tokens // GRID_W
    r = np.arange(rows)
    ang_r = 2 * np.pi * np.outer(r, r) / rows
    a_mat = np.concatenate([np.cos(ang_r), -np.sin(ang_r)], axis=0)
    c = np.arange(GRID_W)
    ang_t = 2 * np.pi * np.outer(r, c) / n_tokens
    ang_c = 2 * np.pi * np.outer(c, c) / GRID_W
    cc, sc = np.cos(ang_c), np.sin(ang_c)
    m_b = np.block([[cc, sc], [-sc, cc]])
    return a_mat, np.cos(ang_t), np.sin(ang_t), m_b


def _channel_dft(n_tokens, width):
    cg = width // FOURIER_GROUPS
    j = np.arange(cg)
    ang = 2 * np.pi * np.outer(j, j) / cg
    eye = np.eye(FOURIER_GROUPS)
    scale = 1.0 / math.sqrt(n_tokens * cg)
    return np.kron(eye, np.cos(ang)) * scale, np.kron(eye, np.sin(ang)) * scale


def _fourier_rows_kernel(u_ref, a_ref, tc_ref, ts_ref, o_ref):
    z = jnp.dot(a_ref[...], u_ref[0], preferred_element_type=F32)
    r = z.shape[0] // 2
    zr, zi = z[:r], z[r:]
    c, s = tc_ref[...], ts_ref[...]
    o_ref[0, :r] = (zr * c + zi * s).astype(BF16)
    o_ref[0, r:] = (zi * c - zr * s).astype(BF16)


def _fourier_cols_kernel(z_ref, mb_ref, cs_ref, wf_ref, o_ref):
    kb = z_ref.shape[1]
    w = GRID_W
    cw = wf_ref.shape[0]
    ys = [jnp.dot(mb_ref[...], z_ref[0, t], preferred_element_type=F32) for t in range(kb)]
    yr = jnp.concatenate([y[:w] for y in ys], axis=0).astype(BF16)
    yi = jnp.concatenate([y[w:] for y in ys], axis=0).astype(BF16)
    f = (jnp.dot(yr, cs_ref[:cw], preferred_element_type=F32)
         + jnp.dot(yi, cs_ref[cw:], preferred_element_type=F32))
    out = jnp.dot(f.astype(BF16), wf_ref[...], preferred_element_type=F32)
    for t in range(kb):
        o_ref[0, t] = out[t * w:(t + 1) * w].astype(o_ref.dtype)


def _fourier_latent(uf, wf):
    b, n, cw = uf.shape
    rows = n // GRID_W
    a_mat, tw_c, tw_s, m_b = _dft_consts(n)
    cc, sc = _channel_dft(n, cw)
    wide = GRID_W * cw
    tn = 2048
    expand = lambda t: jnp.broadcast_to(jnp.asarray(t, F32)[:, :, None], (rows, GRID_W, cw)).reshape(rows, wide)
    z = pl.pallas_call(
        _fourier_rows_kernel,
        grid=(wide // tn, b),
        in_specs=[pl.BlockSpec((1, rows, tn), lambda j, i: (i, 0, j)),
                  pl.BlockSpec((2 * rows, rows), lambda j, i: (0, 0)),
                  pl.BlockSpec((rows, tn), lambda j, i: (0, j)),
                  pl.BlockSpec((rows, tn), lambda j, i: (0, j))],
        out_specs=pl.BlockSpec((1, 2 * rows, tn), lambda j, i: (i, 0, j)),
        out_shape=jax.ShapeDtypeStruct((b, 2 * rows, wide), BF16),
        compiler_params=_cparams(("parallel", "parallel")),
        name="fourier_rows",
    )(uf.reshape(b, rows, wide), _table(a_mat), expand(tw_c), expand(tw_s))
    z = z.reshape(b, 2, rows, GRID_W, cw).transpose(0, 2, 1, 3, 4).reshape(b, rows, 2 * GRID_W, cw)
    kb = 16
    y = pl.pallas_call(
        _fourier_cols_kernel,
        grid=(b, rows // kb),
        in_specs=[pl.BlockSpec((1, kb, 2 * GRID_W, cw), lambda i, j: (i, j, 0, 0)),
                  pl.BlockSpec((2 * GRID_W, 2 * GRID_W), lambda i, j: (0, 0)),
                  pl.BlockSpec((2 * cw, cw), lambda i, j: (0, 0)),
                  pl.BlockSpec((cw, cw), lambda i, j: (0, 0))],
        out_specs=pl.BlockSpec((1, kb, GRID_W, cw), lambda i, j: (i, j, 0, 0)),
        out_shape=jax.ShapeDtypeStruct((b, rows, GRID_W, cw), BF16),
        compiler_params=_cparams(("parallel", "parallel")),
        name="fourier_cols",
    )(z, _table(m_b), _table(np.concatenate([cc, sc], axis=0)), wf)
    return y.transpose(0, 2, 1, 3).reshape(b, n, cw)


def _fourier_dense_kernel(u_ref, cl_ref, sl_ref, cs_ref, wf_ref, o_ref):
    u = u_ref[0]
    cw = wf_ref.shape[0]
    a = jnp.dot(u, cs_ref[:cw], preferred_element_type=F32).astype(BF16)
    bb = jnp.dot(u, cs_ref[cw:], preferred_element_type=F32).astype(BF16)
    f = (jnp.dot(cl_ref[...], a, preferred_element_type=F32)
         - jnp.dot(sl_ref[...], bb, preferred_element_type=F32))
    o_ref[0] = jnp.dot(f.astype(BF16), wf_ref[...], preferred_element_type=F32).astype(o_ref.dtype)


def _fourier_dense(uf, wf):
    b, n, cw = uf.shape
    t = np.arange(n)
    ang = 2 * np.pi * np.outer(t, t) / n
    cc, sc = _channel_dft(n, cw)
    full = lambda shape: pl.BlockSpec(shape, lambda i: (0,) * len(shape))
    return pl.pallas_call(
        _fourier_dense_kernel,
        grid=(b,),
        in_specs=[pl.BlockSpec((1, n, cw), lambda i: (i, 0, 0)), full((n, n)), full((n, n)),
                  full((2 * cw, cw)), full((cw, cw))],
        out_specs=pl.BlockSpec((1, n, cw), lambda i: (i, 0, 0)),
        out_shape=jax.ShapeDtypeStruct((b, n, cw), BF16),
        compiler_params=_cparams(("parallel",)),
        name="fourier_dense",
    )(uf, _table(np.cos(ang)), _table(np.sin(ang)), _table(np.concatenate([cc, sc], axis=0)), wf)


def _pool_kernel(x_ref, prev_ref, next_ref, wp_ref, ps_ref, o_ref, pad_ref, s2_ref, s4_ref, s8_ref, *,
                 n_tokens):
    assert POOL_WINDOWS == (2, 4, 8, 16)
    i = pl.program_id(1)
    tm = x_ref.shape[1]
    h = POOL_HALO
    n = tm + 2 * h
    zeros = jnp.zeros((h, x_ref.shape[2]), F32)
    pad_ref[0:h] = zeros
    pad_ref[h:2 * h] = jnp.where(i > 0, prev_ref[0], 0.0)
    pad_ref[2 * h:2 * h + tm] = x_ref[0]
    pad_ref[2 * h + tm:3 * h + tm] = jnp.where(i < pl.num_programs(1) - 1, next_ref[0], 0.0)
    for ref in (s2_ref, s4_ref):
        ref[0:h] = zeros
        ref[h + n:2 * h + n] = zeros
    s2_ref[h:h + n] = pad_ref[h - 1:h - 1 + n] + pad_ref[h:h + n]
    s4_ref[h:h + n] = s2_ref[h - 1:h - 1 + n] + s2_ref[h + 1:h + 1 + n]
    s8_ref[h:h + n] = s4_ref[h - 2:h - 2 + n] + s4_ref[h + 2:h + 2 + n]
    lo = 2 * h
    x0 = pad_ref[lo:lo + tm]
    sums = [s2_ref[lo:lo + tm], s4_ref[lo:lo + tm], s8_ref[lo:lo + tm],
            s8_ref[lo - 4:lo - 4 + tm] + s8_ref[lo + 4:lo + 4 + tm]]
    t = i * tm + lax.broadcasted_iota(jnp.int32, (tm, 1), 0)
    grp = lax.broadcasted_iota(jnp.int32, (1, x0.shape[1]), 1) // (x0.shape[1] // len(POOL_WINDOWS))
    pooled = jnp.zeros_like(x0)
    for gi, w in enumerate(POOL_WINDOWS):
        cnt = (jnp.minimum(t + w // 2, n_tokens) - jnp.maximum(t - w // 2, 0)).astype(F32)
        pooled = jnp.where(grp == gi, sums[gi] * (1.0 / cnt), pooled)
    pooled = pooled - x0
    y = jnp.dot(pooled.astype(BF16), wp_ref[...], preferred_element_type=F32) * ps_ref[...]
    o_ref[0] = y.astype(o_ref.dtype)


def _pool(up, wp_bd, ps, tm):
    b, n, cw = up.shape
    h = POOL_HALO
    nblk = n // h
    return pl.pallas_call(
        functools.partial(_pool_kernel, n_tokens=n),
        grid=(b, n // tm),
        in_specs=[pl.BlockSpec((1, tm, cw), lambda i, j: (i, j, 0)),
                  pl.BlockSpec((1, h, cw), lambda i, j: (i, jnp.maximum(j * (tm // h) - 1, 0), 0)),
                  pl.BlockSpec((1, h, cw), lambda i, j: (i, jnp.minimum((j + 1) * (tm // h), nblk - 1), 0)),
                  pl.BlockSpec((cw, cw), lambda i, j: (0, 0)),
                  pl.BlockSpec((1, cw), lambda i, j: (0, 0))],
        out_specs=pl.BlockSpec((1, tm, cw), lambda i, j: (i, j, 0)),
        out_shape=jax.ShapeDtypeStruct((b, n, cw), BF16),
        scratch_shapes=[pltpu.VMEM((tm + 4 * h, cw), F32)] * 4,
        compiler_params=_cparams(("parallel", "parallel")),
        name="pool",
    )(up, up, up, wp_bd, ps)


def _na_row_cases():
    qr = np.arange(NA_QROWS)[:, None]
    kr = np.arange(NA_KROWS)[None, :]
    masked = 2 * NA_WIN_ROWS - 1
    cases = []
    for case in range(3):
        krel = kr - 4 * case
        r0 = qr - NA_WIN_ROWS // 2
        r0 = np.maximum(r0, 0) if case == 0 else (np.minimum(r0, 0) if case == 2 else r0)
        row_ok = (krel >= r0) & (krel < r0 + NA_WIN_ROWS)
        cases.append(np.where(row_ok, krel - qr + NA_WIN_ROWS - 1, masked))
    return cases


def _na_bias_kernel(lo_ref, hi_ref, o_ref):
    for case, ro in enumerate(_na_row_cases()):
        for qr in range(NA_QROWS):
            for j in range(NA_KROWS // 2):
                o_ref[case, 0, qr * GRID_W:(qr + 1) * GRID_W, j * LANES:(j + 1) * LANES] = (
                    lo_ref[0, int(ro[qr, 2 * j])] + hi_ref[0, int(ro[qr, 2 * j + 1])])


def _na_bias_tables(rpb):
    heads = rpb.shape[0]
    qc = np.arange(GRID_W)[:, None]
    kc = np.arange(GRID_W)[None, :]
    c0 = np.clip(qc - NA_WIN_COLS // 2, 0, GRID_W - NA_WIN_COLS)
    col_ok = (kc >= c0) & (kc < c0 + NA_WIN_COLS)
    co = np.clip(kc - qc + NA_WIN_COLS - 1, 0, 2 * NA_WIN_COLS - 2)
    pick_c = (co.reshape(1, -1) == np.arange(2 * NA_WIN_COLS - 1)[:, None]).astype(np.float32)
    by_col = jnp.einsum("hrc,cx->hrx", rpb.astype(F32) * LOG2E, jnp.asarray(pick_c),
                        precision=lax.Precision.HIGHEST).reshape(heads, -1, GRID_W, GRID_W)
    blocks = jnp.where(jnp.asarray(col_ok)[None, None], by_col, NEG_INF)
    blocks = jnp.concatenate([blocks, jnp.full((heads, 1, GRID_W, GRID_W), NEG_INF, F32)], axis=1)
    zeros = jnp.zeros_like(blocks)
    lo = jnp.concatenate([blocks, zeros], axis=-1)
    hi = jnp.concatenate([zeros, blocks], axis=-1)
    n_off = blocks.shape[1]
    blk = pl.BlockSpec((1, n_off, GRID_W, LANES), lambda h: (h, 0, 0, 0))
    return pl.pallas_call(
        _na_bias_kernel,
        grid=(heads,),
        in_specs=[blk, blk],
        out_specs=pl.BlockSpec((3, 1, NA_QROWS * GRID_W, NA_KROWS * GRID_W), lambda h: (0, h, 0, 0)),
        out_shape=jax.ShapeDtypeStruct((3, heads, NA_QROWS * GRID_W, NA_KROWS * GRID_W), F32),
        compiler_params=_cparams(("parallel",), VMEM_LIMIT),
        name="na_bias",
    )(lo, hi)


def _na_kernel(q_ref, k_ref, v_ref, kc_ref, vc_ref, bias_ref, o_ref, *, rows, blocks, bounded):
    nkeys = NA_KROWS * GRID_W
    tq = NA_QROWS * GRID_W
    for sub in range(blocks):
        rb = pl.program_id(2) * blocks + sub
        kr0 = jnp.clip(rb * NA_QROWS - NA_WIN_ROWS // 2, 0, rows - NA_KROWS)
        case = lax.shift_right_logical(rb * NA_QROWS - kr0, 2)
        start = pl.multiple_of(kr0 * GRID_W, GRID_W)
        accs = []
        for hh in range(2):
            q = q_ref[0, hh, sub * tq:(sub + 1) * tq, :]
            kb = k_ref[0, hh, pl.ds(start, nkeys), :]
            vb = v_ref[0, hh, pl.ds(start, nkeys), :]
            s = lax.dot_general(q, kb, _DN_T, preferred_element_type=F32) + bias_ref[case, hh]
            sc = lax.dot_general(q, kc_ref[0, hh], _DN_T, preferred_element_type=F32)
            if not bounded:
                m = jnp.maximum(jnp.max(s, -1, keepdims=True), jnp.max(sc, -1, keepdims=True))
                s, sc = s - m, sc - m
            accs.append(jnp.dot(jnp.exp2(s).astype(BF16), vb, preferred_element_type=F32)
                        + jnp.dot(jnp.exp2(sc).astype(BF16), vc_ref[0, hh], preferred_element_type=F32))
        o_ref[0, sub * tq:(sub + 1) * tq, :] = _pair_output(accs).astype(o_ref.dtype)


def _na_latent(nq, nk, nv, nkc, nvc, bias, bounded):
    b, heads, n, _ = nq.shape
    rows = n // GRID_W
    assert rows >= NA_KROWS and rows % NA_QROWS == 0
    blocks = NA_BLOCKS_PER_STEP
    while (rows // NA_QROWS) % blocks:
        blocks //= 2
    tq = NA_QROWS * GRID_W * blocks
    nctx = nkc.shape[2]
    res = pl.BlockSpec((1, 2, n, LANES), lambda i, p, j: (i, p, 0, 0))
    ctx = pl.BlockSpec((1, 2, nctx, LANES), lambda i, p, j: (i, p, 0, 0))
    bias_spec = pl.BlockSpec((3, 2) + bias.shape[2:], lambda i, p, j: (0, p, 0, 0),
                             pipeline_mode=pl.Buffered(1))

    def call(flag):
        return pl.pallas_call(
            functools.partial(_na_kernel, rows=rows, blocks=blocks, bounded=flag),
            grid=(b, heads // 2, n // tq),
            in_specs=[pl.BlockSpec((1, 2, tq, LANES), lambda i, p, j: (i, p, j, 0)), res, res, ctx, ctx,
                      bias_spec],
            out_specs=pl.BlockSpec((1, tq, LANES), lambda i, p, j: (i, j, p)),
            out_shape=jax.ShapeDtypeStruct((b, n, heads * 64), BF16),
            compiler_params=_cparams(("parallel", "parallel", "parallel"), VMEM_LIMIT),
            name="na_latent" if flag else "na_latent_rowmax",
        )

    return lax.cond(bounded, call(True), call(False), nq, nk, nv, nkc, nvc, bias)


def _mix_residual(x_ref, o_refs, w_ref, ga_ref, g_ref, sh_ref, sc_ref):
    o = jnp.dot(jnp.concatenate([r[0] for r in o_refs], axis=1), w_ref[...], preferred_element_type=F32)
    x1 = x_ref[0] + ga_ref[0] * o
    h = (x1 * _rms(x1, x1.shape[-1])) * g_ref[...]
    return x1, h * (1.0 + sc_ref[0]) + sh_ref[0]


def _outproj_kernel(x_ref, oa_ref, of_ref, op_ref, on_ref, w_ref, ga_ref, g_ref, sh_ref, sc_ref,
                    wrh_ref, wrl_ref, x1_ref, h2_ref, rt_ref, *, n_exp):
    x1, h = _mix_residual(x_ref, (oa_ref, of_ref, op_ref, on_ref), w_ref, ga_ref, g_ref, sh_ref, sc_ref)
    x1_ref[0] = x1
    h2_ref[0] = h
    lane = lax.broadcasted_iota(jnp.int32, (1, LANES), 1)
    hh = h.astype(BF16)
    hl = (h - hh.astype(F32)).astype(BF16)
    lg = (jnp.dot(hh, wrh_ref[...], preferred_element_type=F32)
          + (jnp.dot(hl, wrh_ref[...], preferred_element_type=F32)
             + jnp.dot(hh, wrl_ref[...], preferred_element_type=F32)))
    lg = jnp.where(lane < n_exp, lg, NEG_INF)
    v1 = jnp.max(lg, -1, keepdims=True)
    i1 = jnp.min(jnp.where(lg == v1, lane, LANES), -1, keepdims=True)
    lg2 = jnp.where(lane == i1, NEG_INF, lg)
    v2 = jnp.max(lg2, -1, keepdims=True)
    i2 = jnp.min(jnp.where(lg2 == v2, lane, LANES), -1, keepdims=True)
    e2 = jnp.exp(v2 - v1)
    g1 = 1.0 / (1.0 + e2)
    g2 = e2 / (1.0 + e2)
    rt_ref[0] = jnp.where(lane == 0, i1.astype(F32),
                          jnp.where(lane == 1, i2.astype(F32),
                                    jnp.where(lane == 2, g1, jnp.where(lane == 3, g2, 0.0))))


def _mix_specs(x, oa, w_out, g, tm):
    b, n, d = x.shape
    tok = lambda w: pl.BlockSpec((1, tm, w), lambda i, j: (i, j, 0))
    per_b = pl.BlockSpec((1, 1, d), lambda i, j: (i, 0, 0))
    full = lambda a: pl.BlockSpec(a.shape, lambda i, j: (0,) * a.ndim, pipeline_mode=pl.Buffered(1))
    gw = oa.shape[-1]
    specs = [tok(d), tok(gw), tok(gw), tok(gw), tok(gw), full(w_out), per_b, full(g), per_b, per_b]
    return tok, per_b, full, specs


def _outproj(x, oa, of, op, on, w_out, ga, g, sh, sc, tm, w_router):
    b, n, d = x.shape
    tok, per_b, full, specs = _mix_specs(x, oa, w_out, g, tm)
    n_exp = w_router.shape[-1]
    wr = jnp.pad(w_router.astype(F32), ((0, 0), (0, LANES - n_exp)))
    wr_hi = wr.astype(BF16)
    wr_lo = (wr - wr_hi.astype(F32)).astype(BF16)
    return pl.pallas_call(
        functools.partial(_outproj_kernel, n_exp=n_exp),
        grid=(b, n // tm),
        in_specs=specs + [full(wr_hi), full(wr_lo)],
        out_specs=[tok(d), tok(d), tok(LANES)],
        out_shape=[jax.ShapeDtypeStruct((b, n, d), F32), jax.ShapeDtypeStruct((b, n, d), F32),
                   jax.ShapeDtypeStruct((b, n, LANES), F32)],
        compiler_params=_cparams(("parallel", "parallel"), VMEM_LIMIT),
        name="outproj",
    )(x, oa, of, op, on, w_out, ga, g, sh, sc, wr_hi, wr_lo)


def _outffn_kernel(x_ref, oa_ref, of_ref, op_ref, on_ref, w_ref, ga_ref, g_ref, sh_ref, sc_ref,
                   ga2_ref, w1_ref, w3_ref, w2_ref, o_ref, *, fc):
    x1, h = _mix_residual(x_ref, (oa_ref, of_ref, op_ref, on_ref), w_ref, ga_ref, g_ref, sh_ref, sc_ref)
    hb = h.astype(BF16)
    ff = w1_ref.shape[1]
    gated = []
    for j in range(ff // fc):
        a = jnp.dot(hb, w1_ref[:, j * fc:(j + 1) * fc], preferred_element_type=F32)
        bb = jnp.dot(hb, w3_ref[:, j * fc:(j + 1) * fc], preferred_element_type=F32)
        gated.append((_silu(a) * bb).astype(BF16))
    y = jnp.dot(jnp.concatenate(gated, axis=1), w2_ref[...], preferred_element_type=F32)
    o_ref[0] = x1 + ga2_ref[0] * y


def _outffn(x, oa, of, op, on, w_out, ga, g, sh, sc, ga2, w1, w3, w2, tm):
    b, n, d = x.shape
    ff = w1.shape[1]
    fc = 256 if ff % 256 == 0 else ff
    tok, per_b, full, specs = _mix_specs(x, oa, w_out, g, tm)
    return pl.pallas_call(
        functools.partial(_outffn_kernel, fc=fc),
        grid=(b, n // tm),
        in_specs=specs + [per_b, full(w1), full(w3), full(w2)],
        out_specs=tok(d),
        out_shape=jax.ShapeDtypeStruct((b, n, d), F32),
        compiler_params=_cparams(("parallel", "parallel"), VMEM_LIMIT),
        name="outproj_ffn",
    )(x, oa, of, op, on, w_out, ga, g, sh, sc, ga2, w1, w3, w2)


def _expert_kernel(te_ref, tv_ref, rows_ref, h_hbm, w1_ref, w3_ref, w2_ref, y_hbm, xs_ref, ys_ref,
                   gsem, ssem, zsem, *, fc, n_tiles, n_slots, n_dump_tiles):
    i = pl.program_id(0)
    tme = xs_ref.shape[1]
    nf = w1_ref.shape[2] // fc
    valid = tv_ref[i] > 0
    prev_valid = jnp.logical_and(i > 0, tv_ref[jnp.maximum(i - 1, 0)] > 0)
    slot = i % 2
    prev = jnp.maximum(i - 1, 0)

    def token_of(v):
        if n_slots & (n_slots - 1) == 0:
            return jnp.bitwise_and(v, n_slots - 1)
        return lax.rem(v, n_slots)

    def gather_row(tile, r, sl):
        pltpu.make_async_copy(h_hbm.at[token_of(rows_ref[tile * tme + r])], xs_ref.at[sl, r],
                              gsem.at[sl]).start()

    def scatter_row(r, sl):
        pltpu.make_async_copy(ys_ref.at[sl, r], y_hbm.at[rows_ref[prev * tme + r]], ssem.at[sl]).start()

    def wait_gather(sl):
        pltpu.make_async_copy(h_hbm.at[pl.ds(0, tme)], xs_ref.at[sl], gsem.at[sl]).wait()

    def wait_scatter(sl):
        pltpu.make_async_copy(ys_ref.at[sl], y_hbm.at[pl.ds(0, tme)], ssem.at[sl]).wait()

    @pl.when(i == 0)
    def _():
        ys_ref[1] = jnp.zeros(ys_ref.shape[1:], F32)
        fills = [pltpu.make_async_copy(ys_ref.at[1], y_hbm.at[pl.ds(2 * n_slots + e * tme, tme)], zsem)
                 for e in range(n_dump_tiles)]
        for cp in fills:
            cp.start()
        for cp in fills:
            cp.wait()

        def issue(r, c):
            gather_row(0, r, 0)
            return c

        lax.fori_loop(0, tme, issue, 0, unroll=8)

    @pl.when(jnp.logical_or(i == 0, prev_valid))
    def _():
        wait_gather(slot)

    @pl.when(prev_valid)
    def _():
        wait_scatter(slot)

    @pl.when(valid)
    def _():
        nxt = jnp.minimum(i + 1, n_tiles - 1)
        xb = xs_ref[slot].astype(BF16)
        cuts = [(k * tme) // nf for k in range(nf + 1)]
        gated = []
        for j in range(nf):
            for r in range(cuts[j], cuts[j + 1]):
                gather_row(nxt, r, 1 - slot)
                scatter_row(r, 1 - slot)
            a = jnp.dot(xb, w1_ref[0, :, j * fc:(j + 1) * fc], preferred_element_type=F32)
            bb = jnp.dot(xb, w3_ref[0, :, j * fc:(j + 1) * fc], preferred_element_type=F32)
            gated.append((_silu(a) * bb).astype(BF16))
        ys_ref[slot] = jnp.dot(jnp.concatenate(gated, axis=1), w2_ref[0], preferred_element_type=F32)

    @pl.when(jnp.logical_and(jnp.logical_not(valid), prev_valid))
    def _():
        def issue(r, c):
            scatter_row(r, 1 - slot)
            return c

        lax.fori_loop(0, tme, issue, 0, unroll=8)
        wait_scatter(1 - slot)


def _combine_kernel(x_ref, y0_ref, y1_ref, ga_ref, rt_ref, o_ref):
    rt = rt_ref[0]
    y = rt[:, TOP_K:TOP_K + 1] * y0_ref[...] + rt[:, TOP_K + 1:TOP_K + 2] * y1_ref[...]
    o_ref[0] = x_ref[0] + ga_ref[0] * y


def _moe(h2, x1, ga, rt, w1, w3, w2, tme, tc):
    b, n_b, d = h2.shape
    n = b * n_b
    n_exp, _, ff = w1.shape
    fc = 512 if ff % 512 == 0 else ff
    e_idx = rt[..., 0:TOP_K].astype(jnp.int32).reshape(-1)
    onehot = (e_idx[:, None] == jnp.arange(n_exp)[None, :]).astype(jnp.int32)
    pos = jnp.take_along_axis(jnp.cumsum(onehot, axis=0) - onehot, e_idx[:, None], axis=1)[:, 0]
    counts = jnp.sum(onehot, axis=0)
    padded = ((counts + tme - 1) // tme) * tme
    ends = jnp.cumsum(padded)
    offs = ends - padded
    dest = offs[e_idx] + pos
    n_rows = n * TOP_K + n_exp * tme
    n_tiles = n_rows // tme
    tile_start = jnp.arange(n_tiles + 1, dtype=jnp.int32) * tme
    tvalid = (tile_start < ends[-1]).astype(jnp.int32)
    texp = jnp.minimum(jnp.sum((tile_start[:, None] >= ends[None, :]).astype(jnp.int32), axis=1), n_exp - 1)
    texp = jnp.where(tvalid > 0, texp, texp[jnp.maximum(ends[-1] // tme - 1, 0)])
    p_idx = jnp.arange(n_rows, dtype=jnp.int32)
    row_e = jnp.repeat(texp[:n_tiles], tme)
    sel = row_e[:, None] == jnp.arange(n_exp)[None, :]
    per_row = lambda v: jnp.sum(jnp.where(sel, v[None, :], 0), axis=1)
    first_pad = per_row(offs + counts)
    pads_before = per_row(offs - (jnp.cumsum(counts) - counts))
    order = jnp.argsort(dest).astype(jnp.int32)
    flat = order[jnp.clip(p_idx - pads_before, 0, n * TOP_K - 1)]
    dump = 2 * n + row_e * tme + jnp.clip(p_idx - first_pad, 0, tme - 1)
    rows = jnp.where(p_idx < first_pad, (flat % TOP_K) * n + flat // TOP_K, dump).astype(jnp.int32)
    wspec = lambda shape: pl.BlockSpec((1,) + shape, lambda i, te, tv, s: (te[i], 0, 0),
                                       pipeline_mode=pl.Buffered(1))
    ys = pl.pallas_call(
        functools.partial(_expert_kernel, fc=fc, n_tiles=n_tiles, n_slots=n, n_dump_tiles=n_exp),
        grid_spec=pltpu.PrefetchScalarGridSpec(
            num_scalar_prefetch=3,
            grid=(n_tiles + 1,),
            in_specs=[pl.BlockSpec(memory_space=pl.ANY), wspec((d, ff)), wspec((d, ff)), wspec((ff, d))],
            out_specs=pl.BlockSpec(memory_space=pl.ANY),
            scratch_shapes=[pltpu.VMEM((2, tme, d), F32), pltpu.VMEM((2, tme, d), F32),
                            pltpu.SemaphoreType.DMA((2,)), pltpu.SemaphoreType.DMA((2,)),
                            pltpu.SemaphoreType.DMA(())]),
        out_shape=jax.ShapeDtypeStruct((2 * n + n_exp * tme, d), F32),
        compiler_params=_cparams(("arbitrary",), VMEM_LIMIT),
        name="experts",
    )(texp, tvalid, rows, h2.reshape(n, d), w1, w3, w2)

    nb = n_b // tc
    tok = lambda w: pl.BlockSpec((1, tc, w), lambda i, j: (i, j, 0))
    return pl.pallas_call(
        _combine_kernel,
        grid=(b, nb),
        in_specs=[tok(d),
                  pl.BlockSpec((tc, d), lambda i, j: (i * nb + j, 0)),
                  pl.BlockSpec((tc, d), lambda i, j: (n // tc + i * nb + j, 0)),
                  pl.BlockSpec((1, 1, d), lambda i, j: (i, 0, 0)), tok(LANES)],
        out_specs=tok(d),
        out_shape=jax.ShapeDtypeStruct((b, n_b, d), F32),
        compiler_params=_cparams(("parallel", "parallel"), VMEM_LIMIT),
        name="moe_combine",
    )(x1, ys, ys, ga, rt)


def _pick_tile(n, pref):
    t = pref
    while n % t:
        t //= 2
    return t


def kernel(x, c, ctx, c_ctx, w_ada, b_ada, g_mix, g_ffn, w_in, w_out, g_cq, g_ckv, w_uq, w_ukv,
           g_mla_qn, g_mla_qr, g_mla_kn, g_mla_kr, w_fourier, w_pool, pool_scale, g_na_q, g_na_k,
           na_rpb, w1_dense, w3_dense, w2_dense, w_router, w1_moe, w3_moe, w2_moe):
    b, n, d = x.shape
    nctx = ctx.shape[1]
    depth = w_ada.shape[0]
    assert b <= 7 and n % GRID_W == 0

    cc = jnp.zeros((8, d), F32).at[:b].set(c).at[b].set(c_ctx)
    mods = _adaln(cc, w_ada, b_ada)
    cos_x, sin_x = _rope_tables(n)
    cos_c = jnp.ones((nctx, LANES), F32)
    sin_c = jnp.zeros((nctx, LANES), F32)
    sw = _rope_swap_perm()
    tm_x = _pick_tile(n, 512)
    tm_c = _pick_tile(nctx, 256)
    amax = lambda g: jnp.max(jnp.abs(g.astype(F32)))

    for l in range(depth):
        last = l == depth - 1
        mx = mods[l, :b].reshape(b, 1, 6 * d)
        mc = jnp.broadcast_to(mods[l, b].reshape(1, 1, 6 * d), (b, 1, 6 * d))
        part = lambda m, k: m[:, :, k * d:(k + 1) * d]

        bound_mla = (MLA_SCALE * LOG2E) * (MLA_NOPE * amax(g_mla_qn[l]) * amax(g_mla_kn[l])
                                           + MLA_ROPE * amax(g_mla_qr[l]) * amax(g_mla_kr[l]))
        qk_na = (NA_SCALE * LOG2E) * NA_HEAD_DIM * amax(g_na_q[l]) * amax(g_na_k[l])
        rpb_hi = jnp.maximum(jnp.max(na_rpb[l]).astype(F32), 0.0) * LOG2E
        rpb_lo = jnp.minimum(jnp.min(na_rpb[l]).astype(F32), 0.0) * LOG2E
        bound_na = qk_na + rpb_hi
        mla_bounded = 2.0 * bound_mla <= MAX_SOFTMAX_GAP_LOG2
        na_bounded = 2.0 * qk_na + rpb_hi - rpb_lo <= MAX_SOFTMAX_GAP_LOG2
        bounds = jnp.zeros((1, LANES), F32).at[0, 0].set(bound_mla).at[0, 1].set(bound_na)

        wts = [
            _prep_in_weights(w_in[l]),
            g_cq[l].reshape(1, -1), _prep_uq(w_uq[l]),
            _lane_vec([(g_mla_qn[l], 64), (g_mla_qr[l], 32), (None, 32)]),
            _lane_vec([(None, 64), (g_mla_qr[l][sw], 32), (None, 32)]),
            g_ckv[l].reshape(1, -1), _prep_ukv(w_ukv[l]),
            _lane_vec([(g_mla_kn[l], 64), (None, 64)]),
            _lane_vec([(None, 64), (g_mla_kr[l], 32), (None, 32)]),
            _lane_vec([(None, 64), (g_mla_kr[l][sw], 32), (None, 32)]),
            jnp.tile(g_na_q[l], NA_HEADS).reshape(1, -1), jnp.tile(g_na_k[l], NA_HEADS).reshape(1, -1),
        ]
        gmix = g_mix[l].reshape(1, d)
        qx, kx, vx, fx, px, nqx, nkx, nvx = _inproj(x, part(mx, 0), part(mx, 1), gmix, cos_x, sin_x,
                                                    bounds, wts, tm_x)
        qc, kc, vc, fc_, pc, nqc, nkc, nvc = _inproj(ctx, part(mc, 0), part(mc, 1), gmix, cos_c, sin_c,
                                                     bounds, wts, tm_c)

        wf = w_fourier[l].astype(BF16)
        cg = w_pool.shape[-1]
        wp_bd = jnp.zeros((len(POOL_WINDOWS) * cg,) * 2, F32)
        for gi in range(len(POOL_WINDOWS)):
            wp_bd = wp_bd.at[gi * cg:(gi + 1) * cg, gi * cg:(gi + 1) * cg].set(w_pool[l, gi])
        wp_bd = wp_bd.astype(BF16)
        ps = pool_scale[l].reshape(1, -1)
        w_out_l = w_out[l].astype(BF16)
        gffn = g_ffn[l].reshape(1, d)

        o_mla = _flash_pairs(qx, [(kx, vx), (kc, vc)], _pick_tile(n, 256), mla_bounded)
        o_f = _fourier_latent(fx, wf)
        o_p = _pool(px, wp_bd, ps, _pick_tile(n, 512))
        o_na = _na_latent(nqx, nkx, nvx, nkc, nvc, _na_bias_tables(na_rpb[l]), na_bounded)

        moe_layer = l % 2 == 1
        i = l // 2
        if moe_layer:
            w1, w3, w2 = w1_moe[i].astype(BF16), w3_moe[i].astype(BF16), w2_moe[i].astype(BF16)
            x1, h2, rt = _outproj(x, o_mla, o_f, o_p, o_na, w_out_l, part(mx, 2), gffn, part(mx, 3),
                                  part(mx, 4), _pick_tile(n, 256), w_router[i])
            x = _moe(h2, x1, part(mx, 5), rt, w1, w3, w2, tme=512, tc=256)
        else:
            w1, w3, w2 = w1_dense[i].astype(BF16), w3_dense[i].astype(BF16), w2_dense[i].astype(BF16)
            x = _outffn(x, o_mla, o_f, o_p, o_na, w_out_l, part(mx, 2), gffn, part(mx, 3), part(mx, 4),
                        part(mx, 5), w1, w3, w2, tm_x)

        if not last:
            oc_mla = _flash_pairs(qc, [(kc, vc)], tm_c, mla_bounded)
            oc_f = _fourier_dense(fc_, wf)
            oc_p = _pool(pc, wp_bd, ps, tm_c)
            oc_na = _flash_pairs(nqc, [(nkc, nvc)], tm_c, na_bounded)
            if moe_layer:
                c1, hc2, rtc = _outproj(ctx, oc_mla, oc_f, oc_p, oc_na, w_out_l, part(mc, 2), gffn,
                                        part(mc, 3), part(mc, 4), tm_c, w_router[i])
                ctx = _moe(hc2, c1, part(mc, 5), rtc, w1, w3, w2, tme=256, tc=256)
            else:
                ctx = _outffn(ctx, oc_mla, oc_f, oc_p, oc_na, w_out_l, part(mc, 2), gffn, part(mc, 3),
                              part(mc, 4), part(mc, 5), w1, w3, w2, tm_c)
    return x
```

```python
import functools
import math

import numpy as np
import jax
import jax.numpy as jnp
from jax import lax
from jax.experimental import pallas as pl
from jax.experimental.pallas import tpu as pltpu

F32 = jnp.float32
BF16 = jnp.bfloat16

GRID_W = 64
LANES = 128
EPS = 1e-6
NEG_INF = -1e30

MLA_HEADS = 4
MLA_NOPE = 64
MLA_ROPE = 32
MLA_V = 64
MLA_SCALE = (MLA_NOPE + MLA_ROPE) ** -0.5
MLA_SPARE_LANE = MLA_NOPE + MLA_ROPE
ROPE_BASE = 10000.0
LOG2E = 1.0 / math.log(2.0)
MAX_SOFTMAX_GAP_LOG2 = 120.0

FOURIER_GROUPS = 4
POOL_WINDOWS = (2, 4, 8, 16)
POOL_HALO = 8

NA_HEADS = 4
NA_HEAD_DIM = 64
NA_SCALE = NA_HEAD_DIM ** -0.5
NA_WIN_ROWS = 8
NA_WIN_COLS = 16
NA_QROWS = 8
NA_KROWS = 16
NA_BLOCKS_PER_STEP = 8

TOP_K = 2
VMEM_LIMIT = 56 * 1024 * 1024


def _cparams(sem, vmem=None):
    return pltpu.CompilerParams(dimension_semantics=sem, vmem_limit_bytes=vmem)


def _silu(a):
    return a / (1.0 + jnp.exp(-a))


def _rms(v, n):
    return lax.rsqrt(jnp.sum(v * v, axis=-1, keepdims=True) * (1.0 / n) + EPS)


def _adaln_kernel(c_ref, w_ref, b_ref, o_ref):
    o_ref[0] = jnp.dot(_silu(c_ref[...]), w_ref[0], preferred_element_type=F32) + b_ref[0]


def _adaln(cc, w_ada, b_ada):
    depth, d, d6 = w_ada.shape
    tn = 512
    return pl.pallas_call(
        _adaln_kernel,
        grid=(depth, d6 // tn),
        in_specs=[pl.BlockSpec((8, d), lambda l, j: (0, 0)),
                  pl.BlockSpec((1, d, tn), lambda l, j: (l, 0, j)),
                  pl.BlockSpec((1, 1, tn), lambda l, j: (l, 0, j))],
        out_specs=pl.BlockSpec((1, 8, tn), lambda l, j: (l, 0, j)),
        out_shape=jax.ShapeDtypeStruct((depth, 8, d6), F32),
        compiler_params=_cparams(("arbitrary", "arbitrary")),
        name="adaln",
    )(cc, w_ada, b_ada.reshape(depth, 1, d6))


IN_COLS = 1920


def _rope_swap_perm():
    j = np.arange(MLA_ROPE)
    return np.where((j % 16) < 8, j + 8, j - 8)


def _prep_in_weights(w_in_l):
    d = w_in_l.shape[0]
    s = [0, 256, 384, 416, 672, 928, 1184, 1440, 1696]
    cq, ckv, kr, f, p, nq, nk, nv = [w_in_l[:, s[i]:s[i + 1]] for i in range(8)]
    z = lambda n: jnp.zeros((d, n), w_in_l.dtype)
    krsw = kr[:, _rope_swap_perm()]
    return jnp.concatenate([cq, ckv, f, p, nq, nk, nv, z(64), kr, z(32), z(64), krsw, z(32)],
                           axis=1).astype(BF16)


def _prep_uq(w_uq_l):
    r = w_uq_l.shape[0]
    w = w_uq_l.reshape(r, MLA_HEADS, MLA_NOPE + MLA_ROPE)
    z = lambda n: jnp.zeros((r, MLA_HEADS, n), w.dtype)
    main = jnp.concatenate([w, z(32)], axis=-1)
    sw = jnp.concatenate([z(64), w[..., MLA_NOPE:][..., _rope_swap_perm()], z(32)], axis=-1)
    return jnp.concatenate([main.reshape(r, -1), sw.reshape(r, -1)], axis=1).astype(BF16)


def _prep_ukv(w_ukv_l):
    r = w_ukv_l.shape[0]
    w = w_ukv_l.reshape(r, MLA_HEADS, MLA_NOPE + MLA_V)
    z = jnp.zeros((r, 64), w.dtype)
    tiles = [jnp.concatenate([w[:, h, :MLA_NOPE], z], axis=1) for h in range(MLA_HEADS)]
    for h in range(MLA_HEADS):
        v = w[:, h, MLA_NOPE:]
        tiles.append(jnp.concatenate([v, z] if h % 2 == 0 else [z, v], axis=1))
    return jnp.concatenate(tiles, axis=1).astype(BF16)


def _lane_vec(parts):
    cols = [jnp.zeros((n,), F32) if a is None else a.astype(F32) for a, n in parts]
    return jnp.concatenate(cols).reshape(1, -1)


def _rope_tables(n_tokens):
    t = np.arange(n_tokens)
    pos = np.stack([t // GRID_W, t % GRID_W], axis=1).astype(np.float64)
    half = MLA_ROPE // 2
    inv = 1.0 / (ROPE_BASE ** (np.arange(0, half, 2, dtype=np.float64) / half))
    j = np.arange(MLA_ROPE)
    ang = pos[:, j // 16] * inv[j % 8]
    sign = np.where((j % 16) < 8, -1.0, 1.0)
    cos = np.ones((n_tokens, LANES), np.float32)
    sin = np.zeros((n_tokens, LANES), np.float32)
    cos[:, 64:96] = np.cos(ang)
    sin[:, 64:96] = np.sin(ang) * sign
    return jnp.asarray(cos), jnp.asarray(sin)


def _inproj_kernel(x_ref, sh_ref, sc_ref, g_ref, cos_ref, sin_ref, bound_ref, win_ref,
                   gcq_ref, wuq_ref, gq_ref, gqsw_ref,
                   gckv_ref, wukv_ref, gkn_ref, gkr_ref, gkrsw_ref, gnq_ref, gnk_ref,
                   q_ref, k_ref, v_ref, f_ref, p_ref, nq_ref, nk_ref, nv_ref):
    x = x_ref[0]
    d = x.shape[-1]
    h = (x * _rms(x, d)) * g_ref[...]
    h = h * (1.0 + sc_ref[0]) + sh_ref[0]
    u = jnp.dot(h.astype(BF16), win_ref[...], preferred_element_type=F32)
    cq, ckv = u[:, 0:256], u[:, 256:384]
    f_ref[0] = u[:, 384:640].astype(BF16)
    p_ref[0] = u[:, 640:896]
    nq, nk, nv = u[:, 896:1152], u[:, 1152:1408], u[:, 1408:1664]
    krb, krs = u[:, 1664:1792], u[:, 1792:1920]
    cos, sin = cos_ref[...], sin_ref[...]
    lane = lax.broadcasted_iota(jnp.int32, (1, LANES), 1)
    m_nope = lane < MLA_NOPE
    m_rope = jnp.logical_and(lane >= MLA_NOPE, lane < MLA_NOPE + MLA_ROPE)
    bound_mla, bound_na = bound_ref[:, 0:1], bound_ref[:, 1:2]

    cqn = (cq * _rms(cq, 256) * gcq_ref[...]).astype(BF16)
    qall = jnp.dot(cqn, wuq_ref[...], preferred_element_type=F32)
    for hd in range(MLA_HEADS):
        blk = qall[:, hd * LANES:(hd + 1) * LANES]
        sw = qall[:, (MLA_HEADS + hd) * LANES:(MLA_HEADS + hd + 1) * LANES]
        sq = blk * blk
        rn = lax.rsqrt(jnp.sum(jnp.where(m_nope, sq, 0.0), -1, keepdims=True) * (1.0 / MLA_NOPE) + EPS)
        rr = lax.rsqrt(jnp.sum(jnp.where(m_rope, sq, 0.0), -1, keepdims=True) * (1.0 / MLA_ROPE) + EPS)
        qh = blk * jnp.where(m_nope, rn, rr) * gq_ref[...]
        qs = sw * rr * gqsw_ref[...]
        q_ref[0, hd] = jnp.where(lane == MLA_SPARE_LANE, -bound_mla,
                                 (qh * cos + qs * sin) * (MLA_SCALE * LOG2E)).astype(BF16)

    ckvn = (ckv * _rms(ckv, 128) * gckv_ref[...]).astype(BF16)
    kvall = jnp.dot(ckvn, wukv_ref[...], preferred_element_type=F32)
    rkr = _rms(krb, MLA_ROPE)
    krot = (krb * rkr * gkr_ref[...]) * cos + (krs * rkr * gkrsw_ref[...]) * sin
    for hd in range(MLA_HEADS):
        blk = kvall[:, hd * LANES:(hd + 1) * LANES]
        k_ref[0, hd] = jnp.where(lane == MLA_SPARE_LANE, 1.0,
                                 blk * _rms(blk, MLA_NOPE) * gkn_ref[...] + krot).astype(BF16)
        vb = kvall[:, (MLA_HEADS + hd) * LANES:(MLA_HEADS + hd + 1) * LANES]
        one_lane = 64 if hd % 2 == 0 else 0
        v_ref[0, hd] = jnp.where(lane == one_lane, 1.0, vb).astype(BF16)

    lane2 = lax.broadcasted_iota(jnp.int32, (1, 2 * LANES), 1)

    def seg_rms(t):
        sq = t * t
        r = jnp.zeros_like(t)
        for s in range(NA_HEADS):
            m = (lane2 // NA_HEAD_DIM) == s
            ss = jnp.sum(jnp.where(m, sq, 0.0), -1, keepdims=True)
            r = jnp.where(m, lax.rsqrt(ss * (1.0 / NA_HEAD_DIM) + EPS), r)
        return r

    nqn = nq * seg_rms(nq) * gnq_ref[...] * (NA_SCALE * LOG2E)
    nkn = nk * seg_rms(nk) * gnk_ref[...]
    for hd in range(NA_HEADS):
        t0 = (hd // 2) * LANES
        own = (lane < 64) if hd % 2 == 0 else (lane >= 64)
        one_lane = 64 if hd % 2 == 0 else 0
        spare = lane == (LANES - 1 if hd % 2 == 0 else 0)
        nq_ref[0, hd] = jnp.where(own, nqn[:, t0:t0 + LANES], jnp.where(spare, -bound_na, 0.0)).astype(BF16)
        nk_ref[0, hd] = jnp.where(own, nkn[:, t0:t0 + LANES], jnp.where(spare, 1.0, 0.0)).astype(BF16)
        nv_ref[0, hd] = jnp.where(own, nv[:, t0:t0 + LANES],
                                  jnp.where(lane == one_lane, 1.0, 0.0)).astype(BF16)


def _inproj(x, sh, sc, g, cos, sin, bounds, wts, tm):
    b, n, d = x.shape
    heads = MLA_HEADS
    row = lambda a: pl.BlockSpec(a.shape, lambda i, j: (0,) * a.ndim)
    per_b = pl.BlockSpec((1, 1, d), lambda i, j: (i, 0, 0))
    tok = lambda w: pl.BlockSpec((1, tm, w), lambda i, j: (i, j, 0))
    hd_spec = pl.BlockSpec((1, heads, tm, LANES), lambda i, j: (i, 0, j, 0))
    tab = pl.BlockSpec((tm, LANES), lambda i, j: (j, 0))
    hshape = jax.ShapeDtypeStruct((b, heads, n, LANES), BF16)
    return pl.pallas_call(
        _inproj_kernel,
        grid=(b, n // tm),
        in_specs=[tok(d), per_b, per_b, row(g), tab, tab, row(bounds)] + [row(w) for w in wts],
        out_specs=[hd_spec, hd_spec, hd_spec, tok(256), tok(256), hd_spec, hd_spec, hd_spec],
        out_shape=[hshape, hshape, hshape,
                   jax.ShapeDtypeStruct((b, n, 256), BF16), jax.ShapeDtypeStruct((b, n, 256), F32),
                   hshape, hshape, hshape],
        compiler_params=_cparams(("parallel", "parallel"), VMEM_LIMIT),
        name="inproj",
    )(x, sh, sc, g, cos, sin, bounds, *wts)


_DN_T = (((1,), (1,)), ((), ()))


def _pair_output(accs):
    lane = lax.broadcasted_iota(jnp.int32, (1, LANES), 1)
    out = None
    for hh, acc in enumerate(accs):
        one_lane = 64 if hh == 0 else 0
        own = (lane < 64) if hh == 0 else (lane >= 64)
        o = jnp.where(own, acc / acc[:, one_lane:one_lane + 1], 0.0)
        out = o if out is None else out + o
    return out


def _flash_pair_kernel(q_ref, *refs, chunks, bounded):
    o_ref = refs[-1]
    tq = q_ref.shape[2]
    accs = []
    for hh in range(2):
        q = q_ref[0, hh]
        acc = jnp.zeros((tq, LANES), F32)
        m = jnp.full((tq, 1), NEG_INF, F32)
        for kset, start, size in chunks:
            ks = refs[2 * kset][0, hh, start:start + size, :]
            vs = refs[2 * kset + 1][0, hh, start:start + size, :]
            s = lax.dot_general(q, ks, _DN_T, preferred_element_type=F32)
            if bounded:
                acc = acc + jnp.dot(jnp.exp2(s).astype(BF16), vs, preferred_element_type=F32)
            else:
                m_new = jnp.maximum(m, jnp.max(s, axis=-1, keepdims=True))
                p = jnp.exp2(s - m_new).astype(BF16)
                acc = acc * jnp.exp2(m - m_new) + jnp.dot(p, vs, preferred_element_type=F32)
                m = m_new
        accs.append(acc)
    o_ref[0] = _pair_output(accs).astype(o_ref.dtype)


def _flash_pairs(q, kv_sets, tq, bounded, tk=2048):
    b, heads, lq, _ = q.shape
    chunks, ops, kv_specs = [], [], []
    for si, (k, v) in enumerate(kv_sets):
        lk = k.shape[2]
        step = tk if lk % tk == 0 else lk
        chunks += [(si, st, step) for st in range(0, lk, step)]
        ops += [k, v]
        kv_specs += [pl.BlockSpec((1, 2, lk, LANES), lambda i, p, j: (i, p, 0, 0))] * 2

    def call(flag):
        return pl.pallas_call(
            functools.partial(_flash_pair_kernel, chunks=tuple(chunks), bounded=flag),
            grid=(b, heads // 2, lq // tq),
            in_specs=[pl.BlockSpec((1, 2, tq, LANES), lambda i, p, j: (i, p, j, 0))] + kv_specs,
            out_specs=pl.BlockSpec((1, tq, LANES), lambda i, p, j: (i, j, p)),
            out_shape=jax.ShapeDtypeStruct((b, lq, heads * 64), BF16),
            compiler_params=_cparams(("parallel", "parallel", "parallel"), VMEM_LIMIT),
            name="flash_pairs" if flag else "flash_pairs_online",
        )

    return lax.cond(bounded, call(True), call(False), q, *ops)


def _table(a):
    return jnp.asarray(a, F32).astype(BF16)


def _dft_consts(n_tokens):
    rows = n_tokens // GRID_W
    r = np.arange(rows)
    ang_r = 2 * np.pi * np.outer(r, r) / rows
    a_mat = np.concatenate([np.cos(ang_r), -np.sin(ang_r)], axis=0)
    c = np.arange(GRID_W)
    ang_t = 2 * np.pi * np.outer(r, c) / n_tokens
    ang_c = 2 * np.pi * np.outer(c, c) / GRID_W
    cc, sc = np.cos(ang_c), np.sin(ang_c)
    m_b = np.block([[cc, sc], [-sc, cc]])
    return a_mat, np.cos(ang_t), np.sin(ang_t), m_b


def _channel_dft(n_tokens, width):
    cg = width // FOURIER_GROUPS
    j = np.arange(cg)
    ang = 2 * np.pi * np.outer(j, j) / cg
    eye = np.eye(FOURIER_GROUPS)
    scale = 1.0 / math.sqrt(n_tokens * cg)
    return np.kron(eye, np.cos(ang)) * scale, np.kron(eye, np.sin(ang)) * scale


def _fourier_rows_kernel(u_ref, a_ref, tc_ref, ts_ref, o_ref):
    z = jnp.dot(a_ref[...], u_ref[0], preferred_element_type=F32)
    r = z.shape[0] // 2
    zr, zi = z[:r], z[r:]
    c, s = tc_ref[...], ts_ref[...]
    o_ref[0, :r] = (zr * c + zi * s).astype(BF16)
    o_ref[0, r:] = (zi * c - zr * s).astype(BF16)


def _fourier_cols_kernel(z_ref, mb_ref, cs_ref, wf_ref, o_ref):
    kb = z_ref.shape[1]
    w = GRID_W
    cw = wf_ref.shape[0]
    ys = [jnp.dot(mb_ref[...], z_ref[0, t], preferred_element_type=F32) for t in range(kb)]
    yr = jnp.concatenate([y[:w] for y in ys], axis=0).astype(BF16)
    yi = jnp.concatenate([y[w:] for y in ys], axis=0).astype(BF16)
    f = (jnp.dot(yr, cs_ref[:cw], preferred_element_type=F32)
         + jnp.dot(yi, cs_ref[cw:], preferred_element_type=F32))
    out = jnp.dot(f.astype(BF16), wf_ref[...], preferred_element_type=F32)
    for t in range(kb):
        o_ref[0, t] = out[t * w:(t + 1) * w].astype(o_ref.dtype)


def _fourier_latent(uf, wf):
    b, n, cw = uf.shape
    rows = n // GRID_W
    a_mat, tw_c, tw_s, m_b = _dft_consts(n)
    cc, sc = _channel_dft(n, cw)
    wide = GRID_W * cw
    tn = 2048
    expand = lambda t: jnp.broadcast_to(jnp.asarray(t, F32)[:, :, None], (rows, GRID_W, cw)).reshape(rows, wide)
    z = pl.pallas_call(
        _fourier_rows_kernel,
        grid=(wide // tn, b),
        in_specs=[pl.BlockSpec((1, rows, tn), lambda j, i: (i, 0, j)),
                  pl.BlockSpec((2 * rows, rows), lambda j, i: (0, 0)),
                  pl.BlockSpec((rows, tn), lambda j, i: (0, j)),
                  pl.BlockSpec((rows, tn), lambda j, i: (0, j))],
        out_specs=pl.BlockSpec((1, 2 * rows, tn), lambda j, i: (i, 0, j)),
        out_shape=jax.ShapeDtypeStruct((b, 2 * rows, wide), BF16),
        compiler_params=_cparams(("parallel", "parallel")),
        name="fourier_rows",
    )(uf.reshape(b, rows, wide), _table(a_mat), expand(tw_c), expand(tw_s))
    z = z.reshape(b, 2, rows, GRID_W, cw).transpose(0, 2, 1, 3, 4).reshape(b, rows, 2 * GRID_W, cw)
    kb = 16
    y = pl.pallas_call(
        _fourier_cols_kernel,
        grid=(b, rows // kb),
        in_specs=[pl.BlockSpec((1, kb, 2 * GRID_W, cw), lambda i, j: (i, j, 0, 0)),
                  pl.BlockSpec((2 * GRID_W, 2 * GRID_W), lambda i, j: (0, 0)),
                  pl.BlockSpec((2 * cw, cw), lambda i, j: (0, 0)),
                  pl.BlockSpec((cw, cw), lambda i, j: (0, 0))],
        out_specs=pl.BlockSpec((1, kb, GRID_W, cw), lambda i, j: (i, j, 0, 0)),
        out_shape=jax.ShapeDtypeStruct((b, rows, GRID_W, cw), BF16),
        compiler_params=_cparams(("parallel", "parallel")),
        name="fourier_cols",
    )(z, _table(m_b), _table(np.concatenate([cc, sc], axis=0)), wf)
    return y.transpose(0, 2, 1, 3).reshape(b, n, cw)


def _fourier_dense_kernel(u_ref, cl_ref, sl_ref, cs_ref, wf_ref, o_ref):
    u = u_ref[0]
    cw = wf_ref.shape[0]
    a = jnp.dot(u, cs_ref[:cw], preferred_element_type=F32).astype(BF16)
    bb = jnp.dot(u, cs_ref[cw:], preferred_element_type=F32).astype(BF16)
    f = (jnp.dot(cl_ref[...], a, preferred_element_type=F32)
         - jnp.dot(sl_ref[...], bb, preferred_element_type=F32))
    o_ref[0] = jnp.dot(f.astype(BF16), wf_ref[...], preferred_element_type=F32).astype(o_ref.dtype)


def _fourier_dense(uf, wf):
    b, n, cw = uf.shape
    t = np.arange(n)
    ang = 2 * np.pi * np.outer(t, t) / n
    cc, sc = _channel_dft(n, cw)
    full = lambda shape: pl.BlockSpec(shape, lambda i: (0,) * len(shape))
    return pl.pallas_call(
        _fourier_dense_kernel,
        grid=(b,),
        in_specs=[pl.BlockSpec((1, n, cw), lambda i: (i, 0, 0)), full((n, n)), full((n, n)),
                  full((2 * cw, cw)), full((cw, cw))],
        out_specs=pl.BlockSpec((1, n, cw), lambda i: (i, 0, 0)),
        out_shape=jax.ShapeDtypeStruct((b, n, cw), BF16),
        compiler_params=_cparams(("parallel",)),
        name="fourier_dense",
    )(uf, _table(np.cos(ang)), _table(np.sin(ang)), _table(np.concatenate([cc, sc], axis=0)), wf)


def _pool_kernel(x_ref, prev_ref, next_ref, wp_ref, ps_ref, o_ref, pad_ref, s2_ref, s4_ref, s8_ref, *,
                 n_tokens):
    assert POOL_WINDOWS == (2, 4, 8, 16)
    i = pl.program_id(1)
    tm = x_ref.shape[1]
    h = POOL_HALO
    n = tm + 2 * h
    zeros = jnp.zeros((h, x_ref.shape[2]), F32)
    pad_ref[0:h] = zeros
    pad_ref[h:2 * h] = jnp.where(i > 0, prev_ref[0], 0.0)
    pad_ref[2 * h:2 * h + tm] = x_ref[0]
    pad_ref[2 * h + tm:3 * h + tm] = jnp.where(i < pl.num_programs(1) - 1, next_ref[0], 0.0)
    for ref in (s2_ref, s4_ref):
        ref[0:h] = zeros
        ref[h + n:2 * h + n] = zeros
    s2_ref[h:h + n] = pad_ref[h - 1:h - 1 + n] + pad_ref[h:h + n]
    s4_ref[h:h + n] = s2_ref[h - 1:h - 1 + n] + s2_ref[h + 1:h + 1 + n]
    s8_ref[h:h + n] = s4_ref[h - 2:h - 2 + n] + s4_ref[h + 2:h + 2 + n]
    lo = 2 * h
    x0 = pad_ref[lo:lo + tm]
    sums = [s2_ref[lo:lo + tm], s4_ref[lo:lo + tm], s8_ref[lo:lo + tm],
            s8_ref[lo - 4:lo - 4 + tm] + s8_ref[lo + 4:lo + 4 + tm]]
    t = i * tm + lax.broadcasted_iota(jnp.int32, (tm, 1), 0)
    grp = lax.broadcasted_iota(jnp.int32, (1, x0.shape[1]), 1) // (x0.shape[1] // len(POOL_WINDOWS))
    pooled = jnp.zeros_like(x0)
    for gi, w in enumerate(POOL_WINDOWS):
        cnt = (jnp.minimum(t + w // 2, n_tokens) - jnp.maximum(t - w // 2, 0)).astype(F32)
        pooled = jnp.where(grp == gi, sums[gi] * (1.0 / cnt), pooled)
    pooled = pooled - x0
    y = jnp.dot(pooled.astype(BF16), wp_ref[...], preferred_element_type=F32) * ps_ref[...]
    o_ref[0] = y.astype(o_ref.dtype)


def _pool(up, wp_bd, ps, tm):
    b, n, cw = up.shape
    h = POOL_HALO
    nblk = n // h
    return pl.pallas_call(
        functools.partial(_pool_kernel, n_tokens=n),
        grid=(b, n // tm),
        in_specs=[pl.BlockSpec((1, tm, cw), lambda i, j: (i, j, 0)),
                  pl.BlockSpec((1, h, cw), lambda i, j: (i, jnp.maximum(j * (tm // h) - 1, 0), 0)),
                  pl.BlockSpec((1, h, cw), lambda i, j: (i, jnp.minimum((j + 1) * (tm // h), nblk - 1), 0)),
                  pl.BlockSpec((cw, cw), lambda i, j: (0, 0)),
                  pl.BlockSpec((1, cw), lambda i, j: (0, 0))],
        out_specs=pl.BlockSpec((1, tm, cw), lambda i, j: (i, j, 0)),
        out_shape=jax.ShapeDtypeStruct((b, n, cw), BF16),
        scratch_shapes=[pltpu.VMEM((tm + 4 * h, cw), F32)] * 4,
        compiler_params=_cparams(("parallel", "parallel")),
        name="pool",
    )(up, up, up, wp_bd, ps)


def _na_row_cases():
    qr = np.arange(NA_QROWS)[:, None]
    kr = np.arange(NA_KROWS)[None, :]
    masked = 2 * NA_WIN_ROWS - 1
    cases = []
    for case in range(3):
        krel = kr - 4 * case
        r0 = qr - NA_WIN_ROWS // 2
        r0 = np.maximum(r0, 0) if case == 0 else (np.minimum(r0, 0) if case == 2 else r0)
        row_ok = (krel >= r0) & (krel < r0 + NA_WIN_ROWS)
        cases.append(np.where(row_ok, krel - qr + NA_WIN_ROWS - 1, masked))
    return cases


def _na_bias_kernel(lo_ref, hi_ref, o_ref):
    for case, ro in enumerate(_na_row_cases()):
        for qr in range(NA_QROWS):
            for j in range(NA_KROWS // 2):
                o_ref[case, 0, qr * GRID_W:(qr + 1) * GRID_W, j * LANES:(j + 1) * LANES] = (
                    lo_ref[0, int(ro[qr, 2 * j])] + hi_ref[0, int(ro[qr, 2 * j + 1])])


def _na_bias_tables(rpb):
    heads = rpb.shape[0]
    qc = np.arange(GRID_W)[:, None]
    kc = np.arange(GRID_W)[None, :]
    c0 = np.clip(qc - NA_WIN_COLS // 2, 0, GRID_W - NA_WIN_COLS)
    col_ok = (kc >= c0) & (kc < c0 + NA_WIN_COLS)
    co = np.clip(kc - qc + NA_WIN_COLS - 1, 0, 2 * NA_WIN_COLS - 2)
    pick_c = (co.reshape(1, -1) == np.arange(2 * NA_WIN_COLS - 1)[:, None]).astype(np.float32)
    by_col = jnp.einsum("hrc,cx->hrx", rpb.astype(F32) * LOG2E, jnp.asarray(pick_c),
                        precision=lax.Precision.HIGHEST).reshape(heads, -1, GRID_W, GRID_W)
    blocks = jnp.where(jnp.asarray(col_ok)[None, None], by_col, NEG_INF)
    blocks = jnp.concatenate([blocks, jnp.full((heads, 1, GRID_W, GRID_W), NEG_INF, F32)], axis=1)
    zeros = jnp.zeros_like(blocks)
    lo = jnp.concatenate([blocks, zeros], axis=-1)
    hi = jnp.concatenate([zeros, blocks], axis=-1)
    n_off = blocks.shape[1]
    blk = pl.BlockSpec((1, n_off, GRID_W, LANES), lambda h: (h, 0, 0, 0))
    return pl.pallas_call(
        _na_bias_kernel,
        grid=(heads,),
        in_specs=[blk, blk],
        out_specs=pl.BlockSpec((3, 1, NA_QROWS * GRID_W, NA_KROWS * GRID_W), lambda h: (0, h, 0, 0)),
        out_shape=jax.ShapeDtypeStruct((3, heads, NA_QROWS * GRID_W, NA_KROWS * GRID_W), F32),
        compiler_params=_cparams(("parallel",), VMEM_LIMIT),
        name="na_bias",
    )(lo, hi)


def _na_kernel(q_ref, k_ref, v_ref, kc_ref, vc_ref, bias_ref, o_ref, *, rows, blocks, bounded):
    nkeys = NA_KROWS * GRID_W
    tq = NA_QROWS * GRID_W
    for sub in range(blocks):
        rb = pl.program_id(2) * blocks + sub
        kr0 = jnp.clip(rb * NA_QROWS - NA_WIN_ROWS // 2, 0, rows - NA_KROWS)
        case = lax.shift_right_logical(rb * NA_QROWS - kr0, 2)
        start = pl.multiple_of(kr0 * GRID_W, GRID_W)
        accs = []
        for hh in range(2):
            q = q_ref[0, hh, sub * tq:(sub + 1) * tq, :]
            kb = k_ref[0, hh, pl.ds(start, nkeys), :]
            vb = v_ref[0, hh, pl.ds(start, nkeys), :]
            s = lax.dot_general(q, kb, _DN_T, preferred_element_type=F32) + bias_ref[case, hh]
            sc = lax.dot_general(q, kc_ref[0, hh], _DN_T, preferred_element_type=F32)
            if not bounded:
                m = jnp.maximum(jnp.max(s, -1, keepdims=True), jnp.max(sc, -1, keepdims=True))
                s, sc = s - m, sc - m
            accs.append(jnp.dot(jnp.exp2(s).astype(BF16), vb, preferred_element_type=F32)
                        + jnp.dot(jnp.exp2(sc).astype(BF16), vc_ref[0, hh], preferred_element_type=F32))
        o_ref[0, sub * tq:(sub + 1) * tq, :] = _pair_output(accs).astype(o_ref.dtype)


def _na_latent(nq, nk, nv, nkc, nvc, bias, bounded):
    b, heads, n, _ = nq.shape
    rows = n // GRID_W
    assert rows >= NA_KROWS and rows % NA_QROWS == 0
    blocks = NA_BLOCKS_PER_STEP
    while (rows // NA_QROWS) % blocks:
        blocks //= 2
    tq = NA_QROWS * GRID_W * blocks
    nctx = nkc.shape[2]
    res = pl.BlockSpec((1, 2, n, LANES), lambda p, i, j: (i, p, 0, 0))
    ctx = pl.BlockSpec((1, 2, nctx, LANES), lambda p, i, j: (i, p, 0, 0))
    bias_spec = pl.BlockSpec((3, 2) + bias.shape[2:], lambda p, i, j: (0, p, 0, 0),
                             pipeline_mode=pl.Buffered(1))

    def call(flag):
        return pl.pallas_call(
            functools.partial(_na_kernel, rows=rows, blocks=blocks, bounded=flag),
            grid=(heads // 2, b, n // tq),
            in_specs=[pl.BlockSpec((1, 2, tq, LANES), lambda p, i, j: (i, p, j, 0)), res, res, ctx, ctx,
                      bias_spec],
            out_specs=pl.BlockSpec((1, tq, LANES), lambda p, i, j: (i, j, p)),
            out_shape=jax.ShapeDtypeStruct((b, n, heads * 64), BF16),
            compiler_params=_cparams(("parallel", "parallel", "parallel"), VMEM_LIMIT),
            name="na_latent" if flag else "na_latent_rowmax",
        )

    return lax.cond(bounded, call(True), call(False), nq, nk, nv, nkc, nvc, bias)


def _mix_residual(x_ref, o_refs, w_ref, ga_ref, g_ref, sh_ref, sc_ref):
    o = jnp.dot(jnp.concatenate([r[0] for r in o_refs], axis=1), w_ref[...], preferred_element_type=F32)
    x1 = x_ref[0] + ga_ref[0] * o
    h = (x1 * _rms(x1, x1.shape[-1])) * g_ref[...]
    return x1, h * (1.0 + sc_ref[0]) + sh_ref[0]


def _outproj_kernel(x_ref, oa_ref, of_ref, op_ref, on_ref, w_ref, ga_ref, g_ref, sh_ref, sc_ref,
                    wrh_ref, wrl_ref, x1_ref, h2_ref, rt_ref, *, n_exp):
    x1, h = _mix_residual(x_ref, (oa_ref, of_ref, op_ref, on_ref), w_ref, ga_ref, g_ref, sh_ref, sc_ref)
    x1_ref[0] = x1
    h2_ref[0] = h
    lane = lax.broadcasted_iota(jnp.int32, (1, LANES), 1)
    hh = h.astype(BF16)
    hl = (h - hh.astype(F32)).astype(BF16)
    lg = (jnp.dot(hh, wrh_ref[...], preferred_element_type=F32)
          + (jnp.dot(hl, wrh_ref[...], preferred_element_type=F32)
             + jnp.dot(hh, wrl_ref[...], preferred_element_type=F32)))
    lg = jnp.where(lane < n_exp, lg, NEG_INF)
    v1 = jnp.max(lg, -1, keepdims=True)
    i1 = jnp.min(jnp.where(lg == v1, lane, LANES), -1, keepdims=True)
    lg2 = jnp.where(lane == i1, NEG_INF, lg)
    v2 = jnp.max(lg2, -1, keepdims=True)
    i2 = jnp.min(jnp.where(lg2 == v2, lane, LANES), -1, keepdims=True)
    e2 = jnp.exp(v2 - v1)
    g1 = 1.0 / (1.0 + e2)
    g2 = e2 / (1.0 + e2)
    rt_ref[0] = jnp.where(lane == 0, i1.astype(F32),
                          jnp.where(lane == 1, i2.astype(F32),
                                    jnp.where(lane == 2, g1, jnp.where(lane == 3, g2, 0.0))))


def _mix_specs(x, oa, w_out, g, tm):
    b, n, d = x.shape
    tok = lambda w: pl.BlockSpec((1, tm, w), lambda i, j: (i, j, 0))
    per_b = pl.BlockSpec((1, 1, d), lambda i, j: (i, 0, 0))
    full = lambda a: pl.BlockSpec(a.shape, lambda i, j: (0,) * a.ndim, pipeline_mode=pl.Buffered(1))
    gw = oa.shape[-1]
    specs = [tok(d), tok(gw), tok(gw), tok(gw), tok(gw), full(w_out), per_b, full(g), per_b, per_b]
    return tok, per_b, full, specs


def _outproj(x, oa, of, op, on, w_out, ga, g, sh, sc, tm, w_router):
    b, n, d = x.shape
    tok, per_b, full, specs = _mix_specs(x, oa, w_out, g, tm)
    n_exp = w_router.shape[-1]
    wr = jnp.pad(w_router.astype(F32), ((0, 0), (0, LANES - n_exp)))
    wr_hi = wr.astype(BF16)
    wr_lo = (wr - wr_hi.astype(F32)).astype(BF16)
    return pl.pallas_call(
        functools.partial(_outproj_kernel, n_exp=n_exp),
        grid=(b, n // tm),
        in_specs=specs + [full(wr_hi), full(wr_lo)],
        out_specs=[tok(d), tok(d), tok(LANES)],
        out_shape=[jax.ShapeDtypeStruct((b, n, d), F32), jax.ShapeDtypeStruct((b, n, d), F32),
                   jax.ShapeDtypeStruct((b, n, LANES), F32)],
        compiler_params=_cparams(("parallel", "parallel"), VMEM_LIMIT),
        name="outproj",
    )(x, oa, of, op, on, w_out, ga, g, sh, sc, wr_hi, wr_lo)


def _outffn_kernel(x_ref, oa_ref, of_ref, op_ref, on_ref, w_ref, ga_ref, g_ref, sh_ref, sc_ref,
                   ga2_ref, w1_ref, w3_ref, w2_ref, o_ref, *, fc):
    x1, h = _mix_residual(x_ref, (oa_ref, of_ref, op_ref, on_ref), w_ref, ga_ref, g_ref, sh_ref, sc_ref)
    hb = h.astype(BF16)
    ff = w1_ref.shape[1]
    gated = []
    for j in range(ff // fc):
        a = jnp.dot(hb, w1_ref[:, j * fc:(j + 1) * fc], preferred_element_type=F32)
        bb = jnp.dot(hb, w3_ref[:, j * fc:(j + 1) * fc], preferred_element_type=F32)
        gated.append((_silu(a) * bb).astype(BF16))
    y = jnp.dot(jnp.concatenate(gated, axis=1), w2_ref[...], preferred_element_type=F32)
    o_ref[0] = x1 + ga2_ref[0] * y


def _outffn(x, oa, of, op, on, w_out, ga, g, sh, sc, ga2, w1, w3, w2, tm):
    b, n, d = x.shape
    ff = w1.shape[1]
    fc = 256 if ff % 256 == 0 else ff
    tok, per_b, full, specs = _mix_specs(x, oa, w_out, g, tm)
    return pl.pallas_call(
        functools.partial(_outffn_kernel, fc=fc),
        grid=(b, n // tm),
        in_specs=specs + [per_b, full(w1), full(w3), full(w2)],
        out_specs=tok(d),
        out_shape=jax.ShapeDtypeStruct((b, n, d), F32),
        compiler_params=_cparams(("parallel", "parallel"), VMEM_LIMIT),
        name="outproj_ffn",
    )(x, oa, of, op, on, w_out, ga, g, sh, sc, ga2, w1, w3, w2)


def _expert_kernel(te_ref, tv_ref, rows_ref, h_hbm, w1_ref, w3_ref, w2_ref, y_hbm, xs_ref, ys_ref,
                   gsem, ssem, zsem, *, fc, n_tiles, n_slots, n_dump_tiles):
    i = pl.program_id(0)
    tme = xs_ref.shape[1]
    nf = w1_ref.shape[2] // fc
    valid = tv_ref[i] > 0
    prev_valid = jnp.logical_and(i > 0, tv_ref[jnp.maximum(i - 1, 0)] > 0)
    slot = i % 2
    prev = jnp.maximum(i - 1, 0)

    def token_of(v):
        if n_slots & (n_slots - 1) == 0:
            return jnp.bitwise_and(v, n_slots - 1)
        return lax.rem(v, n_slots)

    def gather_row(tile, r, sl):
        pltpu.make_async_copy(h_hbm.at[token_of(rows_ref[tile * tme + r])], xs_ref.at[sl, r],
                              gsem.at[sl]).start()

    def scatter_row(r, sl):
        pltpu.make_async_copy(ys_ref.at[sl, r], y_hbm.at[rows_ref[prev * tme + r]], ssem.at[sl]).start()

    def wait_gather(sl):
        pltpu.make_async_copy(h_hbm.at[pl.ds(0, tme)], xs_ref.at[sl], gsem.at[sl]).wait()

    def wait_scatter(sl):
        pltpu.make_async_copy(ys_ref.at[sl], y_hbm.at[pl.ds(0, tme)], ssem.at[sl]).wait()

    @pl.when(i == 0)
    def _():
        ys_ref[1] = jnp.zeros(ys_ref.shape[1:], F32)
        fills = [pltpu.make_async_copy(ys_ref.at[1], y_hbm.at[pl.ds(2 * n_slots + e * tme, tme)], zsem)
                 for e in range(n_dump_tiles)]
        for cp in fills:
            cp.start()
        for cp in fills:
            cp.wait()

        def issue(r, c):
            gather_row(0, r, 0)
            return c

        lax.fori_loop(0, tme, issue, 0, unroll=8)

    @pl.when(jnp.logical_or(i == 0, prev_valid))
    def _():
        wait_gather(slot)

    @pl.when(prev_valid)
    def _():
        wait_scatter(slot)

    @pl.when(valid)
    def _():
        nxt = jnp.minimum(i + 1, n_tiles - 1)
        xb = xs_ref[slot].astype(BF16)
        cuts = [(k * tme) // nf for k in range(nf + 1)]
        gated = []
        for j in range(nf):
            for r in range(cuts[j], cuts[j + 1]):
                gather_row(nxt, r, 1 - slot)
                scatter_row(r, 1 - slot)
            a = jnp.dot(xb, w1_ref[0, :, j * fc:(j + 1) * fc], preferred_element_type=F32)
            bb = jnp.dot(xb, w3_ref[0, :, j * fc:(j + 1) * fc], preferred_element_type=F32)
            gated.append((_silu(a) * bb).astype(BF16))
        ys_ref[slot] = jnp.dot(jnp.concatenate(gated, axis=1), w2_ref[0], preferred_element_type=F32)

    @pl.when(jnp.logical_and(jnp.logical_not(valid), prev_valid))
    def _():
        def issue(r, c):
            scatter_row(r, 1 - slot)
            return c

        lax.fori_loop(0, tme, issue, 0, unroll=8)
        wait_scatter(1 - slot)


def _combine_kernel(x_ref, y0_ref, y1_ref, ga_ref, rt_ref, o_ref):
    rt = rt_ref[0]
    y = rt[:, TOP_K:TOP_K + 1] * y0_ref[...] + rt[:, TOP_K + 1:TOP_K + 2] * y1_ref[...]
    o_ref[0] = x_ref[0] + ga_ref[0] * y


def _moe(h2, x1, ga, rt, w1, w3, w2, tme, tc):
    b, n_b, d = h2.shape
    n = b * n_b
    n_exp, _, ff = w1.shape
    fc = 512 if ff % 512 == 0 else ff
    e_idx = rt[..., 0:TOP_K].astype(jnp.int32).reshape(-1)
    counts = jnp.sum((e_idx[:, None] == jnp.arange(n_exp)[None, :]).astype(jnp.int32), axis=0)
    padded = ((counts + tme - 1) // tme) * tme
    ends = jnp.cumsum(padded)
    offs = ends - padded
    n_rows = n * TOP_K + n_exp * tme
    n_tiles = n_rows // tme
    tile_start = jnp.arange(n_tiles + 1, dtype=jnp.int32) * tme
    tvalid = (tile_start < ends[-1]).astype(jnp.int32)
    texp = jnp.minimum(jnp.sum((tile_start[:, None] >= ends[None, :]).astype(jnp.int32), axis=1), n_exp - 1)
    texp = jnp.where(tvalid > 0, texp, texp[jnp.maximum(ends[-1] // tme - 1, 0)])
    p_idx = jnp.arange(n_rows, dtype=jnp.int32)
    pair = jnp.arange(n * TOP_K, dtype=jnp.int32)
    row_e = jnp.repeat(texp[:n_tiles], tme)
    sel = row_e[:, None] == jnp.arange(n_exp)[None, :]
    per_row = lambda v: jnp.sum(jnp.where(sel, v[None, :], 0), axis=1)
    first_pad = per_row(offs + counts)
    pads_before = per_row(offs - (jnp.cumsum(counts) - counts))
    order = jnp.argsort(e_idx * (n * TOP_K) + pair).astype(jnp.int32)
    flat = order[jnp.clip(p_idx - pads_before, 0, n * TOP_K - 1)]
    dump = 2 * n + row_e * tme + jnp.clip(p_idx - first_pad, 0, tme - 1)
    rows = jnp.where(p_idx < first_pad, (flat % TOP_K) * n + flat // TOP_K, dump).astype(jnp.int32)
    wspec = lambda shape: pl.BlockSpec((1,) + shape, lambda i, te, tv, s: (te[i], 0, 0),
                                       pipeline_mode=pl.Buffered(1))
    ys = pl.pallas_call(
        functools.partial(_expert_kernel, fc=fc, n_tiles=n_tiles, n_slots=n, n_dump_tiles=n_exp),
        grid_spec=pltpu.PrefetchScalarGridSpec(
            num_scalar_prefetch=3,
            grid=(n_tiles + 1,),
            in_specs=[pl.BlockSpec(memory_space=pl.ANY), wspec((d, ff)), wspec((d, ff)), wspec((ff, d))],
            out_specs=pl.BlockSpec(memory_space=pl.ANY),
            scratch_shapes=[pltpu.VMEM((2, tme, d), F32), pltpu.VMEM((2, tme, d), F32),
                            pltpu.SemaphoreType.DMA((2,)), pltpu.SemaphoreType.DMA((2,)),
                            pltpu.SemaphoreType.DMA(())]),
        out_shape=jax.ShapeDtypeStruct((2 * n + n_exp * tme, d), F32),
        compiler_params=_cparams(("arbitrary",), VMEM_LIMIT),
        name="experts",
    )(texp, tvalid, rows, h2.reshape(n, d), w1, w3, w2)

    nb = n_b // tc
    tok = lambda w: pl.BlockSpec((1, tc, w), lambda i, j: (i, j, 0))
    return pl.pallas_call(
        _combine_kernel,
        grid=(b, nb),
        in_specs=[tok(d),
                  pl.BlockSpec((tc, d), lambda i, j: (i * nb + j, 0)),
                  pl.BlockSpec((tc, d), lambda i, j: (n // tc + i * nb + j, 0)),
                  pl.BlockSpec((1, 1, d), lambda i, j: (i, 0, 0)), tok(LANES)],
        out_specs=tok(d),
        out_shape=jax.ShapeDtypeStruct((b, n_b, d), F32),
        compiler_params=_cparams(("parallel", "parallel"), VMEM_LIMIT),
        name="moe_combine",
    )(x1, ys, ys, ga, rt)


def _pick_tile(n, pref):
    t = pref
    while n % t:
        t //= 2
    return t


def kernel(x, c, ctx, c_ctx, w_ada, b_ada, g_mix, g_ffn, w_in, w_out, g_cq, g_ckv, w_uq, w_ukv,
           g_mla_qn, g_mla_qr, g_mla_kn, g_mla_kr, w_fourier, w_pool, pool_scale, g_na_q, g_na_k,
           na_rpb, w1_dense, w3_dense, w2_dense, w_router, w1_moe, w3_moe, w2_moe):
    b, n, d = x.shape
    nctx = ctx.shape[1]
    depth = w_ada.shape[0]
    assert b <= 7 and n % GRID_W == 0

    cc = jnp.zeros((8, d), F32).at[:b].set(c).at[b].set(c_ctx)
    mods = _adaln(cc, w_ada, b_ada)
    cos_x, sin_x = _rope_tables(n)
    cos_c = jnp.ones((nctx, LANES), F32)
    sin_c = jnp.zeros((nctx, LANES), F32)
    sw = _rope_swap_perm()
    tm_x = _pick_tile(n, 512)
    tm_c = _pick_tile(nctx, 256)
    amax = lambda g: jnp.max(jnp.abs(g.astype(F32)))

    for l in range(depth):
        last = l == depth - 1
        mx = mods[l, :b].reshape(b, 1, 6 * d)
        mc = jnp.broadcast_to(mods[l, b].reshape(1, 1, 6 * d), (b, 1, 6 * d))
        part = lambda m, k: m[:, :, k * d:(k + 1) * d]

        bound_mla = (MLA_SCALE * LOG2E) * (MLA_NOPE * amax(g_mla_qn[l]) * amax(g_mla_kn[l])
                                           + MLA_ROPE * amax(g_mla_qr[l]) * amax(g_mla_kr[l]))
        qk_na = (NA_SCALE * LOG2E) * NA_HEAD_DIM * amax(g_na_q[l]) * amax(g_na_k[l])
        rpb_hi = jnp.maximum(jnp.max(na_rpb[l]).astype(F32), 0.0) * LOG2E
        rpb_lo = jnp.minimum(jnp.min(na_rpb[l]).astype(F32), 0.0) * LOG2E
        bound_na = qk_na + rpb_hi
        mla_bounded = 2.0 * bound_mla <= MAX_SOFTMAX_GAP_LOG2
        na_bounded = 2.0 * qk_na + rpb_hi - rpb_lo <= MAX_SOFTMAX_GAP_LOG2
        bounds = jnp.zeros((1, LANES), F32).at[0, 0].set(bound_mla).at[0, 1].set(bound_na)

        wts = [
            _prep_in_weights(w_in[l]),
            g_cq[l].reshape(1, -1), _prep_uq(w_uq[l]),
            _lane_vec([(g_mla_qn[l], 64), (g_mla_qr[l], 32), (None, 32)]),
            _lane_vec([(None, 64), (g_mla_qr[l][sw], 32), (None, 32)]),
            g_ckv[l].reshape(1, -1), _prep_ukv(w_ukv[l]),
            _lane_vec([(g_mla_kn[l], 64), (None, 64)]),
            _lane_vec([(None, 64), (g_mla_kr[l], 32), (None, 32)]),
            _lane_vec([(None, 64), (g_mla_kr[l][sw], 32), (None, 32)]),
            jnp.tile(g_na_q[l], NA_HEADS).reshape(1, -1), jnp.tile(g_na_k[l], NA_HEADS).reshape(1, -1),
        ]
        gmix = g_mix[l].reshape(1, d)
        qx, kx, vx, fx, px, nqx, nkx, nvx = _inproj(x, part(mx, 0), part(mx, 1), gmix, cos_x, sin_x,
                                                    bounds, wts, tm_x)
        qc, kc, vc, fc_, pc, nqc, nkc, nvc = _inproj(ctx, part(mc, 0), part(mc, 1), gmix, cos_c, sin_c,
                                                     bounds, wts, tm_c)

        wf = w_fourier[l].astype(BF16)
        cg = w_pool.shape[-1]
        wp_bd = jnp.zeros((len(POOL_WINDOWS) * cg,) * 2, F32)
        for gi in range(len(POOL_WINDOWS)):
            wp_bd = wp_bd.at[gi * cg:(gi + 1) * cg, gi * cg:(gi + 1) * cg].set(w_pool[l, gi])
        wp_bd = wp_bd.astype(BF16)
        ps = pool_scale[l].reshape(1, -1)
        w_out_l = w_out[l].astype(BF16)
        gffn = g_ffn[l].reshape(1, d)

        o_mla = _flash_pairs(qx, [(kx, vx), (kc, vc)], _pick_tile(n, 256), mla_bounded)
        o_f = _fourier_latent(fx, wf)
        o_p = _pool(px, wp_bd, ps, _pick_tile(n, 512))
        o_na = _na_latent(nqx, nkx, nvx, nkc, nvc, _na_bias_tables(na_rpb[l]), na_bounded)

        moe_layer = l % 2 == 1
        i = l // 2
        if moe_layer:
            w1, w3, w2 = w1_moe[i].astype(BF16), w3_moe[i].astype(BF16), w2_moe[i].astype(BF16)
            x1, h2, rt = _outproj(x, o_mla, o_f, o_p, o_na, w_out_l, part(mx, 2), gffn, part(mx, 3),
                                  part(mx, 4), _pick_tile(n, 256), w_router[i])
            x = _moe(h2, x1, part(mx, 5), rt, w1, w3, w2, tme=512, tc=256)
        else:
            w1, w3, w2 = w1_dense[i].astype(BF16), w3_dense[i].astype(BF16), w2_dense[i].astype(BF16)
            x = _outffn(x, o_mla, o_f, o_p, o_na, w_out_l, part(mx, 2), gffn, part(mx, 3), part(mx, 4),
                        part(mx, 5), w1, w3, w2, tm_x)

        if not last:
            oc_mla = _flash_pairs(qc, [(kc, vc)], tm_c, mla_bounded)
            oc_f = _fourier_dense(fc_, wf)
            oc_p = _pool(pc, wp_bd, ps, tm_c)
            oc_na = _flash_pairs(nqc, [(nkc, nvc)], tm_c, na_bounded)
            if moe_layer:
                c1, hc2, rtc = _outproj(ctx, oc_mla, oc_f, oc_p, oc_na, w_out_l, part(mc, 2), gffn,
                                        part(mc, 3), part(mc, 4), tm_c, w_router[i])
                ctx = _moe(hc2, c1, part(mc, 5), rtc, w1, w3, w2, tme=256, tc=256)
            else:
                ctx = _outffn(ctx, oc_mla, oc_f, oc_p, oc_na, w_out_l, part(mc, 2), gffn, part(mc, 3),
                              part(mc, 4), part(mc, 5), w1, w3, w2, tm_c)
    return x
```

```python
import functools
import math

import numpy as np
import jax
import jax.numpy as jnp
from jax import lax
from jax.experimental import pallas as pl
from jax.experimental.pallas import tpu as pltpu

F32 = jnp.float32
BF16 = jnp.bfloat16

GRID_W = 64
LANES = 128
EPS = 1e-6
NEG_INF = -1e30

MLA_HEADS = 4
MLA_NOPE = 64
MLA_ROPE = 32
MLA_V = 64
MLA_SCALE = (MLA_NOPE + MLA_ROPE) ** -0.5
MLA_SPARE_LANE = MLA_NOPE + MLA_ROPE
ROPE_BASE = 10000.0
LOG2E = 1.0 / math.log(2.0)
MAX_SOFTMAX_GAP_LOG2 = 120.0

FOURIER_GROUPS = 4
POOL_WINDOWS = (2, 4, 8, 16)
POOL_HALO = 8

NA_HEADS = 4
NA_HEAD_DIM = 64
NA_SCALE = NA_HEAD_DIM ** -0.5
NA_WIN_ROWS = 8
NA_WIN_COLS = 16
NA_QROWS = 8
NA_KROWS = 16
NA_BLOCKS_PER_STEP = 8

TOP_K = 2
VMEM_LIMIT = 56 * 1024 * 1024


def _cparams(sem, vmem=None):
    return pltpu.CompilerParams(dimension_semantics=sem, vmem_limit_bytes=vmem)


def _silu(a):
    return a / (1.0 + jnp.exp(-a))


def _rms(v, n):
    return lax.rsqrt(jnp.sum(v * v, axis=-1, keepdims=True) * (1.0 / n) + EPS)


def _adaln_kernel(c_ref, w_ref, b_ref, o_ref):
    o_ref[0] = jnp.dot(_silu(c_ref[...]), w_ref[0], preferred_element_type=F32) + b_ref[0]


def _adaln(cc, w_ada, b_ada):
    depth, d, d6 = w_ada.shape
    tn = 512
    return pl.pallas_call(
        _adaln_kernel,
        grid=(depth, d6 // tn),
        in_specs=[pl.BlockSpec((8, d), lambda l, j: (0, 0)),
                  pl.BlockSpec((1, d, tn), lambda l, j: (l, 0, j)),
                  pl.BlockSpec((1, 1, tn), lambda l, j: (l, 0, j))],
        out_specs=pl.BlockSpec((1, 8, tn), lambda l, j: (l, 0, j)),
        out_shape=jax.ShapeDtypeStruct((depth, 8, d6), F32),
        compiler_params=_cparams(("arbitrary", "arbitrary")),
        name="adaln",
    )(cc, w_ada, b_ada.reshape(depth, 1, d6))


IN_COLS = 1920


def _rope_swap_perm():
    j = np.arange(MLA_ROPE)
    return np.where((j % 16) < 8, j + 8, j - 8)


def _prep_in_weights(w_in_l):
    d = w_in_l.shape[0]
    s = [0, 256, 384, 416, 672, 928, 1184, 1440, 1696]
    cq, ckv, kr, f, p, nq, nk, nv = [w_in_l[:, s[i]:s[i + 1]] for i in range(8)]
    z = lambda n: jnp.zeros((d, n), w_in_l.dtype)
    krsw = kr[:, _rope_swap_perm()]
    return jnp.concatenate([cq, ckv, f, p, nq, nk, nv, z(64), kr, z(32), z(64), krsw, z(32)],
                           axis=1).astype(BF16)


def _prep_uq(w_uq_l):
    r = w_uq_l.shape[0]
    w = w_uq_l.reshape(r, MLA_HEADS, MLA_NOPE + MLA_ROPE)
    z = lambda n: jnp.zeros((r, MLA_HEADS, n), w.dtype)
    main = jnp.concatenate([w, z(32)], axis=-1)
    sw = jnp.concatenate([z(64), w[..., MLA_NOPE:][..., _rope_swap_perm()], z(32)], axis=-1)
    return jnp.concatenate([main.reshape(r, -1), sw.reshape(r, -1)], axis=1).astype(BF16)


def _prep_ukv(w_ukv_l):
    r = w_ukv_l.shape[0]
    w = w_ukv_l.reshape(r, MLA_HEADS, MLA_NOPE + MLA_V)
    z = jnp.zeros((r, 64), w.dtype)
    tiles = [jnp.concatenate([w[:, h, :MLA_NOPE], z], axis=1) for h in range(MLA_HEADS)]
    for h in range(MLA_HEADS):
        v = w[:, h, MLA_NOPE:]
        tiles.append(jnp.concatenate([v, z] if h % 2 == 0 else [z, v], axis=1))
    return jnp.concatenate(tiles, axis=1).astype(BF16)


def _lane_vec(parts):
    cols = [jnp.zeros((n,), F32) if a is None else a.astype(F32) for a, n in parts]
    return jnp.concatenate(cols).reshape(1, -1)


def _rope_tables(n_tokens):
    t = np.arange(n_tokens)
    pos = np.stack([t // GRID_W, t % GRID_W], axis=1).astype(np.float64)
    half = MLA_ROPE // 2
    inv = 1.0 / (ROPE_BASE ** (np.arange(0, half, 2, dtype=np.float64) / half))
    j = np.arange(MLA_ROPE)
    ang = pos[:, j // 16] * inv[j % 8]
    sign = np.where((j % 16) < 8, -1.0, 1.0)
    cos = np.ones((n_tokens, LANES), np.float32)
    sin = np.zeros((n_tokens, LANES), np.float32)
    cos[:, 64:96] = np.cos(ang)
    sin[:, 64:96] = np.sin(ang) * sign
    return jnp.asarray(cos), jnp.asarray(sin)


def _inproj_kernel(x_ref, sh_ref, sc_ref, g_ref, cos_ref, sin_ref, bound_ref, win_ref,
                   gcq_ref, wuq_ref, gq_ref, gqsw_ref,
                   gckv_ref, wukv_ref, gkn_ref, gkr_ref, gkrsw_ref, gnq_ref, gnk_ref,
                   q_ref, k_ref, v_ref, f_ref, p_ref, nq_ref, nk_ref, nv_ref):
    x = x_ref[0]
    d = x.shape[-1]
    h = (x * _rms(x, d)) * g_ref[...]
    h = h * (1.0 + sc_ref[0]) + sh_ref[0]
    u = jnp.dot(h.astype(BF16), win_ref[...], preferred_element_type=F32)
    cq, ckv = u[:, 0:256], u[:, 256:384]
    f_ref[0] = u[:, 384:640].astype(BF16)
    p_ref[0] = u[:, 640:896]
    nq, nk, nv = u[:, 896:1152], u[:, 1152:1408], u[:, 1408:1664]
    krb, krs = u[:, 1664:1792], u[:, 1792:1920]
    cos, sin = cos_ref[...], sin_ref[...]
    lane = lax.broadcasted_iota(jnp.int32, (1, LANES), 1)
    m_nope = lane < MLA_NOPE
    m_rope = jnp.logical_and(lane >= MLA_NOPE, lane < MLA_NOPE + MLA_ROPE)
    bound_mla, bound_na = bound_ref[:, 0:1], bound_ref[:, 1:2]

    cqn = (cq * _rms(cq, 256) * gcq_ref[...]).astype(BF16)
    qall = jnp.dot(cqn, wuq_ref[...], preferred_element_type=F32)
    for hd in range(MLA_HEADS):
        blk = qall[:, hd * LANES:(hd + 1) * LANES]
        sw = qall[:, (MLA_HEADS + hd) * LANES:(MLA_HEADS + hd + 1) * LANES]
        sq = blk * blk
        rn = lax.rsqrt(jnp.sum(jnp.where(m_nope, sq, 0.0), -1, keepdims=True) * (1.0 / MLA_NOPE) + EPS)
        rr = lax.rsqrt(jnp.sum(jnp.where(m_rope, sq, 0.0), -1, keepdims=True) * (1.0 / MLA_ROPE) + EPS)
        qh = blk * jnp.where(m_nope, rn, rr) * gq_ref[...]
        qs = sw * rr * gqsw_ref[...]
        q_ref[0, hd] = jnp.where(lane == MLA_SPARE_LANE, -bound_mla,
                                 (qh * cos + qs * sin) * (MLA_SCALE * LOG2E)).astype(BF16)

    ckvn = (ckv * _rms(ckv, 128) * gckv_ref[...]).astype(BF16)
    kvall = jnp.dot(ckvn, wukv_ref[...], preferred_element_type=F32)
    rkr = _rms(krb, MLA_ROPE)
    krot = (krb * rkr * gkr_ref[...]) * cos + (krs * rkr * gkrsw_ref[...]) * sin
    for hd in range(MLA_HEADS):
        blk = kvall[:, hd * LANES:(hd + 1) * LANES]
        k_ref[0, hd] = jnp.where(lane == MLA_SPARE_LANE, 1.0,
                                 blk * _rms(blk, MLA_NOPE) * gkn_ref[...] + krot).astype(BF16)
        vb = kvall[:, (MLA_HEADS + hd) * LANES:(MLA_HEADS + hd + 1) * LANES]
        one_lane = 64 if hd % 2 == 0 else 0
        v_ref[0, hd] = jnp.where(lane == one_lane, 1.0, vb).astype(BF16)

    lane2 = lax.broadcasted_iota(jnp.int32, (1, 2 * LANES), 1)

    def seg_rms(t):
        sq = t * t
        r = jnp.zeros_like(t)
        for s in range(NA_HEADS):
            m = (lane2 // NA_HEAD_DIM) == s
            ss = jnp.sum(jnp.where(m, sq, 0.0), -1, keepdims=True)
            r = jnp.where(m, lax.rsqrt(ss * (1.0 / NA_HEAD_DIM) + EPS), r)
        return r

    nqn = nq * seg_rms(nq) * gnq_ref[...] * (NA_SCALE * LOG2E)
    nkn = nk * seg_rms(nk) * gnk_ref[...]
    for hd in range(NA_HEADS):
        t0 = (hd // 2) * LANES
        own = (lane < 64) if hd % 2 == 0 else (lane >= 64)
        one_lane = 64 if hd % 2 == 0 else 0
        spare = lane == (LANES - 1 if hd % 2 == 0 else 0)
        nq_ref[0, hd] = jnp.where(own, nqn[:, t0:t0 + LANES], jnp.where(spare, -bound_na, 0.0)).astype(BF16)
        nk_ref[0, hd] = jnp.where(own, nkn[:, t0:t0 + LANES], jnp.where(spare, 1.0, 0.0)).astype(BF16)
        nv_ref[0, hd] = jnp.where(own, nv[:, t0:t0 + LANES],
                                  jnp.where(lane == one_lane, 1.0, 0.0)).astype(BF16)


def _inproj(x, sh, sc, g, cos, sin, bounds, wts, tm):
    b, n, d = x.shape
    heads = MLA_HEADS
    row = lambda a: pl.BlockSpec(a.shape, lambda i, j: (0,) * a.ndim)
    per_b = pl.BlockSpec((1, 1, d), lambda i, j: (i, 0, 0))
    tok = lambda w: pl.BlockSpec((1, tm, w), lambda i, j: (i, j, 0))
    hd_spec = pl.BlockSpec((1, heads, tm, LANES), lambda i, j: (i, 0, j, 0))
    tab = pl.BlockSpec((tm, LANES), lambda i, j: (j, 0))
    hshape = jax.ShapeDtypeStruct((b, heads, n, LANES), BF16)
    return pl.pallas_call(
        _inproj_kernel,
        grid=(b, n // tm),
        in_specs=[tok(d), per_b, per_b, row(g), tab, tab, row(bounds)] + [row(w) for w in wts],
        out_specs=[hd_spec, hd_spec, hd_spec, tok(256), tok(256), hd_spec, hd_spec, hd_spec],
        out_shape=[hshape, hshape, hshape,
                   jax.ShapeDtypeStruct((b, n, 256), BF16), jax.ShapeDtypeStruct((b, n, 256), F32),
                   hshape, hshape, hshape],
        compiler_params=_cparams(("parallel", "parallel"), VMEM_LIMIT),
        name="inproj",
    )(x, sh, sc, g, cos, sin, bounds, *wts)


_DN_T = (((1,), (1,)), ((), ()))


def _pair_output(accs):
    lane = lax.broadcasted_iota(jnp.int32, (1, LANES), 1)
    out = None
    for hh, acc in enumerate(accs):
        one_lane = 64 if hh == 0 else 0
        own = (lane < 64) if hh == 0 else (lane >= 64)
        o = jnp.where(own, acc / acc[:, one_lane:one_lane + 1], 0.0)
        out = o if out is None else out + o
    return out


def _flash_pair_kernel(q_ref, *refs, chunks, bounded):
    o_ref = refs[-1]
    tq = q_ref.shape[2]
    accs = []
    for hh in range(2):
        q = q_ref[0, hh]
        acc = jnp.zeros((tq, LANES), F32)
        m = jnp.full((tq, 1), NEG_INF, F32)
        for kset, start, size in chunks:
            ks = refs[2 * kset][0, hh, start:start + size, :]
            vs = refs[2 * kset + 1][0, hh, start:start + size, :]
            s = lax.dot_general(q, ks, _DN_T, preferred_element_type=F32)
            if bounded:
                acc = acc + jnp.dot(jnp.exp2(s).astype(BF16), vs, preferred_element_type=F32)
            else:
                m_new = jnp.maximum(m, jnp.max(s, axis=-1, keepdims=True))
                p = jnp.exp2(s - m_new).astype(BF16)
                acc = acc * jnp.exp2(m - m_new) + jnp.dot(p, vs, preferred_element_type=F32)
                m = m_new
        accs.append(acc)
    o_ref[0] = _pair_output(accs).astype(o_ref.dtype)


def _flash_pairs(q, kv_sets, tq, bounded, tk=2048):
    b, heads, lq, _ = q.shape
    chunks, ops, kv_specs = [], [], []
    for si, (k, v) in enumerate(kv_sets):
        lk = k.shape[2]
        step = tk if lk % tk == 0 else lk
        chunks += [(si, st, step) for st in range(0, lk, step)]
        ops += [k, v]
        kv_specs += [pl.BlockSpec((1, 2, lk, LANES), lambda i, p, j: (i, p, 0, 0))] * 2

    def call(flag):
        return pl.pallas_call(
            functools.partial(_flash_pair_kernel, chunks=tuple(chunks), bounded=flag),
            grid=(b, heads // 2, lq // tq),
            in_specs=[pl.BlockSpec((1, 2, tq, LANES), lambda i, p, j: (i, p, j, 0))] + kv_specs,
            out_specs=pl.BlockSpec((1, tq, LANES), lambda i, p, j: (i, j, p)),
            out_shape=jax.ShapeDtypeStruct((b, lq, heads * 64), BF16),
            compiler_params=_cparams(("parallel", "parallel", "parallel"), VMEM_LIMIT),
            name="flash_pairs" if flag else "flash_pairs_online",
        )

    return lax.cond(bounded, call(True), call(False), q, *ops)


def _table(a):
    return jnp.asarray(a, F32).astype(BF16)


def _dft_consts(n_tokens):
    rows = n_tokens // GRID_W
    r = np.arange(rows)
    ang_r = 2 * np.pi * np.outer(r, r) / rows
    a_mat = np.concatenate([np.cos(ang_r), -np.sin(ang_r)], axis=0)
    c = np.arange(GRID_W)
    ang_t = 2 * np.pi * np.outer(r, c) / n_tokens
    ang_c = 2 * np.pi * np.outer(c, c) / GRID_W
    cc, sc = np.cos(ang_c), np.sin(ang_c)
    m_b = np.block([[cc, sc], [-sc, cc]])
    return a_mat, np.cos(ang_t), np.sin(ang_t), m_b


def _channel_dft(n_tokens, width):
    cg = width // FOURIER_GROUPS
    j = np.arange(cg)
    ang = 2 * np.pi * np.outer(j, j) / cg
    eye = np.eye(FOURIER_GROUPS)
    scale = 1.0 / math.sqrt(n_tokens * cg)
    return np.kron(eye, np.cos(ang)) * scale, np.kron(eye, np.sin(ang)) * scale


def _fourier_rows_kernel(u_ref, a_ref, tc_ref, ts_ref, o_ref):
    z = jnp.dot(a_ref[...], u_ref[0], preferred_element_type=F32)
    r = z.shape[0] // 2
    zr, zi = z[:r], z[r:]
    c, s = tc_ref[...], ts_ref[...]
    o_ref[0, :r] = (zr * c + zi * s).astype(BF16)
    o_ref[0, r:] = (zi * c - zr * s).astype(BF16)


def _fourier_cols_kernel(z_ref, mb_ref, cs_ref, wf_ref, o_ref):
    kb = z_ref.shape[1]
    w = GRID_W
    cw = wf_ref.shape[0]
    ys = [jnp.dot(mb_ref[...], z_ref[0, t], preferred_element_type=F32) for t in range(kb)]
    yr = jnp.concatenate([y[:w] for y in ys], axis=0).astype(BF16)
    yi = jnp.concatenate([y[w:] for y in ys], axis=0).astype(BF16)
    f = (jnp.dot(yr, cs_ref[:cw], preferred_element_type=F32)
         + jnp.dot(yi, cs_ref[cw:], preferred_element_type=F32))
    out = jnp.dot(f.astype(BF16), wf_ref[...], preferred_element_type=F32)
    for t in range(kb):
        o_ref[0, t] = out[t * w:(t + 1) * w].astype(o_ref.dtype)


def _fourier_latent(uf, wf):
    b, n, cw = uf.shape
    rows = n // GRID_W
    a_mat, tw_c, tw_s, m_b = _dft_consts(n)
    cc, sc = _channel_dft(n, cw)
    wide = GRID_W * cw
    tn = 2048
    expand = lambda t: jnp.broadcast_to(jnp.asarray(t, F32)[:, :, None], (rows, GRID_W, cw)).reshape(rows, wide)
    z = pl.pallas_call(
        _fourier_rows_kernel,
        grid=(wide // tn, b),
        in_specs=[pl.BlockSpec((1, rows, tn), lambda j, i: (i, 0, j)),
                  pl.BlockSpec((2 * rows, rows), lambda j, i: (0, 0)),
                  pl.BlockSpec((rows, tn), lambda j, i: (0, j)),
                  pl.BlockSpec((rows, tn), lambda j, i: (0, j))],
        out_specs=pl.BlockSpec((1, 2 * rows, tn), lambda j, i: (i, 0, j)),
        out_shape=jax.ShapeDtypeStruct((b, 2 * rows, wide), BF16),
        compiler_params=_cparams(("parallel", "parallel")),
        name="fourier_rows",
    )(uf.reshape(b, rows, wide), _table(a_mat), expand(tw_c), expand(tw_s))
    z = z.reshape(b, 2, rows, GRID_W, cw).transpose(0, 2, 1, 3, 4).reshape(b, rows, 2 * GRID_W, cw)
    kb = 16
    y = pl.pallas_call(
        _fourier_cols_kernel,
        grid=(b, rows // kb),
        in_specs=[pl.BlockSpec((1, kb, 2 * GRID_W, cw), lambda i, j: (i, j, 0, 0)),
                  pl.BlockSpec((2 * GRID_W, 2 * GRID_W), lambda i, j: (0, 0)),
                  pl.BlockSpec((2 * cw, cw), lambda i, j: (0, 0)),
                  pl.BlockSpec((cw, cw), lambda i, j: (0, 0))],
        out_specs=pl.BlockSpec((1, kb, GRID_W, cw), lambda i, j: (i, j, 0, 0)),
        out_shape=jax.ShapeDtypeStruct((b, rows, GRID_W, cw), BF16),
        compiler_params=_cparams(("parallel", "parallel")),
        name="fourier_cols",
    )(z, _table(m_b), _table(np.concatenate([cc, sc], axis=0)), wf)
    return y.transpose(0, 2, 1, 3).reshape(b, n, cw)


def _fourier_dense_kernel(u_ref, cl_ref, sl_ref, cs_ref, wf_ref, o_ref):
    u = u_ref[0]
    cw = wf_ref.shape[0]
    a = jnp.dot(u, cs_ref[:cw], preferred_element_type=F32).astype(BF16)
    bb = jnp.dot(u, cs_ref[cw:], preferred_element_type=F32).astype(BF16)
    f = (jnp.dot(cl_ref[...], a, preferred_element_type=F32)
         - jnp.dot(sl_ref[...], bb, preferred_element_type=F32))
    o_ref[0] = jnp.dot(f.astype(BF16), wf_ref[...], preferred_element_type=F32).astype(o_ref.dtype)


def _fourier_dense(uf, wf):
    b, n, cw = uf.shape
    t = np.arange(n)
    ang = 2 * np.pi * np.outer(t, t) / n
    cc, sc = _channel_dft(n, cw)
    full = lambda shape: pl.BlockSpec(shape, lambda i: (0,) * len(shape))
    return pl.pallas_call(
        _fourier_dense_kernel,
        grid=(b,),
        in_specs=[pl.BlockSpec((1, n, cw), lambda i: (i, 0, 0)), full((n, n)), full((n, n)),
                  full((2 * cw, cw)), full((cw, cw))],
        out_specs=pl.BlockSpec((1, n, cw), lambda i: (i, 0, 0)),
        out_shape=jax.ShapeDtypeStruct((b, n, cw), BF16),
        compiler_params=_cparams(("parallel",)),
        name="fourier_dense",
    )(uf, _table(np.cos(ang)), _table(np.sin(ang)), _table(np.concatenate([cc, sc], axis=0)), wf)


def _pool_kernel(x_ref, prev_ref, next_ref, wp_ref, ps_ref, o_ref, pad_ref, s2_ref, s4_ref, s8_ref, *,
                 n_tokens):
    assert POOL_WINDOWS == (2, 4, 8, 16)
    i = pl.program_id(1)
    tm = x_ref.shape[1]
    h = POOL_HALO
    n = tm + 2 * h
    zeros = jnp.zeros((h, x_ref.shape[2]), F32)
    pad_ref[0:h] = zeros
    pad_ref[h:2 * h] = jnp.where(i > 0, prev_ref[0], 0.0)
    pad_ref[2 * h:2 * h + tm] = x_ref[0]
    pad_ref[2 * h + tm:3 * h + tm] = jnp.where(i < pl.num_programs(1) - 1, next_ref[0], 0.0)
    for ref in (s2_ref, s4_ref):
        ref[0:h] = zeros
        ref[h + n:2 * h + n] = zeros
    s2_ref[h:h + n] = pad_ref[h - 1:h - 1 + n] + pad_ref[h:h + n]
    s4_ref[h:h + n] = s2_ref[h - 1:h - 1 + n] + s2_ref[h + 1:h + 1 + n]
    s8_ref[h:h + n] = s4_ref[h - 2:h - 2 + n] + s4_ref[h + 2:h + 2 + n]
    lo = 2 * h
    x0 = pad_ref[lo:lo + tm]
    sums = [s2_ref[lo:lo + tm], s4_ref[lo:lo + tm], s8_ref[lo:lo + tm],
            s8_ref[lo - 4:lo - 4 + tm] + s8_ref[lo + 4:lo + 4 + tm]]
    t = i * tm + lax.broadcasted_iota(jnp.int32, (tm, 1), 0)
    grp = lax.broadcasted_iota(jnp.int32, (1, x0.shape[1]), 1) // (x0.shape[1] // len(POOL_WINDOWS))
    pooled = jnp.zeros_like(x0)
    for gi, w in enumerate(POOL_WINDOWS):
        cnt = (jnp.minimum(t + w // 2, n_tokens) - jnp.maximum(t - w // 2, 0)).astype(F32)
        pooled = jnp.where(grp == gi, sums[gi] * (1.0 / cnt), pooled)
    pooled = pooled - x0
    y = jnp.dot(pooled.astype(BF16), wp_ref[...], preferred_element_type=F32) * ps_ref[...]
    o_ref[0] = y.astype(o_ref.dtype)


def _pool(up, wp_bd, ps, tm):
    b, n, cw = up.shape
    h = POOL_HALO
    nblk = n // h
    return pl.pallas_call(
        functools.partial(_pool_kernel, n_tokens=n),
        grid=(b, n // tm),
        in_specs=[pl.BlockSpec((1, tm, cw), lambda i, j: (i, j, 0)),
                  pl.BlockSpec((1, h, cw), lambda i, j: (i, jnp.maximum(j * (tm // h) - 1, 0), 0)),
                  pl.BlockSpec((1, h, cw), lambda i, j: (i, jnp.minimum((j + 1) * (tm // h), nblk - 1), 0)),
                  pl.BlockSpec((cw, cw), lambda i, j: (0, 0)),
                  pl.BlockSpec((1, cw), lambda i, j: (0, 0))],
        out_specs=pl.BlockSpec((1, tm, cw), lambda i, j: (i, j, 0)),
        out_shape=jax.ShapeDtypeStruct((b, n, cw), BF16),
        scratch_shapes=[pltpu.VMEM((tm + 4 * h, cw), F32)] * 4,
        compiler_params=_cparams(("parallel", "parallel")),
        name="pool",
    )(up, up, up, wp_bd, ps)


def _na_row_cases():
    qr = np.arange(NA_QROWS)[:, None]
    kr = np.arange(NA_KROWS)[None, :]
    masked = 2 * NA_WIN_ROWS - 1
    cases = []
    for case in range(3):
        krel = kr - 4 * case
        r0 = qr - NA_WIN_ROWS // 2
        r0 = np.maximum(r0, 0) if case == 0 else (np.minimum(r0, 0) if case == 2 else r0)
        row_ok = (krel >= r0) & (krel < r0 + NA_WIN_ROWS)
        cases.append(np.where(row_ok, krel - qr + NA_WIN_ROWS - 1, masked))
    return cases


def _na_bias_kernel(lo_ref, hi_ref, o_ref):
    for case, ro in enumerate(_na_row_cases()):
        for qr in range(NA_QROWS):
            for j in range(NA_KROWS // 2):
                o_ref[case, 0, qr * GRID_W:(qr + 1) * GRID_W, j * LANES:(j + 1) * LANES] = (
                    lo_ref[0, int(ro[qr, 2 * j])] + hi_ref[0, int(ro[qr, 2 * j + 1])])


def _na_bias_tables(rpb):
    heads = rpb.shape[0]
    qc = np.arange(GRID_W)[:, None]
    kc = np.arange(GRID_W)[None, :]
    c0 = np.clip(qc - NA_WIN_COLS // 2, 0, GRID_W - NA_WIN_COLS)
    col_ok = (kc >= c0) & (kc < c0 + NA_WIN_COLS)
    co = np.clip(kc - qc + NA_WIN_COLS - 1, 0, 2 * NA_WIN_COLS - 2)
    pick_c = (co.reshape(1, -1) == np.arange(2 * NA_WIN_COLS - 1)[:, None]).astype(np.float32)
    by_col = jnp.einsum("hrc,cx->hrx", rpb.astype(F32) * LOG2E, jnp.asarray(pick_c),
                        precision=lax.Precision.HIGHEST).reshape(heads, -1, GRID_W, GRID_W)
    blocks = jnp.where(jnp.asarray(col_ok)[None, None], by_col, NEG_INF)
    blocks = jnp.concatenate([blocks, jnp.full((heads, 1, GRID_W, GRID_W), NEG_INF, F32)], axis=1)
    zeros = jnp.zeros_like(blocks)
    lo = jnp.concatenate([blocks, zeros], axis=-1)
    hi = jnp.concatenate([zeros, blocks], axis=-1)
    n_off = blocks.shape[1]
    blk = pl.BlockSpec((1, n_off, GRID_W, LANES), lambda h: (h, 0, 0, 0))
    return pl.pallas_call(
        _na_bias_kernel,
        grid=(heads,),
        in_specs=[blk, blk],
        out_specs=pl.BlockSpec((3, 1, NA_QROWS * GRID_W, NA_KROWS * GRID_W), lambda h: (0, h, 0, 0)),
        out_shape=jax.ShapeDtypeStruct((3, heads, NA_QROWS * GRID_W, NA_KROWS * GRID_W), F32),
        compiler_params=_cparams(("parallel",), VMEM_LIMIT),
        name="na_bias",
    )(lo, hi)


def _na_kernel(q_ref, k_ref, v_ref, kc_ref, vc_ref, bias_ref, o_ref, *, rows, blocks, bounded):
    nkeys = NA_KROWS * GRID_W
    tq = NA_QROWS * GRID_W
    for sub in range(blocks):
        rb = pl.program_id(2) * blocks + sub
        kr0 = jnp.clip(rb * NA_QROWS - NA_WIN_ROWS // 2, 0, rows - NA_KROWS)
        case = lax.shift_right_logical(rb * NA_QROWS - kr0, 2)
        start = pl.multiple_of(kr0 * GRID_W, GRID_W)
        accs = []
        for hh in range(2):
            q = q_ref[0, hh, sub * tq:(sub + 1) * tq, :]
            kb = k_ref[0, hh, pl.ds(start, nkeys), :]
            vb = v_ref[0, hh, pl.ds(start, nkeys), :]
            s = lax.dot_general(q, kb, _DN_T, preferred_element_type=F32) + bias_ref[case, hh]
            sc = lax.dot_general(q, kc_ref[0, hh], _DN_T, preferred_element_type=F32)
            if not bounded:
                m = jnp.maximum(jnp.max(s, -1, keepdims=True), jnp.max(sc, -1, keepdims=True))
                s, sc = s - m, sc - m
            accs.append(jnp.dot(jnp.exp2(s).astype(BF16), vb, preferred_element_type=F32)
                        + jnp.dot(jnp.exp2(sc).astype(BF16), vc_ref[0, hh], preferred_element_type=F32))
        o_ref[0, sub * tq:(sub + 1) * tq, :] = _pair_output(accs).astype(o_ref.dtype)


def _na_latent(nq, nk, nv, nkc, nvc, bias, bounded):
    b, heads, n, _ = nq.shape
    rows = n // GRID_W
    assert rows >= NA_KROWS and rows % NA_QROWS == 0
    blocks = NA_BLOCKS_PER_STEP
    while (rows // NA_QROWS) % blocks:
        blocks //= 2
    tq = NA_QROWS * GRID_W * blocks
    nctx = nkc.shape[2]
    res = pl.BlockSpec((1, 2, n, LANES), lambda p, i, j: (i, p, 0, 0))
    ctx = pl.BlockSpec((1, 2, nctx, LANES), lambda p, i, j: (i, p, 0, 0))
    bias_spec = pl.BlockSpec((3, 2) + bias.shape[2:], lambda p, i, j: (0, p, 0, 0),
                             pipeline_mode=pl.Buffered(1))

    def call(flag):
        return pl.pallas_call(
            functools.partial(_na_kernel, rows=rows, blocks=blocks, bounded=flag),
            grid=(heads // 2, b, n // tq),
            in_specs=[pl.BlockSpec((1, 2, tq, LANES), lambda p, i, j: (i, p, j, 0)), res, res, ctx, ctx,
                      bias_spec],
            out_specs=pl.BlockSpec((1, tq, LANES), lambda p, i, j: (i, j, p)),
            out_shape=jax.ShapeDtypeStruct((b, n, heads * 64), BF16),
            compiler_params=_cparams(("parallel", "parallel", "parallel"), VMEM_LIMIT),
            name="na_latent" if flag else "na_latent_rowmax",
        )

    return lax.cond(bounded, call(True), call(False), nq, nk, nv, nkc, nvc, bias)


def _mix_residual(x_ref, o_refs, w_ref, ga_ref, g_ref, sh_ref, sc_ref, rows=slice(None)):
    o = jnp.dot(jnp.concatenate([r[0, rows, :] for r in o_refs], axis=1), w_ref[...], preferred_element_type=F32)
    x1 = x_ref[0, rows, :] + ga_ref[0] * o
    h = (x1 * _rms(x1, x1.shape[-1])) * g_ref[...]
    return x1, h * (1.0 + sc_ref[0]) + sh_ref[0]


def _outproj_kernel(x_ref, oa_ref, of_ref, op_ref, on_ref, w_ref, ga_ref, g_ref, sh_ref, sc_ref,
                    wrh_ref, wrl_ref, x1_ref, h2_ref, rt_ref, *, n_exp):
    tm = x_ref.shape[1]
    sub = min(tm, 256)
    for k in range(tm // sub):
        _outproj_rows(x_ref, oa_ref, of_ref, op_ref, on_ref, w_ref, ga_ref, g_ref, sh_ref, sc_ref,
                      wrh_ref, wrl_ref, x1_ref, h2_ref, rt_ref, slice(k * sub, (k + 1) * sub), n_exp)


def _outproj_rows(x_ref, oa_ref, of_ref, op_ref, on_ref, w_ref, ga_ref, g_ref, sh_ref, sc_ref,
                  wrh_ref, wrl_ref, x1_ref, h2_ref, rt_ref, rows, n_exp):
    x1, h = _mix_residual(x_ref, (oa_ref, of_ref, op_ref, on_ref), w_ref, ga_ref, g_ref, sh_ref, sc_ref, rows)
    x1_ref[0, rows, :] = x1
    h2_ref[0, rows, :] = h
    lane = lax.broadcasted_iota(jnp.int32, (1, LANES), 1)
    hh = h.astype(BF16)
    hl = (h - hh.astype(F32)).astype(BF16)
    lg = (jnp.dot(hh, wrh_ref[...], preferred_element_type=F32)
          + (jnp.dot(hl, wrh_ref[...], preferred_element_type=F32)
             + jnp.dot(hh, wrl_ref[...], preferred_element_type=F32)))
    lg = jnp.where(lane < n_exp, lg, NEG_INF)
    v1 = jnp.max(lg, -1, keepdims=True)
    i1 = jnp.min(jnp.where(lg == v1, lane, LANES), -1, keepdims=True)
    lg2 = jnp.where(lane == i1, NEG_INF, lg)
    v2 = jnp.max(lg2, -1, keepdims=True)
    i2 = jnp.min(jnp.where(lg2 == v2, lane, LANES), -1, keepdims=True)
    e2 = jnp.exp(v2 - v1)
    g1 = 1.0 / (1.0 + e2)
    g2 = e2 / (1.0 + e2)
    rt_ref[0, rows, :] = jnp.where(lane == 0, i1.astype(F32),
                                   jnp.where(lane == 1, i2.astype(F32),
                                             jnp.where(lane == 2, g1, jnp.where(lane == 3, g2, 0.0))))


def _mix_specs(x, oa, w_out, g, tm):
    b, n, d = x.shape
    tok = lambda w: pl.BlockSpec((1, tm, w), lambda i, j: (i, j, 0))
    per_b = pl.BlockSpec((1, 1, d), lambda i, j: (i, 0, 0))
    full = lambda a: pl.BlockSpec(a.shape, lambda i, j: (0,) * a.ndim, pipeline_mode=pl.Buffered(1))
    gw = oa.shape[-1]
    specs = [tok(d), tok(gw), tok(gw), tok(gw), tok(gw), full(w_out), per_b, full(g), per_b, per_b]
    return tok, per_b, full, specs


def _outproj(x, oa, of, op, on, w_out, ga, g, sh, sc, tm, w_router):
    b, n, d = x.shape
    tok, per_b, full, specs = _mix_specs(x, oa, w_out, g, tm)
    n_exp = w_router.shape[-1]
    wr = jnp.pad(w_router.astype(F32), ((0, 0), (0, LANES - n_exp)))
    wr_hi = wr.astype(BF16)
    wr_lo = (wr - wr_hi.astype(F32)).astype(BF16)
    return pl.pallas_call(
        functools.partial(_outproj_kernel, n_exp=n_exp),
        grid=(b, n // tm),
        in_specs=specs + [full(wr_hi), full(wr_lo)],
        out_specs=[tok(d), tok(d), tok(LANES)],
        out_shape=[jax.ShapeDtypeStruct((b, n, d), F32), jax.ShapeDtypeStruct((b, n, d), F32),
                   jax.ShapeDtypeStruct((b, n, LANES), F32)],
        compiler_params=_cparams(("parallel", "parallel"), VMEM_LIMIT),
        name="outproj",
    )(x, oa, of, op, on, w_out, ga, g, sh, sc, wr_hi, wr_lo)


def _outffn_kernel(x_ref, oa_ref, of_ref, op_ref, on_ref, w_ref, ga_ref, g_ref, sh_ref, sc_ref,
                   ga2_ref, w1_ref, w3_ref, w2_ref, o_ref, *, fc):
    x1, h = _mix_residual(x_ref, (oa_ref, of_ref, op_ref, on_ref), w_ref, ga_ref, g_ref, sh_ref, sc_ref)
    hb = h.astype(BF16)
    ff = w1_ref.shape[1]
    gated = []
    for j in range(ff // fc):
        a = jnp.dot(hb, w1_ref[:, j * fc:(j + 1) * fc], preferred_element_type=F32)
        bb = jnp.dot(hb, w3_ref[:, j * fc:(j + 1) * fc], preferred_element_type=F32)
        gated.append((_silu(a) * bb).astype(BF16))
    y = jnp.dot(jnp.concatenate(gated, axis=1), w2_ref[...], preferred_element_type=F32)
    o_ref[0] = x1 + ga2_ref[0] * y


def _outffn(x, oa, of, op, on, w_out, ga, g, sh, sc, ga2, w1, w3, w2, tm):
    b, n, d = x.shape
    ff = w1.shape[1]
    fc = 256 if ff % 256 == 0 else ff
    tok, per_b, full, specs = _mix_specs(x, oa, w_out, g, tm)
    return pl.pallas_call(
        functools.partial(_outffn_kernel, fc=fc),
        grid=(b, n // tm),
        in_specs=specs + [per_b, full(w1), full(w3), full(w2)],
        out_specs=tok(d),
        out_shape=jax.ShapeDtypeStruct((b, n, d), F32),
        compiler_params=_cparams(("parallel", "parallel"), VMEM_LIMIT),
        name="outproj_ffn",
    )(x, oa, of, op, on, w_out, ga, g, sh, sc, ga2, w1, w3, w2)


def _expert_kernel(te_ref, tv_ref, rows_ref, h_hbm, w1_ref, w3_ref, w2_ref, y_hbm, xs_ref, ys_ref,
                   gsem, ssem, zsem, *, fc, n_tiles, n_slots, n_dump_tiles):
    i = pl.program_id(0)
    tme = xs_ref.shape[1]
    nf = w1_ref.shape[2] // fc
    valid = tv_ref[i] > 0
    prev_valid = jnp.logical_and(i > 0, tv_ref[jnp.maximum(i - 1, 0)] > 0)
    slot = i % 2
    prev = jnp.maximum(i - 1, 0)

    def token_of(v):
        if n_slots & (n_slots - 1) == 0:
            return jnp.bitwise_and(v, n_slots - 1)
        return lax.rem(v, n_slots)

    def gather_row(tile, r, sl):
        pltpu.make_async_copy(h_hbm.at[token_of(rows_ref[tile * tme + r])], xs_ref.at[sl, r],
                              gsem.at[sl]).start()

    def scatter_row(r, sl):
        pltpu.make_async_copy(ys_ref.at[sl, r], y_hbm.at[rows_ref[prev * tme + r]], ssem.at[sl]).start()

    def wait_gather(sl):
        pltpu.make_async_copy(h_hbm.at[pl.ds(0, tme)], xs_ref.at[sl], gsem.at[sl]).wait()

    def wait_scatter(sl):
        pltpu.make_async_copy(ys_ref.at[sl], y_hbm.at[pl.ds(0, tme)], ssem.at[sl]).wait()

    @pl.when(i == 0)
    def _():
        ys_ref[1] = jnp.zeros(ys_ref.shape[1:], F32)
        fills = [pltpu.make_async_copy(ys_ref.at[1], y_hbm.at[pl.ds(2 * n_slots + e * tme, tme)], zsem)
                 for e in range(n_dump_tiles)]
        for cp in fills:
            cp.start()
        for cp in fills:
            cp.wait()

        def issue(r, c):
            gather_row(0, r, 0)
            return c

        lax.fori_loop(0, tme, issue, 0, unroll=8)

    @pl.when(jnp.logical_or(i == 0, prev_valid))
    def _():
        wait_gather(slot)

    @pl.when(prev_valid)
    def _():
        wait_scatter(slot)

    @pl.when(valid)
    def _():
        nxt = jnp.minimum(i + 1, n_tiles - 1)
        xb = xs_ref[slot].astype(BF16)
        cuts = [(k * tme) // nf for k in range(nf + 1)]
        gated = []
        for j in range(nf):
            for r in range(cuts[j], cuts[j + 1]):
                gather_row(nxt, r, 1 - slot)
                scatter_row(r, 1 - slot)
            a = jnp.dot(xb, w1_ref[0, :, j * fc:(j + 1) * fc], preferred_element_type=F32)
            bb = jnp.dot(xb, w3_ref[0, :, j * fc:(j + 1) * fc], preferred_element_type=F32)
            gated.append((_silu(a) * bb).astype(BF16))
        ys_ref[slot] = jnp.dot(jnp.concatenate(gated, axis=1), w2_ref[0], preferred_element_type=F32)

    @pl.when(jnp.logical_and(jnp.logical_not(valid), prev_valid))
    def _():
        def issue(r, c):
            scatter_row(r, 1 - slot)
            return c

        lax.fori_loop(0, tme, issue, 0, unroll=8)
        wait_scatter(1 - slot)


def _combine_kernel(x_ref, y0_ref, y1_ref, ga_ref, rt_ref, o_ref):
    rt = rt_ref[0]
    y = rt[:, TOP_K:TOP_K + 1] * y0_ref[...] + rt[:, TOP_K + 1:TOP_K + 2] * y1_ref[...]
    o_ref[0] = x_ref[0] + ga_ref[0] * y


def _moe(h2, x1, ga, rt, w1, w3, w2, tme, tc):
    b, n_b, d = h2.shape
    n = b * n_b
    n_exp, _, ff = w1.shape
    fc = 512 if ff % 512 == 0 else ff
    e_idx = rt[..., 0:TOP_K].astype(jnp.int32).reshape(-1)
    counts = jnp.sum((e_idx[:, None] == jnp.arange(n_exp)[None, :]).astype(jnp.int32), axis=0)
    padded = ((counts + tme - 1) // tme) * tme
    ends = jnp.cumsum(padded)
    offs = ends - padded
    n_rows = n * TOP_K + n_exp * tme
    n_tiles = n_rows // tme
    tile_start = jnp.arange(n_tiles + 1, dtype=jnp.int32) * tme
    tvalid = (tile_start < ends[-1]).astype(jnp.int32)
    texp = jnp.minimum(jnp.sum((tile_start[:, None] >= ends[None, :]).astype(jnp.int32), axis=1), n_exp - 1)
    texp = jnp.where(tvalid > 0, texp, texp[jnp.maximum(ends[-1] // tme - 1, 0)])
    p_idx = jnp.arange(n_rows, dtype=jnp.int32)
    pair = jnp.arange(n * TOP_K, dtype=jnp.int32)
    row_e = jnp.repeat(texp[:n_tiles], tme)
    sel = row_e[:, None] == jnp.arange(n_exp)[None, :]
    per_row = lambda v: jnp.sum(jnp.where(sel, v[None, :], 0), axis=1)
    first_pad = per_row(offs + counts)
    pads_before = per_row(offs - (jnp.cumsum(counts) - counts))
    order = jnp.argsort(e_idx * (n * TOP_K) + pair).astype(jnp.int32)
    flat = order[jnp.clip(p_idx - pads_before, 0, n * TOP_K - 1)]
    dump = 2 * n + row_e * tme + jnp.clip(p_idx - first_pad, 0, tme - 1)
    rows = jnp.where(p_idx < first_pad, (flat % TOP_K) * n + flat // TOP_K, dump).astype(jnp.int32)
    wspec = lambda shape: pl.BlockSpec((1,) + shape, lambda i, te, tv, s: (te[i], 0, 0),
                                       pipeline_mode=pl.Buffered(1))
    ys = pl.pallas_call(
        functools.partial(_expert_kernel, fc=fc, n_tiles=n_tiles, n_slots=n, n_dump_tiles=n_exp),
        grid_spec=pltpu.PrefetchScalarGridSpec(
            num_scalar_prefetch=3,
            grid=(n_tiles + 1,),
            in_specs=[pl.BlockSpec(memory_space=pl.ANY), wspec((d, ff)), wspec((d, ff)), wspec((ff, d))],
            out_specs=pl.BlockSpec(memory_space=pl.ANY),
            scratch_shapes=[pltpu.VMEM((2, tme, d), F32), pltpu.VMEM((2, tme, d), F32),
                            pltpu.SemaphoreType.DMA((2,)), pltpu.SemaphoreType.DMA((2,)),
                            pltpu.SemaphoreType.DMA(())]),
        out_shape=jax.ShapeDtypeStruct((2 * n + n_exp * tme, d), F32),
        compiler_params=_cparams(("arbitrary",), VMEM_LIMIT),
        name="experts",
    )(texp, tvalid, rows, h2.reshape(n, d), w1, w3, w2)

    nb = n_b // tc
    tok = lambda w: pl.BlockSpec((1, tc, w), lambda i, j: (i, j, 0))
    return pl.pallas_call(
        _combine_kernel,
        grid=(b, nb),
        in_specs=[tok(d),
                  pl.BlockSpec((tc, d), lambda i, j: (i * nb + j, 0)),
                  pl.BlockSpec((tc, d), lambda i, j: (n // tc + i * nb + j, 0)),
                  pl.BlockSpec((1, 1, d), lambda i, j: (i, 0, 0)), tok(LANES)],
        out_specs=tok(d),
        out_shape=jax.ShapeDtypeStruct((b, n_b, d), F32),
        compiler_params=_cparams(("parallel", "parallel"), VMEM_LIMIT),
        name="moe_combine",
    )(x1, ys, ys, ga, rt)


def _pick_tile(n, pref):
    t = pref
    while n % t:
        t //= 2
    return t


def kernel(x, c, ctx, c_ctx, w_ada, b_ada, g_mix, g_ffn, w_in, w_out, g_cq, g_ckv, w_uq, w_ukv,
           g_mla_qn, g_mla_qr, g_mla_kn, g_mla_kr, w_fourier, w_pool, pool_scale, g_na_q, g_na_k,
           na_rpb, w1_dense, w3_dense, w2_dense, w_router, w1_moe, w3_moe, w2_moe):
    b, n, d = x.shape
    nctx = ctx.shape[1]
    depth = w_ada.shape[0]
    assert b <= 7 and n % GRID_W == 0

    cc = jnp.zeros((8, d), F32).at[:b].set(c).at[b].set(c_ctx)
    mods = _adaln(cc, w_ada, b_ada)
    cos_x, sin_x = _rope_tables(n)
    cos_c = jnp.ones((nctx, LANES), F32)
    sin_c = jnp.zeros((nctx, LANES), F32)
    sw = _rope_swap_perm()
    tm_x = _pick_tile(n, 512)
    tm_c = _pick_tile(nctx, 256)
    amax = lambda g: jnp.max(jnp.abs(g.astype(F32)))

    for l in range(depth):
        last = l == depth - 1
        mx = mods[l, :b].reshape(b, 1, 6 * d)
        mc = jnp.broadcast_to(mods[l, b].reshape(1, 1, 6 * d), (b, 1, 6 * d))
        part = lambda m, k: m[:, :, k * d:(k + 1) * d]

        bound_mla = (MLA_SCALE * LOG2E) * (MLA_NOPE * amax(g_mla_qn[l]) * amax(g_mla_kn[l])
                                           + MLA_ROPE * amax(g_mla_qr[l]) * amax(g_mla_kr[l]))
        qk_na = (NA_SCALE * LOG2E) * NA_HEAD_DIM * amax(g_na_q[l]) * amax(g_na_k[l])
        rpb_hi = jnp.maximum(jnp.max(na_rpb[l]).astype(F32), 0.0) * LOG2E
        rpb_lo = jnp.minimum(jnp.min(na_rpb[l]).astype(F32), 0.0) * LOG2E
        bound_na = qk_na + rpb_hi
        mla_bounded = 2.0 * bound_mla <= MAX_SOFTMAX_GAP_LOG2
        na_bounded = 2.0 * qk_na + rpb_hi - rpb_lo <= MAX_SOFTMAX_GAP_LOG2
        bounds = jnp.zeros((1, LANES), F32).at[0, 0].set(bound_mla).at[0, 1].set(bound_na)

        wts = [
            _prep_in_weights(w_in[l]),
            g_cq[l].reshape(1, -1), _prep_uq(w_uq[l]),
            _lane_vec([(g_mla_qn[l], 64), (g_mla_qr[l], 32), (None, 32)]),
            _lane_vec([(None, 64), (g_mla_qr[l][sw], 32), (None, 32)]),
            g_ckv[l].reshape(1, -1), _prep_ukv(w_ukv[l]),
            _lane_vec([(g_mla_kn[l], 64), (None, 64)]),
            _lane_vec([(None, 64), (g_mla_kr[l], 32), (None, 32)]),
            _lane_vec([(None, 64), (g_mla_kr[l][sw], 32), (None, 32)]),
            jnp.tile(g_na_q[l], NA_HEADS).reshape(1, -1), jnp.tile(g_na_k[l], NA_HEADS).reshape(1, -1),
        ]
        gmix = g_mix[l].reshape(1, d)
        qx, kx, vx, fx, px, nqx, nkx, nvx = _inproj(x, part(mx, 0), part(mx, 1), gmix, cos_x, sin_x,
                                                    bounds, wts, tm_x)
        qc, kc, vc, fc_, pc, nqc, nkc, nvc = _inproj(ctx, part(mc, 0), part(mc, 1), gmix, cos_c, sin_c,
                                                     bounds, wts, tm_c)

        wf = w_fourier[l].astype(BF16)
        cg = w_pool.shape[-1]
        wp_bd = jnp.zeros((len(POOL_WINDOWS) * cg,) * 2, F32)
        for gi in range(len(POOL_WINDOWS)):
            wp_bd = wp_bd.at[gi * cg:(gi + 1) * cg, gi * cg:(gi + 1) * cg].set(w_pool[l, gi])
        wp_bd = wp_bd.astype(BF16)
        ps = pool_scale[l].reshape(1, -1)
        w_out_l = w_out[l].astype(BF16)
        gffn = g_ffn[l].reshape(1, d)

        o_mla = _flash_pairs(qx, [(kx, vx), (kc, vc)], _pick_tile(n, 256), mla_bounded)
        o_f = _fourier_latent(fx, wf)
        o_p = _pool(px, wp_bd, ps, _pick_tile(n, 512))
        o_na = _na_latent(nqx, nkx, nvx, nkc, nvc, _na_bias_tables(na_rpb[l]), na_bounded)

        moe_layer = l % 2 == 1
        i = l // 2
        if moe_layer:
            w1, w3, w2 = w1_moe[i].astype(BF16), w3_moe[i].astype(BF16), w2_moe[i].astype(BF16)
            x1, h2, rt = _outproj(x, o_mla, o_f, o_p, o_na, w_out_l, part(mx, 2), gffn, part(mx, 3),
                                  part(mx, 4), tm_x, w_router[i])
            x = _moe(h2, x1, part(mx, 5), rt, w1, w3, w2, tme=512, tc=256)
        else:
            w1, w3, w2 = w1_dense[i].astype(BF16), w3_dense[i].astype(BF16), w2_dense[i].astype(BF16)
            x = _outffn(x, o_mla, o_f, o_p, o_na, w_out_l, part(mx, 2), gffn, part(mx, 3), part(mx, 4),
                        part(mx, 5), w1, w3, w2, tm_x)

        if not last:
            oc_mla = _flash_pairs(qc, [(kc, vc)], tm_c, mla_bounded)
            oc_f = _fourier_dense(fc_, wf)
            oc_p = _pool(pc, wp_bd, ps, tm_c)
            oc_na = _flash_pairs(nqc, [(nkc, nvc)], tm_c, na_bounded)
            if moe_layer:
                c1, hc2, rtc = _outproj(ctx, oc_mla, oc_f, oc_p, oc_na, w_out_l, part(mc, 2), gffn,
                                        part(mc, 3), part(mc, 4), tm_c, w_router[i])
                ctx = _moe(hc2, c1, part(mc, 5), rtc, w1, w3, w2, tme=256, tc=256)
            else:
                ctx = _outffn(ctx, oc_mla, oc_f, oc_p, oc_na, w_out_l, part(mc, 2), gffn, part(mc, 3),
                              part(mc, 4), part(mc, 5), w1, w3, w2, tm_c)
    return x
```

```python
import functools
import math

import numpy as np
import jax
import jax.numpy as jnp
from jax import lax
from jax.experimental import pallas as pl
from jax.experimental.pallas import tpu as pltpu

F32 = jnp.float32
BF16 = jnp.bfloat16

GRID_W = 64
LANES = 128
EPS = 1e-6
NEG_INF = -1e30

MLA_HEADS = 4
MLA_NOPE = 64
MLA_ROPE = 32
MLA_V = 64
MLA_SCALE = (MLA_NOPE + MLA_ROPE) ** -0.5
MLA_SPARE_LANE = MLA_NOPE + MLA_ROPE
ROPE_BASE = 10000.0
LOG2E = 1.0 / math.log(2.0)
MAX_SOFTMAX_GAP_LOG2 = 120.0

FOURIER_GROUPS = 4
POOL_WINDOWS = (2, 4, 8, 16)
POOL_HALO = 8

NA_HEADS = 4
NA_HEAD_DIM = 64
NA_SCALE = NA_HEAD_DIM ** -0.5
NA_WIN_ROWS = 8
NA_WIN_COLS = 16
NA_QROWS = 8
NA_KROWS = 16
NA_BLOCKS_PER_STEP = 8

TOP_K = 2
VMEM_LIMIT = 56 * 1024 * 1024


def _cparams(sem, vmem=None):
    return pltpu.CompilerParams(dimension_semantics=sem, vmem_limit_bytes=vmem)


def _silu(a):
    return a / (1.0 + jnp.exp(-a))


def _rms(v, n):
    return lax.rsqrt(jnp.sum(v * v, axis=-1, keepdims=True) * (1.0 / n) + EPS)


def _adaln_kernel(c_ref, w_ref, b_ref, o_ref):
    o_ref[0] = jnp.dot(_silu(c_ref[...]), w_ref[0], preferred_element_type=F32) + b_ref[0]


def _adaln(cc, w_ada, b_ada):
    depth, d, d6 = w_ada.shape
    tn = 512
    return pl.pallas_call(
        _adaln_kernel,
        grid=(depth, d6 // tn),
        in_specs=[pl.BlockSpec((8, d), lambda l, j: (0, 0)),
                  pl.BlockSpec((1, d, tn), lambda l, j: (l, 0, j)),
                  pl.BlockSpec((1, 1, tn), lambda l, j: (l, 0, j))],
        out_specs=pl.BlockSpec((1, 8, tn), lambda l, j: (l, 0, j)),
        out_shape=jax.ShapeDtypeStruct((depth, 8, d6), F32),
        compiler_params=_cparams(("arbitrary", "arbitrary")),
        name="adaln",
    )(cc, w_ada, b_ada.reshape(depth, 1, d6))


IN_COLS = 1920


def _rope_swap_perm():
    j = np.arange(MLA_ROPE)
    return np.where((j % 16) < 8, j + 8, j - 8)


def _prep_in_weights(w_in_l):
    d = w_in_l.shape[0]
    s = [0, 256, 384, 416, 672, 928, 1184, 1440, 1696]
    cq, ckv, kr, f, p, nq, nk, nv = [w_in_l[:, s[i]:s[i + 1]] for i in range(8)]
    z = lambda n: jnp.zeros((d, n), w_in_l.dtype)
    krsw = kr[:, _rope_swap_perm()]
    return jnp.concatenate([cq, ckv, f, p, nq, nk, nv, z(64), kr, z(32), z(64), krsw, z(32)],
                           axis=1).astype(BF16)


def _prep_uq(w_uq_l):
    r = w_uq_l.shape[0]
    w = w_uq_l.reshape(r, MLA_HEADS, MLA_NOPE + MLA_ROPE)
    z = lambda n: jnp.zeros((r, MLA_HEADS, n), w.dtype)
    main = jnp.concatenate([w, z(32)], axis=-1)
    sw = jnp.concatenate([z(64), w[..., MLA_NOPE:][..., _rope_swap_perm()], z(32)], axis=-1)
    return jnp.concatenate([main.reshape(r, -1), sw.reshape(r, -1)], axis=1).astype(BF16)


def _prep_ukv(w_ukv_l):
    r = w_ukv_l.shape[0]
    w = w_ukv_l.reshape(r, MLA_HEADS, MLA_NOPE + MLA_V)
    z = jnp.zeros((r, 64), w.dtype)
    tiles = [jnp.concatenate([w[:, h, :MLA_NOPE], z], axis=1) for h in range(MLA_HEADS)]
    for h in range(MLA_HEADS):
        v = w[:, h, MLA_NOPE:]
        tiles.append(jnp.concatenate([v, z] if h % 2 == 0 else [z, v], axis=1))
    return jnp.concatenate(tiles, axis=1).astype(BF16)


def _lane_vec(parts):
    cols = [jnp.zeros((n,), F32) if a is None else a.astype(F32) for a, n in parts]
    return jnp.concatenate(cols).reshape(1, -1)


def _rope_tables(n_tokens):
    t = np.arange(n_tokens)
    pos = np.stack([t // GRID_W, t % GRID_W], axis=1).astype(np.float64)
    half = MLA_ROPE // 2
    inv = 1.0 / (ROPE_BASE ** (np.arange(0, half, 2, dtype=np.float64) / half))
    j = np.arange(MLA_ROPE)
    ang = pos[:, j // 16] * inv[j % 8]
    sign = np.where((j % 16) < 8, -1.0, 1.0)
    cos = np.ones((n_tokens, LANES), np.float32)
    sin = np.zeros((n_tokens, LANES), np.float32)
    cos[:, 64:96] = np.cos(ang)
    sin[:, 64:96] = np.sin(ang) * sign
    return jnp.asarray(cos), jnp.asarray(sin)


def _inproj_kernel(x_ref, sh_ref, sc_ref, g_ref, cos_ref, sin_ref, bound_ref, win_ref,
                   gcq_ref, wuq_ref, gq_ref, gqsw_ref,
                   gckv_ref, wukv_ref, gkn_ref, gkr_ref, gkrsw_ref, gnq_ref, gnk_ref,
                   q_ref, k_ref, v_ref, f_ref, p_ref, nq_ref, nk_ref, nv_ref):
    x = x_ref[0]
    d = x.shape[-1]
    h = (x * _rms(x, d)) * g_ref[...]
    h = h * (1.0 + sc_ref[0]) + sh_ref[0]
    u = jnp.dot(h.astype(BF16), win_ref[...], preferred_element_type=F32)
    cq, ckv = u[:, 0:256], u[:, 256:384]
    f_ref[0] = u[:, 384:640].astype(BF16)
    p_ref[0] = u[:, 640:896]
    nq, nk, nv = u[:, 896:1152], u[:, 1152:1408], u[:, 1408:1664]
    krb, krs = u[:, 1664:1792], u[:, 1792:1920]
    cos, sin = cos_ref[...], sin_ref[...]
    lane = lax.broadcasted_iota(jnp.int32, (1, LANES), 1)
    m_nope = lane < MLA_NOPE
    m_rope = jnp.logical_and(lane >= MLA_NOPE, lane < MLA_NOPE + MLA_ROPE)
    bound_mla, bound_na = bound_ref[:, 0:1], bound_ref[:, 1:2]

    cqn = (cq * _rms(cq, 256) * gcq_ref[...]).astype(BF16)
    qall = jnp.dot(cqn, wuq_ref[...], preferred_element_type=F32)
    for hd in range(MLA_HEADS):
        blk = qall[:, hd * LANES:(hd + 1) * LANES]
        sw = qall[:, (MLA_HEADS + hd) * LANES:(MLA_HEADS + hd + 1) * LANES]
        sq = blk * blk
        rn = lax.rsqrt(jnp.sum(jnp.where(m_nope, sq, 0.0), -1, keepdims=True) * (1.0 / MLA_NOPE) + EPS)
        rr = lax.rsqrt(jnp.sum(jnp.where(m_rope, sq, 0.0), -1, keepdims=True) * (1.0 / MLA_ROPE) + EPS)
        qh = blk * jnp.where(m_nope, rn, rr) * gq_ref[...]
        qs = sw * rr * gqsw_ref[...]
        q_ref[0, hd] = jnp.where(lane == MLA_SPARE_LANE, -bound_mla,
                                 (qh * cos + qs * sin) * (MLA_SCALE * LOG2E)).astype(BF16)

    ckvn = (ckv * _rms(ckv, 128) * gckv_ref[...]).astype(BF16)
    kvall = jnp.dot(ckvn, wukv_ref[...], preferred_element_type=F32)
    rkr = _rms(krb, MLA_ROPE)
    krot = (krb * rkr * gkr_ref[...]) * cos + (krs * rkr * gkrsw_ref[...]) * sin
    for hd in range(MLA_HEADS):
        blk = kvall[:, hd * LANES:(hd + 1) * LANES]
        k_ref[0, hd] = jnp.where(lane == MLA_SPARE_LANE, 1.0,
                                 blk * _rms(blk, MLA_NOPE) * gkn_ref[...] + krot).astype(BF16)
        vb = kvall[:, (MLA_HEADS + hd) * LANES:(MLA_HEADS + hd + 1) * LANES]
        one_lane = 64 if hd % 2 == 0 else 0
        v_ref[0, hd] = jnp.where(lane == one_lane, 1.0, vb).astype(BF16)

    lane2 = lax.broadcasted_iota(jnp.int32, (1, 2 * LANES), 1)

    def seg_rms(t):
        sq = t * t
        r = jnp.zeros_like(t)
        for s in range(NA_HEADS):
            m = (lane2 // NA_HEAD_DIM) == s
            ss = jnp.sum(jnp.where(m, sq, 0.0), -1, keepdims=True)
            r = jnp.where(m, lax.rsqrt(ss * (1.0 / NA_HEAD_DIM) + EPS), r)
        return r

    nqn = nq * seg_rms(nq) * gnq_ref[...] * (NA_SCALE * LOG2E)
    nkn = nk * seg_rms(nk) * gnk_ref[...]
    for hd in range(NA_HEADS):
        t0 = (hd // 2) * LANES
        own = (lane < 64) if hd % 2 == 0 else (lane >= 64)
        one_lane = 64 if hd % 2 == 0 else 0
        spare = lane == (LANES - 1 if hd % 2 == 0 else 0)
        nq_ref[0, hd] = jnp.where(own, nqn[:, t0:t0 + LANES], jnp.where(spare, -bound_na, 0.0)).astype(BF16)
        nk_ref[0, hd] = jnp.where(own, nkn[:, t0:t0 + LANES], jnp.where(spare, 1.0, 0.0)).astype(BF16)
        nv_ref[0, hd] = jnp.where(own, nv[:, t0:t0 + LANES],
                                  jnp.where(lane == one_lane, 1.0, 0.0)).astype(BF16)


def _inproj(x, sh, sc, g, cos, sin, bounds, wts, tm):
    b, n, d = x.shape
    heads = MLA_HEADS
    row = lambda a: pl.BlockSpec(a.shape, lambda i, j: (0,) * a.ndim)
    per_b = pl.BlockSpec((1, 1, d), lambda i, j: (i, 0, 0))
    tok = lambda w: pl.BlockSpec((1, tm, w), lambda i, j: (i, j, 0))
    hd_spec = pl.BlockSpec((1, heads, tm, LANES), lambda i, j: (i, 0, j, 0))
    tab = pl.BlockSpec((tm, LANES), lambda i, j: (j, 0))
    hshape = jax.ShapeDtypeStruct((b, heads, n, LANES), BF16)
    return pl.pallas_call(
        _inproj_kernel,
        grid=(b, n // tm),
        in_specs=[tok(d), per_b, per_b, row(g), tab, tab, row(bounds)] + [row(w) for w in wts],
        out_specs=[hd_spec, hd_spec, hd_spec, tok(256), tok(256), hd_spec, hd_spec, hd_spec],
        out_shape=[hshape, hshape, hshape,
                   jax.ShapeDtypeStruct((b, n, 256), BF16), jax.ShapeDtypeStruct((b, n, 256), F32),
                   hshape, hshape, hshape],
        compiler_params=_cparams(("parallel", "parallel"), VMEM_LIMIT),
        name="inproj",
    )(x, sh, sc, g, cos, sin, bounds, *wts)


_DN_T = (((1,), (1,)), ((), ()))


def _pair_output(accs):
    lane = lax.broadcasted_iota(jnp.int32, (1, LANES), 1)
    out = None
    for hh, acc in enumerate(accs):
        one_lane = 64 if hh == 0 else 0
        own = (lane < 64) if hh == 0 else (lane >= 64)
        o = jnp.where(own, acc / acc[:, one_lane:one_lane + 1], 0.0)
        out = o if out is None else out + o
    return out


def _flash_pair_kernel(q_ref, *refs, chunks, bounded):
    o_ref = refs[-1]
    tq = q_ref.shape[2]
    accs = []
    for hh in range(2):
        q = q_ref[0, hh]
        acc = jnp.zeros((tq, LANES), F32)
        m = jnp.full((tq, 1), NEG_INF, F32)
        for kset, start, size in chunks:
            ks = refs[2 * kset][0, hh, start:start + size, :]
            vs = refs[2 * kset + 1][0, hh, start:start + size, :]
            s = lax.dot_general(q, ks, _DN_T, preferred_element_type=F32)
            if bounded:
                acc = acc + jnp.dot(jnp.exp2(s).astype(BF16), vs, preferred_element_type=F32)
            else:
                m_new = jnp.maximum(m, jnp.max(s, axis=-1, keepdims=True))
                p = jnp.exp2(s - m_new).astype(BF16)
                acc = acc * jnp.exp2(m - m_new) + jnp.dot(p, vs, preferred_element_type=F32)
                m = m_new
        accs.append(acc)
    o_ref[0] = _pair_output(accs).astype(o_ref.dtype)


def _flash_pairs(q, kv_sets, tq, bounded, tk=2048):
    b, heads, lq, _ = q.shape
    chunks, ops, kv_specs = [], [], []
    for si, (k, v) in enumerate(kv_sets):
        lk = k.shape[2]
        step = tk if lk % tk == 0 else lk
        chunks += [(si, st, step) for st in range(0, lk, step)]
        ops += [k, v]
        kv_specs += [pl.BlockSpec((1, 2, lk, LANES), lambda i, p, j: (i, p, 0, 0))] * 2

    def call(flag):
        return pl.pallas_call(
            functools.partial(_flash_pair_kernel, chunks=tuple(chunks), bounded=flag),
            grid=(b, heads // 2, lq // tq),
            in_specs=[pl.BlockSpec((1, 2, tq, LANES), lambda i, p, j: (i, p, j, 0))] + kv_specs,
            out_specs=pl.BlockSpec((1, tq, LANES), lambda i, p, j: (i, j, p)),
            out_shape=jax.ShapeDtypeStruct((b, lq, heads * 64), BF16),
            compiler_params=_cparams(("parallel", "parallel", "parallel"), VMEM_LIMIT),
            name="flash_pairs" if flag else "flash_pairs_online",
        )

    return lax.cond(bounded, call(True), call(False), q, *ops)


def _table(a):
    return jnp.asarray(a, F32).astype(BF16)


def _dft_consts(n_tokens):
    rows = n_tokens // GRID_W
    r = np.arange(rows)
    ang_r = 2 * np.pi * np.outer(r, r) / rows
    a_mat = np.concatenate([np.cos(ang_r), -np.sin(ang_r)], axis=0)
    c = np.arange(GRID_W)
    ang_t = 2 * np.pi * np.outer(r, c) / n_tokens
    ang_c = 2 * np.pi * np.outer(c, c) / GRID_W
    cc, sc = np.cos(ang_c), np.sin(ang_c)
    m_b = np.block([[cc, sc], [-sc, cc]])
    return a_mat, np.cos(ang_t), np.sin(ang_t), m_b


def _channel_dft(n_tokens, width):
    cg = width // FOURIER_GROUPS
    j = np.arange(cg)
    ang = 2 * np.pi * np.outer(j, j) / cg
    eye = np.eye(FOURIER_GROUPS)
    scale = 1.0 / math.sqrt(n_tokens * cg)
    return np.kron(eye, np.cos(ang)) * scale, np.kron(eye, np.sin(ang)) * scale


def _fourier_rows_kernel(u_ref, a_ref, tc_ref, ts_ref, o_ref):
    z = jnp.dot(a_ref[...], u_ref[0], preferred_element_type=F32)
    r = z.shape[0] // 2
    zr, zi = z[:r], z[r:]
    c, s = tc_ref[...], ts_ref[...]
    o_ref[0, :r] = (zr * c + zi * s).astype(BF16)
    o_ref[0, r:] = (zi * c - zr * s).astype(BF16)


def _fourier_cols_kernel(z_ref, mb_ref, cs_ref, wf_ref, o_ref):
    kb = z_ref.shape[1]
    w = GRID_W
    cw = wf_ref.shape[0]
    ys = [jnp.dot(mb_ref[...], z_ref[0, t], preferred_element_type=F32) for t in range(kb)]
    yr = jnp.concatenate([y[:w] for y in ys], axis=0).astype(BF16)
    yi = jnp.concatenate([y[w:] for y in ys], axis=0).astype(BF16)
    f = (jnp.dot(yr, cs_ref[:cw], preferred_element_type=F32)
         + jnp.dot(yi, cs_ref[cw:], preferred_element_type=F32))
    out = jnp.dot(f.astype(BF16), wf_ref[...], preferred_element_type=F32)
    for t in range(kb):
        o_ref[0, t] = out[t * w:(t + 1) * w].astype(o_ref.dtype)


def _fourier_latent(uf, wf):
    b, n, cw = uf.shape
    rows = n // GRID_W
    a_mat, tw_c, tw_s, m_b = _dft_consts(n)
    cc, sc = _channel_dft(n, cw)
    wide = GRID_W * cw
    tn = 4096
    expand = lambda t: jnp.broadcast_to(jnp.asarray(t, F32)[:, :, None], (rows, GRID_W, cw)).reshape(rows, wide)
    z = pl.pallas_call(
        _fourier_rows_kernel,
        grid=(wide // tn, b),
        in_specs=[pl.BlockSpec((1, rows, tn), lambda j, i: (i, 0, j)),
                  pl.BlockSpec((2 * rows, rows), lambda j, i: (0, 0)),
                  pl.BlockSpec((rows, tn), lambda j, i: (0, j)),
                  pl.BlockSpec((rows, tn), lambda j, i: (0, j))],
        out_specs=pl.BlockSpec((1, 2 * rows, tn), lambda j, i: (i, 0, j)),
        out_shape=jax.ShapeDtypeStruct((b, 2 * rows, wide), BF16),
        compiler_params=_cparams(("parallel", "parallel")),
        name="fourier_rows",
    )(uf.reshape(b, rows, wide), _table(a_mat), expand(tw_c), expand(tw_s))
    z = z.reshape(b, 2, rows, GRID_W, cw).transpose(0, 2, 1, 3, 4).reshape(b, rows, 2 * GRID_W, cw)
    kb = 16
    y = pl.pallas_call(
        _fourier_cols_kernel,
        grid=(b, rows // kb),
        in_specs=[pl.BlockSpec((1, kb, 2 * GRID_W, cw), lambda i, j: (i, j, 0, 0)),
                  pl.BlockSpec((2 * GRID_W, 2 * GRID_W), lambda i, j: (0, 0)),
                  pl.BlockSpec((2 * cw, cw), lambda i, j: (0, 0)),
                  pl.BlockSpec((cw, cw), lambda i, j: (0, 0))],
        out_specs=pl.BlockSpec((1, kb, GRID_W, cw), lambda i, j: (i, j, 0, 0)),
        out_shape=jax.ShapeDtypeStruct((b, rows, GRID_W, cw), BF16),
        compiler_params=_cparams(("parallel", "parallel")),
        name="fourier_cols",
    )(z, _table(m_b), _table(np.concatenate([cc, sc], axis=0)), wf)
    return y.transpose(0, 2, 1, 3).reshape(b, n, cw)


def _fourier_dense_kernel(u_ref, cl_ref, sl_ref, cs_ref, wf_ref, o_ref):
    u = u_ref[0]
    cw = wf_ref.shape[0]
    a = jnp.dot(u, cs_ref[:cw], preferred_element_type=F32).astype(BF16)
    bb = jnp.dot(u, cs_ref[cw:], preferred_element_type=F32).astype(BF16)
    f = (jnp.dot(cl_ref[...], a, preferred_element_type=F32)
         - jnp.dot(sl_ref[...], bb, preferred_element_type=F32))
    o_ref[0] = jnp.dot(f.astype(BF16), wf_ref[...], preferred_element_type=F32).astype(o_ref.dtype)


def _fourier_dense(uf, wf):
    b, n, cw = uf.shape
    t = np.arange(n)
    ang = 2 * np.pi * np.outer(t, t) / n
    cc, sc = _channel_dft(n, cw)
    full = lambda shape: pl.BlockSpec(shape, lambda i: (0,) * len(shape))
    return pl.pallas_call(
        _fourier_dense_kernel,
        grid=(b,),
        in_specs=[pl.BlockSpec((1, n, cw), lambda i: (i, 0, 0)), full((n, n)), full((n, n)),
                  full((2 * cw, cw)), full((cw, cw))],
        out_specs=pl.BlockSpec((1, n, cw), lambda i: (i, 0, 0)),
        out_shape=jax.ShapeDtypeStruct((b, n, cw), BF16),
        compiler_params=_cparams(("parallel",)),
        name="fourier_dense",
    )(uf, _table(np.cos(ang)), _table(np.sin(ang)), _table(np.concatenate([cc, sc], axis=0)), wf)


def _pool_kernel(x_ref, prev_ref, next_ref, wp_ref, ps_ref, o_ref, pad_ref, s2_ref, s4_ref, s8_ref, *,
                 n_tokens):
    assert POOL_WINDOWS == (2, 4, 8, 16)
    i = pl.program_id(1)
    tm = x_ref.shape[1]
    h = POOL_HALO
    n = tm + 2 * h
    zeros = jnp.zeros((h, x_ref.shape[2]), F32)
    pad_ref[0:h] = zeros
    pad_ref[h:2 * h] = jnp.where(i > 0, prev_ref[0], 0.0)
    pad_ref[2 * h:2 * h + tm] = x_ref[0]
    pad_ref[2 * h + tm:3 * h + tm] = jnp.where(i < pl.num_programs(1) - 1, next_ref[0], 0.0)
    for ref in (s2_ref, s4_ref):
        ref[0:h] = zeros
        ref[h + n:2 * h + n] = zeros
    s2_ref[h:h + n] = pad_ref[h - 1:h - 1 + n] + pad_ref[h:h + n]
    s4_ref[h:h + n] = s2_ref[h - 1:h - 1 + n] + s2_ref[h + 1:h + 1 + n]
    s8_ref[h:h + n] = s4_ref[h - 2:h - 2 + n] + s4_ref[h + 2:h + 2 + n]
    lo = 2 * h
    x0 = pad_ref[lo:lo + tm]
    sums = [s2_ref[lo:lo + tm], s4_ref[lo:lo + tm], s8_ref[lo:lo + tm],
            s8_ref[lo - 4:lo - 4 + tm] + s8_ref[lo + 4:lo + 4 + tm]]
    t = i * tm + lax.broadcasted_iota(jnp.int32, (tm, 1), 0)
    grp = lax.broadcasted_iota(jnp.int32, (1, x0.shape[1]), 1) // (x0.shape[1] // len(POOL_WINDOWS))
    pooled = jnp.zeros_like(x0)
    for gi, w in enumerate(POOL_WINDOWS):
        cnt = (jnp.minimum(t + w // 2, n_tokens) - jnp.maximum(t - w // 2, 0)).astype(F32)
        pooled = jnp.where(grp == gi, sums[gi] * (1.0 / cnt), pooled)
    pooled = pooled - x0
    y = jnp.dot(pooled.astype(BF16), wp_ref[...], preferred_element_type=F32) * ps_ref[...]
    o_ref[0] = y.astype(o_ref.dtype)


def _pool(up, wp_bd, ps, tm):
    b, n, cw = up.shape
    h = POOL_HALO
    nblk = n // h
    return pl.pallas_call(
        functools.partial(_pool_kernel, n_tokens=n),
        grid=(b, n // tm),
        in_specs=[pl.BlockSpec((1, tm, cw), lambda i, j: (i, j, 0)),
                  pl.BlockSpec((1, h, cw), lambda i, j: (i, jnp.maximum(j * (tm // h) - 1, 0), 0)),
                  pl.BlockSpec((1, h, cw), lambda i, j: (i, jnp.minimum((j + 1) * (tm // h), nblk - 1), 0)),
                  pl.BlockSpec((cw, cw), lambda i, j: (0, 0)),
                  pl.BlockSpec((1, cw), lambda i, j: (0, 0))],
        out_specs=pl.BlockSpec((1, tm, cw), lambda i, j: (i, j, 0)),
        out_shape=jax.ShapeDtypeStruct((b, n, cw), BF16),
        scratch_shapes=[pltpu.VMEM((tm + 4 * h, cw), F32)] * 4,
        compiler_params=_cparams(("parallel", "parallel")),
        name="pool",
    )(up, up, up, wp_bd, ps)


def _na_row_cases():
    qr = np.arange(NA_QROWS)[:, None]
    kr = np.arange(NA_KROWS)[None, :]
    masked = 2 * NA_WIN_ROWS - 1
    cases = []
    for case in range(3):
        krel = kr - 4 * case
        r0 = qr - NA_WIN_ROWS // 2
        r0 = np.maximum(r0, 0) if case == 0 else (np.minimum(r0, 0) if case == 2 else r0)
        row_ok = (krel >= r0) & (krel < r0 + NA_WIN_ROWS)
        cases.append(np.where(row_ok, krel - qr + NA_WIN_ROWS - 1, masked))
    return cases


def _na_bias_kernel(lo_ref, hi_ref, o_ref):
    for case, ro in enumerate(_na_row_cases()):
        for qr in range(NA_QROWS):
            for j in range(NA_KROWS // 2):
                o_ref[case, 0, qr * GRID_W:(qr + 1) * GRID_W, j * LANES:(j + 1) * LANES] = (
                    lo_ref[0, int(ro[qr, 2 * j])] + hi_ref[0, int(ro[qr, 2 * j + 1])])


def _na_bias_tables(rpb):
    heads = rpb.shape[0]
    qc = np.arange(GRID_W)[:, None]
    kc = np.arange(GRID_W)[None, :]
    c0 = np.clip(qc - NA_WIN_COLS // 2, 0, GRID_W - NA_WIN_COLS)
    col_ok = (kc >= c0) & (kc < c0 + NA_WIN_COLS)
    co = np.clip(kc - qc + NA_WIN_COLS - 1, 0, 2 * NA_WIN_COLS - 2)
    pick_c = (co.reshape(1, -1) == np.arange(2 * NA_WIN_COLS - 1)[:, None]).astype(np.float32)
    by_col = jnp.einsum("hrc,cx->hrx", rpb.astype(F32) * LOG2E, jnp.asarray(pick_c),
                        precision=lax.Precision.HIGHEST).reshape(heads, -1, GRID_W, GRID_W)
    blocks = jnp.where(jnp.asarray(col_ok)[None, None], by_col, NEG_INF)
    blocks = jnp.concatenate([blocks, jnp.full((heads, 1, GRID_W, GRID_W), NEG_INF, F32)], axis=1)
    zeros = jnp.zeros_like(blocks)
    lo = jnp.concatenate([blocks, zeros], axis=-1)
    hi = jnp.concatenate([zeros, blocks], axis=-1)
    n_off = blocks.shape[1]
    blk = pl.BlockSpec((1, n_off, GRID_W, LANES), lambda h: (h, 0, 0, 0))
    return pl.pallas_call(
        _na_bias_kernel,
        grid=(heads,),
        in_specs=[blk, blk],
        out_specs=pl.BlockSpec((3, 1, NA_QROWS * GRID_W, NA_KROWS * GRID_W), lambda h: (0, h, 0, 0)),
        out_shape=jax.ShapeDtypeStruct((3, heads, NA_QROWS * GRID_W, NA_KROWS * GRID_W), F32),
        compiler_params=_cparams(("parallel",), VMEM_LIMIT),
        name="na_bias",
    )(lo, hi)


def _na_kernel(q_ref, k_ref, v_ref, kc_ref, vc_ref, bias_ref, o_ref, *, rows, blocks, bounded):
    nkeys = NA_KROWS * GRID_W
    tq = NA_QROWS * GRID_W
    for sub in range(blocks):
        rb = pl.program_id(2) * blocks + sub
        kr0 = jnp.clip(rb * NA_QROWS - NA_WIN_ROWS // 2, 0, rows - NA_KROWS)
        case = lax.shift_right_logical(rb * NA_QROWS - kr0, 2)
        start = pl.multiple_of(kr0 * GRID_W, GRID_W)
        accs = []
        for hh in range(2):
            q = q_ref[0, hh, sub * tq:(sub + 1) * tq, :]
            kb = k_ref[0, hh, pl.ds(start, nkeys), :]
            vb = v_ref[0, hh, pl.ds(start, nkeys), :]
            s = lax.dot_general(q, kb, _DN_T, preferred_element_type=F32) + bias_ref[case, hh]
            sc = lax.dot_general(q, kc_ref[0, hh], _DN_T, preferred_element_type=F32)
            if not bounded:
                m = jnp.maximum(jnp.max(s, -1, keepdims=True), jnp.max(sc, -1, keepdims=True))
                s, sc = s - m, sc - m
            accs.append(jnp.dot(jnp.exp2(s).astype(BF16), vb, preferred_element_type=F32)
                        + jnp.dot(jnp.exp2(sc).astype(BF16), vc_ref[0, hh], preferred_element_type=F32))
        o_ref[0, sub * tq:(sub + 1) * tq, :] = _pair_output(accs).astype(o_ref.dtype)


def _na_latent(nq, nk, nv, nkc, nvc, bias, bounded):
    b, heads, n, _ = nq.shape
    rows = n // GRID_W
    assert rows >= NA_KROWS and rows % NA_QROWS == 0
    blocks = NA_BLOCKS_PER_STEP
    while (rows // NA_QROWS) % blocks:
        blocks //= 2
    tq = NA_QROWS * GRID_W * blocks
    nctx = nkc.shape[2]
    res = pl.BlockSpec((1, 2, n, LANES), lambda p, i, j: (i, p, 0, 0))
    ctx = pl.BlockSpec((1, 2, nctx, LANES), lambda p, i, j: (i, p, 0, 0))
    bias_spec = pl.BlockSpec((3, 2) + bias.shape[2:], lambda p, i, j: (0, p, 0, 0),
                             pipeline_mode=pl.Buffered(1))

    def call(flag):
        return pl.pallas_call(
            functools.partial(_na_kernel, rows=rows, blocks=blocks, bounded=flag),
            grid=(heads // 2, b, n // tq),
            in_specs=[pl.BlockSpec((1, 2, tq, LANES), lambda p, i, j: (i, p, j, 0)), res, res, ctx, ctx,
                      bias_spec],
            out_specs=pl.BlockSpec((1, tq, LANES), lambda p, i, j: (i, j, p)),
            out_shape=jax.ShapeDtypeStruct((b, n, heads * 64), BF16),
            compiler_params=_cparams(("parallel", "parallel", "parallel"), VMEM_LIMIT),
            name="na_latent" if flag else "na_latent_rowmax",
        )

    return lax.cond(bounded, call(True), call(False), nq, nk, nv, nkc, nvc, bias)


def _mix_residual(x_ref, o_refs, w_ref, ga_ref, g_ref, sh_ref, sc_ref, rows=slice(None)):
    o = jnp.dot(jnp.concatenate([r[0, rows, :] for r in o_refs], axis=1), w_ref[...], preferred_element_type=F32)
    x1 = x_ref[0, rows, :] + ga_ref[0] * o
    h = (x1 * _rms(x1, x1.shape[-1])) * g_ref[...]
    return x1, h * (1.0 + sc_ref[0]) + sh_ref[0]


def _outproj_kernel(x_ref, oa_ref, of_ref, op_ref, on_ref, w_ref, ga_ref, g_ref, sh_ref, sc_ref,
                    wrh_ref, wrl_ref, x1_ref, h2_ref, rt_ref, *, n_exp):
    tm = x_ref.shape[1]
    sub = min(tm, 256)
    for k in range(tm // sub):
        _outproj_rows(x_ref, oa_ref, of_ref, op_ref, on_ref, w_ref, ga_ref, g_ref, sh_ref, sc_ref,
                      wrh_ref, wrl_ref, x1_ref, h2_ref, rt_ref, slice(k * sub, (k + 1) * sub), n_exp)


def _outproj_rows(x_ref, oa_ref, of_ref, op_ref, on_ref, w_ref, ga_ref, g_ref, sh_ref, sc_ref,
                  wrh_ref, wrl_ref, x1_ref, h2_ref, rt_ref, rows, n_exp):
    x1, h = _mix_residual(x_ref, (oa_ref, of_ref, op_ref, on_ref), w_ref, ga_ref, g_ref, sh_ref, sc_ref, rows)
    x1_ref[0, rows, :] = x1
    h2_ref[0, rows, :] = h
    lane = lax.broadcasted_iota(jnp.int32, (1, LANES), 1)
    hh = h.astype(BF16)
    hl = (h - hh.astype(F32)).astype(BF16)
    lg = (jnp.dot(hh, wrh_ref[...], preferred_element_type=F32)
          + (jnp.dot(hl, wrh_ref[...], preferred_element_type=F32)
             + jnp.dot(hh, wrl_ref[...], preferred_element_type=F32)))
    lg = jnp.where(lane < n_exp, lg, NEG_INF)
    v1 = jnp.max(lg, -1, keepdims=True)
    i1 = jnp.min(jnp.where(lg == v1, lane, LANES), -1, keepdims=True)
    lg2 = jnp.where(lane == i1, NEG_INF, lg)
    v2 = jnp.max(lg2, -1, keepdims=True)
    i2 = jnp.min(jnp.where(lg2 == v2, lane, LANES), -1, keepdims=True)
    e2 = jnp.exp(v2 - v1)
    g1 = 1.0 / (1.0 + e2)
    g2 = e2 / (1.0 + e2)
    rt_ref[0, rows, :] = jnp.where(lane == 0, i1.astype(F32),
                                   jnp.where(lane == 1, i2.astype(F32),
                                             jnp.where(lane == 2, g1, jnp.where(lane == 3, g2, 0.0))))


def _mix_specs(x, oa, w_out, g, tm):
    b, n, d = x.shape
    tok = lambda w: pl.BlockSpec((1, tm, w), lambda i, j: (i, j, 0))
    per_b = pl.BlockSpec((1, 1, d), lambda i, j: (i, 0, 0))
    full = lambda a: pl.BlockSpec(a.shape, lambda i, j: (0,) * a.ndim, pipeline_mode=pl.Buffered(1))
    gw = oa.shape[-1]
    specs = [tok(d), tok(gw), tok(gw), tok(gw), tok(gw), full(w_out), per_b, full(g), per_b, per_b]
    return tok, per_b, full, specs


def _outproj(x, oa, of, op, on, w_out, ga, g, sh, sc, tm, w_router):
    b, n, d = x.shape
    tok, per_b, full, specs = _mix_specs(x, oa, w_out, g, tm)
    n_exp = w_router.shape[-1]
    wr = jnp.pad(w_router.astype(F32), ((0, 0), (0, LANES - n_exp)))
    wr_hi = wr.astype(BF16)
    wr_lo = (wr - wr_hi.astype(F32)).astype(BF16)
    return pl.pallas_call(
        functools.partial(_outproj_kernel, n_exp=n_exp),
        grid=(b, n // tm),
        in_specs=specs + [full(wr_hi), full(wr_lo)],
        out_specs=[tok(d), tok(d), tok(LANES)],
        out_shape=[jax.ShapeDtypeStruct((b, n, d), F32), jax.ShapeDtypeStruct((b, n, d), F32),
                   jax.ShapeDtypeStruct((b, n, LANES), F32)],
        compiler_params=_cparams(("parallel", "parallel"), VMEM_LIMIT),
        name="outproj",
    )(x, oa, of, op, on, w_out, ga, g, sh, sc, wr_hi, wr_lo)


def _outffn_kernel(x_ref, oa_ref, of_ref, op_ref, on_ref, w_ref, ga_ref, g_ref, sh_ref, sc_ref,
                   ga2_ref, w1_ref, w3_ref, w2_ref, o_ref, *, fc):
    x1, h = _mix_residual(x_ref, (oa_ref, of_ref, op_ref, on_ref), w_ref, ga_ref, g_ref, sh_ref, sc_ref)
    hb = h.astype(BF16)
    ff = w1_ref.shape[1]
    gated = []
    for j in range(ff // fc):
        a = jnp.dot(hb, w1_ref[:, j * fc:(j + 1) * fc], preferred_element_type=F32)
        bb = jnp.dot(hb, w3_ref[:, j * fc:(j + 1) * fc], preferred_element_type=F32)
        gated.append((_silu(a) * bb).astype(BF16))
    y = jnp.dot(jnp.concatenate(gated, axis=1), w2_ref[...], preferred_element_type=F32)
    o_ref[0] = x1 + ga2_ref[0] * y


def _outffn(x, oa, of, op, on, w_out, ga, g, sh, sc, ga2, w1, w3, w2, tm):
    b, n, d = x.shape
    ff = w1.shape[1]
    fc = 256 if ff % 256 == 0 else ff
    tok, per_b, full, specs = _mix_specs(x, oa, w_out, g, tm)
    return pl.pallas_call(
        functools.partial(_outffn_kernel, fc=fc),
        grid=(b, n // tm),
        in_specs=specs + [per_b, full(w1), full(w3), full(w2)],
        out_specs=tok(d),
        out_shape=jax.ShapeDtypeStruct((b, n, d), F32),
        compiler_params=_cparams(("parallel", "parallel"), VMEM_LIMIT),
        name="outproj_ffn",
    )(x, oa, of, op, on, w_out, ga, g, sh, sc, ga2, w1, w3, w2)


def _expert_kernel(te_ref, tv_ref, rows_ref, h_hbm, w1_ref, w3_ref, w2_ref, y_hbm, xs_ref, ys_ref,
                   gsem, ssem, zsem, *, fc, n_tiles, n_slots, n_dump_tiles):
    i = pl.program_id(0)
    tme = xs_ref.shape[1]
    nf = w1_ref.shape[2] // fc
    valid = tv_ref[i] > 0
    prev_valid = jnp.logical_and(i > 0, tv_ref[jnp.maximum(i - 1, 0)] > 0)
    slot = i % 2
    prev = jnp.maximum(i - 1, 0)

    def token_of(v):
        if n_slots & (n_slots - 1) == 0:
            return jnp.bitwise_and(v, n_slots - 1)
        return lax.rem(v, n_slots)

    def gather_row(tile, r, sl):
        pltpu.make_async_copy(h_hbm.at[token_of(rows_ref[tile * tme + r])], xs_ref.at[sl, r],
                              gsem.at[sl]).start()

    def scatter_row(r, sl):
        pltpu.make_async_copy(ys_ref.at[sl, r], y_hbm.at[rows_ref[prev * tme + r]], ssem.at[sl]).start()

    def wait_gather(sl):
        pltpu.make_async_copy(h_hbm.at[pl.ds(0, tme)], xs_ref.at[sl], gsem.at[sl]).wait()

    def wait_scatter(sl):
        pltpu.make_async_copy(ys_ref.at[sl], y_hbm.at[pl.ds(0, tme)], ssem.at[sl]).wait()

    @pl.when(i == 0)
    def _():
        ys_ref[1] = jnp.zeros(ys_ref.shape[1:], F32)
        fills = [pltpu.make_async_copy(ys_ref.at[1], y_hbm.at[pl.ds(2 * n_slots + e * tme, tme)], zsem)
                 for e in range(n_dump_tiles)]
        for cp in fills:
            cp.start()
        for cp in fills:
            cp.wait()

        def issue(r, c):
            gather_row(0, r, 0)
            return c

        lax.fori_loop(0, tme, issue, 0, unroll=8)

    @pl.when(jnp.logical_or(i == 0, prev_valid))
    def _():
        wait_gather(slot)

    @pl.when(prev_valid)
    def _():
        wait_scatter(slot)

    @pl.when(valid)
    def _():
        nxt = jnp.minimum(i + 1, n_tiles - 1)
        xb = xs_ref[slot].astype(BF16)
        cuts = [(k * tme) // nf for k in range(nf + 1)]
        gated = []
        for j in range(nf):
            for r in range(cuts[j], cuts[j + 1]):
                gather_row(nxt, r, 1 - slot)
                scatter_row(r, 1 - slot)
            a = jnp.dot(xb, w1_ref[0, :, j * fc:(j + 1) * fc], preferred_element_type=F32)
            bb = jnp.dot(xb, w3_ref[0, :, j * fc:(j + 1) * fc], preferred_element_type=F32)
            gated.append((_silu(a) * bb).astype(BF16))
        ys_ref[slot] = jnp.dot(jnp.concatenate(gated, axis=1), w2_ref[0], preferred_element_type=F32)

    @pl.when(jnp.logical_and(jnp.logical_not(valid), prev_valid))
    def _():
        def issue(r, c):
            scatter_row(r, 1 - slot)
            return c

        lax.fori_loop(0, tme, issue, 0, unroll=8)
        wait_scatter(1 - slot)


def _combine_kernel(x_ref, y0_ref, y1_ref, ga_ref, rt_ref, o_ref):
    rt = rt_ref[0]
    y = rt[:, TOP_K:TOP_K + 1] * y0_ref[...] + rt[:, TOP_K + 1:TOP_K + 2] * y1_ref[...]
    o_ref[0] = x_ref[0] + ga_ref[0] * y


def _moe(h2, x1, ga, rt, w1, w3, w2, tme, tc):
    b, n_b, d = h2.shape
    n = b * n_b
    n_exp, _, ff = w1.shape
    fc = 512 if ff % 512 == 0 else ff
    e_idx = rt[..., 0:TOP_K].astype(jnp.int32).reshape(-1)
    counts = jnp.sum((e_idx[:, None] == jnp.arange(n_exp)[None, :]).astype(jnp.int32), axis=0)
    padded = ((counts + tme - 1) // tme) * tme
    ends = jnp.cumsum(padded)
    offs = ends - padded
    n_rows = n * TOP_K + n_exp * tme
    n_tiles = n_rows // tme
    tile_start = jnp.arange(n_tiles + 1, dtype=jnp.int32) * tme
    tvalid = (tile_start < ends[-1]).astype(jnp.int32)
    texp = jnp.minimum(jnp.sum((tile_start[:, None] >= ends[None, :]).astype(jnp.int32), axis=1), n_exp - 1)
    texp = jnp.where(tvalid > 0, texp, texp[jnp.maximum(ends[-1] // tme - 1, 0)])
    p_idx = jnp.arange(n_rows, dtype=jnp.int32)
    pair = jnp.arange(n * TOP_K, dtype=jnp.int32)
    row_e = jnp.repeat(texp[:n_tiles], tme)
    sel = row_e[:, None] == jnp.arange(n_exp)[None, :]
    per_row = lambda v: jnp.sum(jnp.where(sel, v[None, :], 0), axis=1)
    first_pad = per_row(offs + counts)
    pads_before = per_row(offs - (jnp.cumsum(counts) - counts))
    order = jnp.argsort(e_idx * (n * TOP_K) + pair).astype(jnp.int32)
    flat = order[jnp.clip(p_idx - pads_before, 0, n * TOP_K - 1)]
    dump = 2 * n + row_e * tme + jnp.clip(p_idx - first_pad, 0, tme - 1)
    rows = jnp.where(p_idx < first_pad, (flat % TOP_K) * n + flat // TOP_K, dump).astype(jnp.int32)
    wspec = lambda shape, depth=1: pl.BlockSpec((1,) + shape, lambda i, te, tv, s: (te[i], 0, 0),
                                                pipeline_mode=pl.Buffered(depth))
    ys = pl.pallas_call(
        functools.partial(_expert_kernel, fc=fc, n_tiles=n_tiles, n_slots=n, n_dump_tiles=n_exp),
        grid_spec=pltpu.PrefetchScalarGridSpec(
            num_scalar_prefetch=3,
            grid=(n_tiles + 1,),
            in_specs=[pl.BlockSpec(memory_space=pl.ANY), wspec((d, ff)), wspec((d, ff)), wspec((ff, d), 2)],
            out_specs=pl.BlockSpec(memory_space=pl.ANY),
            scratch_shapes=[pltpu.VMEM((2, tme, d), F32), pltpu.VMEM((2, tme, d), F32),
                            pltpu.SemaphoreType.DMA((2,)), pltpu.SemaphoreType.DMA((2,)),
                            pltpu.SemaphoreType.DMA(())]),
        out_shape=jax.ShapeDtypeStruct((2 * n + n_exp * tme, d), F32),
        compiler_params=_cparams(("arbitrary",), VMEM_LIMIT),
        name="experts",
    )(texp, tvalid, rows, h2.reshape(n, d), w1, w3, w2)

    nb = n_b // tc
    tok = lambda w: pl.BlockSpec((1, tc, w), lambda i, j: (i, j, 0))
    return pl.pallas_call(
        _combine_kernel,
        grid=(b, nb),
        in_specs=[tok(d),
                  pl.BlockSpec((tc, d), lambda i, j: (i * nb + j, 0)),
                  pl.BlockSpec((tc, d), lambda i, j: (n // tc + i * nb + j, 0)),
                  pl.BlockSpec((1, 1, d), lambda i, j: (i, 0, 0)), tok(LANES)],
        out_specs=tok(d),
        out_shape=jax.ShapeDtypeStruct((b, n_b, d), F32),
        compiler_params=_cparams(("parallel", "parallel"), VMEM_LIMIT),
        name="moe_combine",
    )(x1, ys, ys, ga, rt)


def _pick_tile(n, pref):
    t = pref
    while n % t:
        t //= 2
    return t


def kernel(x, c, ctx, c_ctx, w_ada, b_ada, g_mix, g_ffn, w_in, w_out, g_cq, g_ckv, w_uq, w_ukv,
           g_mla_qn, g_mla_qr, g_mla_kn, g_mla_kr, w_fourier, w_pool, pool_scale, g_na_q, g_na_k,
           na_rpb, w1_dense, w3_dense, w2_dense, w_router, w1_moe, w3_moe, w2_moe):
    b, n, d = x.shape
    nctx = ctx.shape[1]
    depth = w_ada.shape[0]
    assert b <= 7 and n % GRID_W == 0

    cc = jnp.zeros((8, d), F32).at[:b].set(c).at[b].set(c_ctx)
    mods = _adaln(cc, w_ada, b_ada)
    cos_x, sin_x = _rope_tables(n)
    cos_c = jnp.ones((nctx, LANES), F32)
    sin_c = jnp.zeros((nctx, LANES), F32)
    sw = _rope_swap_perm()
    tm_x = _pick_tile(n, 512)
    tm_c = _pick_tile(nctx, 256)
    amax = lambda g: jnp.max(jnp.abs(g.astype(F32)))

    for l in range(depth):
        last = l == depth - 1
        mx = mods[l, :b].reshape(b, 1, 6 * d)
        mc = jnp.broadcast_to(mods[l, b].reshape(1, 1, 6 * d), (b, 1, 6 * d))
        part = lambda m, k: m[:, :, k * d:(k + 1) * d]

        bound_mla = (MLA_SCALE * LOG2E) * (MLA_NOPE * amax(g_mla_qn[l]) * amax(g_mla_kn[l])
                                           + MLA_ROPE * amax(g_mla_qr[l]) * amax(g_mla_kr[l]))
        qk_na = (NA_SCALE * LOG2E) * NA_HEAD_DIM * amax(g_na_q[l]) * amax(g_na_k[l])
        rpb_hi = jnp.maximum(jnp.max(na_rpb[l]).astype(F32), 0.0) * LOG2E
        rpb_lo = jnp.minimum(jnp.min(na_rpb[l]).astype(F32), 0.0) * LOG2E
        bound_na = qk_na + rpb_hi
        mla_bounded = 2.0 * bound_mla <= MAX_SOFTMAX_GAP_LOG2
        na_bounded = 2.0 * qk_na + rpb_hi - rpb_lo <= MAX_SOFTMAX_GAP_LOG2
        bounds = jnp.zeros((1, LANES), F32).at[0, 0].set(bound_mla).at[0, 1].set(bound_na)

        wts = [
            _prep_in_weights(w_in[l]),
            g_cq[l].reshape(1, -1), _prep_uq(w_uq[l]),
            _lane_vec([(g_mla_qn[l], 64), (g_mla_qr[l], 32), (None, 32)]),
            _lane_vec([(None, 64), (g_mla_qr[l][sw], 32), (None, 32)]),
            g_ckv[l].reshape(1, -1), _prep_ukv(w_ukv[l]),
            _lane_vec([(g_mla_kn[l], 64), (None, 64)]),
            _lane_vec([(None, 64), (g_mla_kr[l], 32), (None, 32)]),
            _lane_vec([(None, 64), (g_mla_kr[l][sw], 32), (None, 32)]),
            jnp.tile(g_na_q[l], NA_HEADS).reshape(1, -1), jnp.tile(g_na_k[l], NA_HEADS).reshape(1, -1),
        ]
        gmix = g_mix[l].reshape(1, d)
        qx, kx, vx, fx, px, nqx, nkx, nvx = _inproj(x, part(mx, 0), part(mx, 1), gmix, cos_x, sin_x,
                                                    bounds, wts, tm_x)
        qc, kc, vc, fc_, pc, nqc, nkc, nvc = _inproj(ctx, part(mc, 0), part(mc, 1), gmix, cos_c, sin_c,
                                                     bounds, wts, tm_c)

        wf = w_fourier[l].astype(BF16)
        cg = w_pool.shape[-1]
        wp_bd = jnp.zeros((len(POOL_WINDOWS) * cg,) * 2, F32)
        for gi in range(len(POOL_WINDOWS)):
            wp_bd = wp_bd.at[gi * cg:(gi + 1) * cg, gi * cg:(gi + 1) * cg].set(w_pool[l, gi])
        wp_bd = wp_bd.astype(BF16)
        ps = pool_scale[l].reshape(1, -1)
        w_out_l = w_out[l].astype(BF16)
        gffn = g_ffn[l].reshape(1, d)

        o_mla = _flash_pairs(qx, [(kx, vx), (kc, vc)], _pick_tile(n, 256), mla_bounded)
        o_f = _fourier_latent(fx, wf)
        o_p = _pool(px, wp_bd, ps, _pick_tile(n, 1024))
        o_na = _na_latent(nqx, nkx, nvx, nkc, nvc, _na_bias_tables(na_rpb[l]), na_bounded)

        moe_layer = l % 2 == 1
        i = l // 2
        if moe_layer:
            w1, w3, w2 = w1_moe[i].astype(BF16), w3_moe[i].astype(BF16), w2_moe[i].astype(BF16)
            x1, h2, rt = _outproj(x, o_mla, o_f, o_p, o_na, w_out_l, part(mx, 2), gffn, part(mx, 3),
                                  part(mx, 4), tm_x, w_router[i])
            x = _moe(h2, x1, part(mx, 5), rt, w1, w3, w2, tme=512, tc=_pick_tile(n, 512))
        else:
            w1, w3, w2 = w1_dense[i].astype(BF16), w3_dense[i].astype(BF16), w2_dense[i].astype(BF16)
            x = _outffn(x, o_mla, o_f, o_p, o_na, w_out_l, part(mx, 2), gffn, part(mx, 3), part(mx, 4),
                        part(mx, 5), w1, w3, w2, tm_x)

        if not last:
            oc_mla = _flash_pairs(qc, [(kc, vc)], tm_c, mla_bounded)
            oc_f = _fourier_dense(fc_, wf)
            oc_p = _pool(pc, wp_bd, ps, tm_c)
            oc_na = _flash_pairs(nqc, [(nkc, nvc)], tm_c, na_bounded)
            if moe_layer:
                c1, hc2, rtc = _outproj(ctx, oc_mla, oc_f, oc_p, oc_na, w_out_l, part(mc, 2), gffn,
                                        part(mc, 3), part(mc, 4), tm_c, w_router[i])
                ctx = _moe(hc2, c1, part(mc, 5), rtc, w1, w3, w2, tme=256, tc=256)
            else:
                ctx = _outffn(ctx, oc_mla, oc_f, oc_p, oc_na, w_out_l, part(mc, 2), gffn, part(mc, 3),
                              part(mc, 4), part(mc, 5), w1, w3, w2, tm_c)
    return x
```
